```python
import math
import jax, jax.numpy as jnp
from jax import lax
import numpy as np

D_MODEL = 1024
BATCH = 8
SEQ = 2048
DEPTH = 2

CTX_LEN = 256
GRID_W = 64
D_FF = 2816
N_MOD = 9
EPS = 1e-6

RWKV_HEADS = 4
RWKV_HEAD_DIM = 64
D_RWKV = RWKV_HEADS * RWKV_HEAD_DIM
DECAY_LORA = 64
AAA_LORA = 64
GATE_LORA = 128
GN_EPS = 64e-5
NORM_EPS = 1e-12
D_CONV = 256
CONV_WIDTH = 31
DIFF_HEADS = 4
DIFF_QK_DIM = 64
DIFF_V_DIM = 2 * DIFF_QK_DIM
D_DIFF = DIFF_HEADS * DIFF_V_DIM
Q_BLOCK = 128
ROPE_THETA = 10000.0
AXIS_DIM = DIFF_QK_DIM // 2
ROPE_FREQS = AXIS_DIM // 2

D_MIX = D_RWKV + D_CONV + D_DIFF
RWKV_IN = 3 * D_RWKV + 2 * DECAY_LORA + 2 * AAA_LORA + GATE_LORA
CONV_IN = 2 * D_CONV
DIFF_IN = 2 * DIFF_HEADS * 2 * DIFF_QK_DIM + D_DIFF
P_IN = RWKV_IN + CONV_IN + DIFF_IN

kernel_name = 'hybrid_rwkv7_conformer_diffattn_dit'

f32 = jnp.float32


def rmsnorm(x, g, eps=EPS):
    xf = x.astype(f32)
    y = xf * lax.rsqrt(jnp.mean(xf * xf, axis=-1, keepdims=True) + eps)
    return (y * g.astype(f32)).astype(x.dtype)


def layernorm(x, g, b, eps):
    xf = x.astype(f32)
    mu = jnp.mean(xf, axis=-1, keepdims=True)
    xc = xf - mu
    var = jnp.mean(xc * xc, axis=-1, keepdims=True)
    return (xc * lax.rsqrt(var + eps) * g.astype(f32) + b.astype(f32)).astype(x.dtype)


def modulate(h, shift, scale):
    return h * (1.0 + scale) + shift


def swiglu(h, w_in, w_out):
    gate, up = jnp.split(h @ w_in, 2, axis=-1)
    return (jax.nn.silu(gate) * up) @ w_out


def to_heads(t):
    return t.reshape(t.shape[:-1] + (RWKV_HEADS, RWKV_HEAD_DIM))


def centred_shift(f, mu_prev, mu_next):
    zero = jnp.zeros_like(f[:, :1])
    prev = jnp.concatenate([zero, f[:, :-1]], axis=1)
    nxt = jnp.concatenate([f[:, 1:], zero], axis=1)
    return f + mu_prev * (prev - f) + mu_next * (nxt - f)


def rwkv_prepare(f, mu, w0, w2, a0, a2, g2, kk_scale, ka_scale):
    b, n, _ = f.shape
    f = centred_shift(f, mu[0], mu[1])
    r = f[..., 0:D_RWKV]
    k = f[..., D_RWKV:2 * D_RWKV]
    v = f[..., 2 * D_RWKV:3 * D_RWKV]
    o = 3 * D_RWKV
    wd = f[..., o:o + 2 * DECAY_LORA].reshape(b, n, 2, DECAY_LORA)
    o = o + 2 * DECAY_LORA
    ad = f[..., o:o + 2 * AAA_LORA].reshape(b, n, 2, AAA_LORA)
    o = o + 2 * AAA_LORA
    gd = f[..., o:o + GATE_LORA]
    w_raw = w0[:, None, None, :] + jnp.einsum('bndr,drc->dbnc', jnp.tanh(wd), w2)
    decay = jnp.exp(-jnp.exp(-jax.nn.softplus(-w_raw.astype(f32)) - 0.5))
    a = jax.nn.sigmoid((a0[:, None, None, :] + jnp.einsum('bndr,drc->dbnc', ad, a2)).astype(f32))
    g = jax.nn.sigmoid(gd) @ g2
    kk = to_heads((k * kk_scale).astype(f32))
    kk = kk * lax.rsqrt(jnp.sum(kk * kk, axis=-1, keepdims=True) + NORM_EPS)
    k_dir = k.astype(f32)[None] * (1.0 + (a - 1.0) * ka_scale.astype(f32))
    return (to_heads(r.astype(f32)), to_heads(v.astype(f32)), g, to_heads(k_dir),
            to_heads(decay), -kk, kk[None] * to_heads(a))


def wkv_scan(r, decay, k, v, avec, bvec, s0, reverse):
    xs = tuple(jnp.moveaxis(t, 1, 0) for t in (r, decay, k, v, avec, bvec))

    def step(s, inp):
        r_t, w_t, k_t, v_t, a_t, b_t = inp
        sa = jnp.einsum('bhij,bhj->bhi', s, a_t)
        s = s * w_t[:, :, None, :] + sa[..., None] * b_t[:, :, None, :] + v_t[..., None] * k_t[:, :, None, :]
        y = jnp.einsum('bhij,bhj->bhi', s, r_t)
        return s, y

    s_final, ys = lax.scan(step, s0, xs, reverse=reverse)
    return jnp.moveaxis(ys, 0, 1), s_final


def rwkv_finish(y, r, k2, v, g, rk, ln_g, ln_b):
    b, n = y.shape[:2]
    yn = layernorm(y, ln_g.reshape(RWKV_HEADS, RWKV_HEAD_DIM), ln_b.reshape(RWKV_HEADS, RWKV_HEAD_DIM), GN_EPS)
    bonus = jnp.sum(r * (k2[0] + k2[1]) * rk.astype(f32), axis=-1, keepdims=True) * v
    return ((yn + bonus).reshape(b, n, D_RWKV) * g.astype(f32)).astype(g.dtype)


def conv_module(f, dw_w, dw_b, ln_g, ln_b):
    val, gate = jnp.split(f, 2, axis=-1)
    h = val * jax.nn.sigmoid(gate)
    h = lax.conv_general_dilated(h, dw_w[:, None, :].astype(h.dtype), window_strides=(1,),
                                 padding=((CONV_WIDTH // 2, CONV_WIDTH // 2),),
                                 dimension_numbers=('NWC', 'WIO', 'NWC'),
                                 feature_group_count=D_CONV) + dw_b
    return jax.nn.silu(layernorm(h, ln_g, ln_b, 1e-5))


def rope_2d_tables(n_rows):
    row = jnp.repeat(jnp.arange(n_rows, dtype=jnp.int32), GRID_W)
    col = jnp.tile(jnp.arange(GRID_W, dtype=jnp.int32), n_rows)
    inv = 1.0 / (ROPE_THETA ** (jnp.arange(ROPE_FREQS, dtype=f32) * 2.0 / AXIS_DIM))
    ang = jnp.stack([row, col], axis=-1).astype(f32)[..., None] * inv
    return jnp.cos(ang), jnp.sin(ang)


def apply_rope_2d(t, cos, sin):
    ts = t.astype(f32).reshape(t.shape[:-1] + (2, 2, ROPE_FREQS))
    t1 = ts[..., 0, :]
    t2 = ts[..., 1, :]
    cs = cos[None, :, None, None]
    sn = sin[None, :, None, None]
    out = jnp.stack([t1 * cs - t2 * sn, t2 * cs + t1 * sn], axis=-2)
    return out.reshape(t.shape).astype(t.dtype)


def diff_qkv(f):
    b, n, _ = f.shape
    hq = DIFF_HEADS * 2 * DIFF_QK_DIM
    q = f[..., 0:hq].reshape(b, n, DIFF_HEADS, 2, DIFF_QK_DIM)
    k = f[..., hq:2 * hq].reshape(b, n, DIFF_HEADS, 2, DIFF_QK_DIM)
    v = f[..., 2 * hq:].reshape(b, n, DIFF_HEADS, DIFF_V_DIM)
    return q, k, v


def diff_attend(q, k, v, lam):
    s = jnp.einsum('bqhmd,bkhmd->bhmqk', q, k).astype(f32) * (DIFF_QK_DIM ** -0.5)
    p = jax.nn.softmax(s, axis=-1)
    attn = p[:, :, 0] - lam * p[:, :, 1]
    return jnp.einsum('bhqk,bkhe->bqhe', attn.astype(v.dtype), v)


def diff_heads_out(o, norm_g, lam_init):
    b, n = o.shape[:2]
    return (rmsnorm(o, norm_g, 1e-5) * (1.0 - lam_init)).reshape(b, n, D_DIFF)


def token_mix(hx, hc, layer, w_in, w_out, mu, w0, w2, a0, a2, g2, kk_s, ka_s, rk, lnx_g, lnx_b,
              dw_w, dw_b, cln_g, cln_b, lam_vecs, dnorm_g, cos, sin, need_ctx):
    b, n, _ = hx.shape
    fx = hx @ w_in
    fc = hc @ w_in
    ax, bx, cx = fx[..., :RWKV_IN], fx[..., RWKV_IN:RWKV_IN + CONV_IN], fx[..., RWKV_IN + CONV_IN:]
    ac, bc, cc = fc[..., :RWKV_IN], fc[..., RWKV_IN:RWKV_IN + CONV_IN], fc[..., RWKV_IN + CONV_IN:]

    r_x, v_x, g_x, k_x2, w_x2, av_x, bv_x2 = rwkv_prepare(ax, mu, w0, w2, a0, a2, g2, kk_s, ka_s)
    r_c, v_c, g_c, k_c2, w_c2, av_c, bv_c2 = rwkv_prepare(ac, mu, w0, w2, a0, a2, g2, kk_s, ka_s)
    s0 = jnp.zeros((b, RWKV_HEADS, RWKV_HEAD_DIM, RWKV_HEAD_DIM), f32)
    y_cf, s_cf = wkv_scan(r_c, w_c2[0], k_c2[0], v_c, av_c, bv_c2[0], s0, False)
    y_cb, s_cb = wkv_scan(r_c, w_c2[1], k_c2[1], v_c, av_c, bv_c2[1], s0, True)
    y_xf, _ = wkv_scan(r_x, w_x2[0], k_x2[0], v_x, av_x, bv_x2[0], s_cf, False)
    y_xb, _ = wkv_scan(r_x, w_x2[1], k_x2[1], v_x, av_x, bv_x2[1], s_cb, True)
    out_a_x = rwkv_finish(y_xf + y_xb, r_x, k_x2, v_x, g_x, rk, lnx_g, lnx_b)

    out_b_x = conv_module(bx, dw_w, dw_b, cln_g, cln_b)

    lam_init = 0.8 - 0.6 * math.exp(-0.3 * layer)
    lv = lam_vecs.astype(f32)
    lam = jnp.exp(jnp.sum(lv[0] * lv[1])) - jnp.exp(jnp.sum(lv[2] * lv[3])) + lam_init
    q_x, k_x, v_xa = diff_qkv(cx)
    q_c, k_c, v_ca = diff_qkv(cc)
    q_x = apply_rope_2d(q_x, cos, sin)
    k_x = apply_rope_2d(k_x, cos, sin)
    k_all = jnp.concatenate([k_x, k_c], axis=1)
    v_all = jnp.concatenate([v_xa, v_ca], axis=1)
    nb = n // Q_BLOCK
    qb = jnp.moveaxis(q_x.reshape(b, nb, Q_BLOCK, DIFF_HEADS, 2, DIFF_QK_DIM), 1, 0)
    ob = lax.map(lambda qq: diff_attend(qq, k_all, v_all, lam), qb)
    o_x = jnp.moveaxis(ob, 0, 1).reshape(b, n, DIFF_HEADS, DIFF_V_DIM)
    out_c_x = diff_heads_out(o_x, dnorm_g, lam_init)

    out_x = jnp.concatenate([out_a_x, out_b_x, out_c_x], axis=-1) @ w_out
    if not need_ctx:
        return out_x, None
    out_a_c = rwkv_finish(y_cf + y_cb, r_c, k_c2, v_c, g_c, rk, lnx_g, lnx_b)
    out_b_c = conv_module(bc, dw_w, dw_b, cln_g, cln_b)
    out_c_c = diff_heads_out(diff_attend(q_c, k_c, v_ca, lam), dnorm_g, lam_init)
    out_c = jnp.concatenate([out_a_c, out_b_c, out_c_c], axis=-1) @ w_out
    return out_x, out_c


def setup_inputs(seed: int = 0) -> dict:
    key = jax.random.key(seed)
    ks = jax.random.split(key, 32)

    def nrm(k, shape, scale):
        return jax.random.normal(k, shape, f32) * scale

    return {
        'x': nrm(ks[0], (BATCH, SEQ, D_MODEL), 1.0),
        'c': nrm(ks[1], (BATCH, D_MODEL), 1.0),
        'ctx': nrm(ks[2], (BATCH, CTX_LEN, D_MODEL), 1.0),
        'c_ctx': nrm(ks[3], (D_MODEL,), 1.0),
        'ada_w': nrm(ks[4], (DEPTH, D_MODEL, N_MOD * D_MODEL), 0.5 * D_MODEL ** -0.5),
        'ada_b': nrm(ks[5], (DEPTH, N_MOD * D_MODEL), 0.02),
        'norm_g': 1.0 + nrm(ks[6], (DEPTH, 3, D_MODEL), 0.02),
        'ffn_w_in': nrm(ks[7], (DEPTH, 2, D_MODEL, 2 * D_FF), D_MODEL ** -0.5),
        'ffn_w_out': nrm(ks[8], (DEPTH, 2, D_FF, D_MODEL), D_FF ** -0.5),
        'mix_w_in': nrm(ks[9], (DEPTH, D_MODEL, P_IN), D_MODEL ** -0.5),
        'mix_w_out': nrm(ks[10], (DEPTH, D_MIX, D_MODEL), D_MIX ** -0.5),
        'rwkv_mu': jax.random.uniform(ks[11], (DEPTH, 2, RWKV_IN), f32, 0.0, 0.5),
        'rwkv_w0': jax.random.uniform(ks[12], (DEPTH, 2, D_RWKV), f32, -6.0, 0.0),
        'rwkv_w2': nrm(ks[13], (DEPTH, 2, DECAY_LORA, D_RWKV), 0.5 * DECAY_LORA ** -0.5),
        'rwkv_a0': nrm(ks[14], (DEPTH, 2, D_RWKV), 0.1),
        'rwkv_a2': nrm(ks[15], (DEPTH, 2, AAA_LORA, D_RWKV), 0.5 * AAA_LORA ** -0.5),
        'rwkv_g2': nrm(ks[16], (DEPTH, GATE_LORA, D_RWKV), GATE_LORA ** -0.5),
        'rwkv_kk': 0.85 + nrm(ks[17], (DEPTH, D_RWKV), 0.02),
        'rwkv_ka': 1.0 + nrm(ks[18], (DEPTH, D_RWKV), 0.02),
        'rwkv_rk': nrm(ks[19], (DEPTH, RWKV_HEADS, RWKV_HEAD_DIM), 0.1),
        'rwkv_ln_g': 1.0 + nrm(ks[20], (DEPTH, D_RWKV), 0.02),
        'rwkv_ln_b': nrm(ks[21], (DEPTH, D_RWKV), 0.02),
        'conv_dw_w': nrm(ks[22], (DEPTH, CONV_WIDTH, D_CONV), CONV_WIDTH ** -0.5),
        'conv_dw_b': nrm(ks[23], (DEPTH, D_CONV), 0.02),
        'conv_ln_g': 1.0 + nrm(ks[24], (DEPTH, D_CONV), 0.02),
        'conv_ln_b': nrm(ks[25], (DEPTH, D_CONV), 0.02),
        'diff_lam': nrm(ks[26], (DEPTH, 4, DIFF_QK_DIM), 0.1),
        'diff_norm_g': 1.0 + nrm(ks[27], (DEPTH, DIFF_V_DIM), 0.02),
        'final_g': 1.0 + nrm(ks[28], (D_MODEL,), 0.02),
    }


def reference(x, c, ctx, c_ctx, ada_w, ada_b, norm_g, ffn_w_in, ffn_w_out, mix_w_in, mix_w_out,
              rwkv_mu, rwkv_w0, rwkv_w2, rwkv_a0, rwkv_a2, rwkv_g2, rwkv_kk, rwkv_ka, rwkv_rk,
              rwkv_ln_g, rwkv_ln_b, conv_dw_w, conv_dw_b, conv_ln_g, conv_ln_b,
              diff_lam, diff_norm_g, final_g):
    n_rows = x.shape[1] // GRID_W
    cos, sin = rope_2d_tables(n_rows)
    cond_x = jax.nn.silu(c)
    cond_c = jax.nn.silu(c_ctx)
    for l in range(DEPTH):
        need_ctx = l < DEPTH - 1
        ml = jnp.split((cond_x @ ada_w[l] + ada_b[l])[:, None, :], N_MOD, axis=-1)
        mc = jnp.split((cond_c @ ada_w[l] + ada_b[l])[None, None, :], N_MOD, axis=-1)
        x = x + 0.5 * ml[2] * swiglu(modulate(rmsnorm(x, norm_g[l, 0]), ml[0], ml[1]), ffn_w_in[l, 0], ffn_w_out[l, 0])
        ctx = ctx + 0.5 * mc[2] * swiglu(modulate(rmsnorm(ctx, norm_g[l, 0]), mc[0], mc[1]), ffn_w_in[l, 0], ffn_w_out[l, 0])
        hx = modulate(rmsnorm(x, norm_g[l, 1]), ml[3], ml[4])
        hc = modulate(rmsnorm(ctx, norm_g[l, 1]), mc[3], mc[4])
        ox, oc = token_mix(hx, hc, l, mix_w_in[l], mix_w_out[l], rwkv_mu[l], rwkv_w0[l], rwkv_w2[l],
                           rwkv_a0[l], rwkv_a2[l], rwkv_g2[l], rwkv_kk[l], rwkv_ka[l], rwkv_rk[l],
                           rwkv_ln_g[l], rwkv_ln_b[l], conv_dw_w[l], conv_dw_b[l], conv_ln_g[l], conv_ln_b[l],
                           diff_lam[l], diff_norm_g[l], cos, sin, need_ctx)
        x = x + ml[5] * ox
        x = x + 0.5 * ml[8] * swiglu(modulate(rmsnorm(x, norm_g[l, 2]), ml[6], ml[7]), ffn_w_in[l, 1], ffn_w_out[l, 1])
        if need_ctx:
            ctx = ctx + mc[5] * oc
            ctx = ctx + 0.5 * mc[8] * swiglu(modulate(rmsnorm(ctx, norm_g[l, 2]), mc[6], mc[7]), ffn_w_in[l, 1], ffn_w_out[l, 1])
    return rmsnorm(x, final_g)
```

```python
import functools
import math

import jax
import jax.numpy as jnp
from jax import lax
from jax.experimental import pallas as pl
from jax.experimental.pallas import tpu as pltpu

f32 = jnp.float32
bf16 = jnp.bfloat16
HI = lax.Precision.HIGHEST

N_MOD = 9
EPS = 1e-6
GRID_W = 64
RWKV_HEADS = 4
HEAD_DIM = 64
D_RWKV = RWKV_HEADS * HEAD_DIM
DECAY_LORA = 64
AAA_LORA = 64
GATE_LORA = 128
GN_EPS = 64e-5
NORM_EPS = 1e-12
D_CONV = 256
CONV_WIDTH = 31
CONV_HALF = CONV_WIDTH // 2
DIFF_HEADS = 4
DIFF_QK_DIM = 64
DIFF_V_DIM = 2 * DIFF_QK_DIM
D_DIFF = DIFF_HEADS * DIFF_V_DIM
ROPE_THETA = 10000.0
AXIS_DIM = DIFF_QK_DIM // 2
ROPE_FREQS = AXIS_DIM // 2
RWKV_IN = 3 * D_RWKV + 2 * DECAY_LORA + 2 * AAA_LORA + GATE_LORA
CONV_IN = 2 * D_CONV
D_QK = DIFF_HEADS * 2 * DIFF_QK_DIM
DIFF_IN = 2 * D_QK + D_DIFF
P_IN = RWKV_IN + CONV_IN + DIFF_IN

TM = 256
CHUNK = 64
HALO = 16
COND_ROWS = 16
VMEM_LIMIT = 56 * 1024 * 1024


def _dot(a, b, prec=None):
    return jnp.dot(a, b, preferred_element_type=f32, precision=prec)


def _dot_nt(a, b, prec=None):
    return lax.dot_general(a, b, (((1,), (1,)), ((), ())), preferred_element_type=f32, precision=prec)


def _dot_tn(a, b, prec=None):
    return lax.dot_general(a, b, (((0,), (0,)), ((), ())), preferred_element_type=f32, precision=prec)


def _sigmoid(x):
    return 1.0 / (1.0 + jnp.exp(-x))


def _cparams(n_axes):
    return pltpu.CompilerParams(dimension_semantics=("arbitrary",) * n_axes, vmem_limit_bytes=VMEM_LIMIT)


def _head_ones(n, width):
    r = lax.broadcasted_iota(jnp.int32, (n, n), 0) // width
    c = lax.broadcasted_iota(jnp.int32, (n, n), 1) // width
    return jnp.where(r == c, 1.0, 0.0).astype(f32)


def _adaln_kernel(c_ref, w_ref, b_ref, o_ref):
    cond = c_ref[...]
    cond = cond * _sigmoid(cond)
    o_ref[0] = _dot(cond.astype(bf16), w_ref[0].astype(bf16)) + b_ref[0]


def _adaln(cond, ada_w, ada_b):
    depth, d, nd = ada_w.shape
    tn = nd // 4
    return pl.pallas_call(
        _adaln_kernel,
        grid=(depth, nd // tn),
        in_specs=[pl.BlockSpec((COND_ROWS, d), lambda l, j: (0, 0)),
                  pl.BlockSpec((1, d, tn), lambda l, j: (l, 0, j)),
                  pl.BlockSpec((1, 1, tn), lambda l, j: (l, 0, j))],
        out_specs=pl.BlockSpec((1, COND_ROWS, tn), lambda l, j: (l, 0, j)),
        out_shape=jax.ShapeDtypeStruct((depth, COND_ROWS, nd), f32),
        compiler_params=_cparams(2),
        name="adaln",
    )(cond, ada_w, ada_b.reshape(depth, 1, nd))


def _modulated_norm(x, g, shift, scale):
    y = x * lax.rsqrt(jnp.mean(x * x, axis=-1, keepdims=True) + EPS) * g
    return y * (1.0 + scale) + shift


def _ffn_kernel(x_ref, m_ref, g_ref, wi_ref, wo_ref, *rest, s0, ff, fc, final):
    o_ref = rest[-1]
    x = x_ref[...]
    h = _modulated_norm(x, g_ref[...], m_ref[0, s0:s0 + 1, :], m_ref[0, s0 + 1:s0 + 2, :]).astype(bf16)
    acc = jnp.zeros(x.shape, f32)
    for j in range(ff // fc):
        gt = _dot(h, wi_ref[:, j * fc:(j + 1) * fc])
        up = _dot(h, wi_ref[:, ff + j * fc:ff + (j + 1) * fc])
        act = (gt * _sigmoid(gt) * up).astype(bf16)
        acc = acc + _dot(act, wo_ref[j * fc:(j + 1) * fc, :])
    out = x + 0.5 * m_ref[0, s0 + 2:s0 + 3, :] * acc
    if final:
        fg_ref = rest[0]
        out = out * lax.rsqrt(jnp.mean(out * out, axis=-1, keepdims=True) + EPS) * fg_ref[...]
    o_ref[...] = out


def _ffn(t, mods, g, w_in, w_out, s0, grid, row_map, mod_map, out_rows, final_g=None):
    d = t.shape[1]
    ff = w_out.shape[0]
    fc = 256 if ff % 256 == 0 else ff
    const = lambda *a: (0, 0)
    in_specs = [pl.BlockSpec((TM, d), lambda *a: (row_map(*a), 0)),
                pl.BlockSpec((1, N_MOD, d), lambda *a: (mod_map(*a), 0, 0)),
                pl.BlockSpec((1, d), const),
                pl.BlockSpec((d, 2 * ff), const, pipeline_mode=pl.Buffered(1)),
                pl.BlockSpec((ff, d), const, pipeline_mode=pl.Buffered(1))]
    args = [t, mods, g.reshape(1, d), w_in, w_out]
    if final_g is not None:
        in_specs.append(pl.BlockSpec((1, d), const))
        args.append(final_g.reshape(1, d))
    n_axes = len(grid)
    if n_axes == 1:
        out_map = lambda i: (i, 0)
    else:
        nj = grid[1]
        out_map = lambda b, j: (b * nj + j, 0)
    return pl.pallas_call(
        functools.partial(_ffn_kernel, s0=s0, ff=ff, fc=fc, final=final_g is not None),
        grid=grid, in_specs=in_specs,
        out_specs=pl.BlockSpec((TM, d), out_map),
        out_shape=jax.ShapeDtypeStruct((out_rows, d), f32),
        compiler_params=_cparams(n_axes),
        name="ffn",
    )(*args)


def _mixin_kernel(x_ref, m_ref, g_ref, w_ref, cos_ref, sin_ref, fr_ref, hc_ref, q_ref, k_ref, v_ref):
    x = x_ref[...]
    h = _modulated_norm(x, g_ref[...], m_ref[0, 3:4, :], m_ref[0, 4:5, :]).astype(bf16)
    fr_ref[...] = _dot(h, w_ref[:, 0:RWKV_IN])
    o = RWKV_IN
    val = _dot(h, w_ref[:, o:o + D_CONV])
    gate = _dot(h, w_ref[:, o + D_CONV:o + CONV_IN])
    hc_ref[...] = val * _sigmoid(gate)
    o = RWKV_IN + CONV_IN
    cos = cos_ref[...]
    sin = sin_ref[...]
    lane = lax.broadcasted_iota(jnp.int32, cos.shape, 1)
    low_half = ((lane // ROPE_FREQS) % 2) == 0

    def rope(t):
        partner = jnp.where(low_half, pltpu.roll(t, D_QK - ROPE_FREQS, 1), pltpu.roll(t, ROPE_FREQS, 1))
        return t * cos + partner * sin

    q = _dot(h, w_ref[:, o:o + D_QK])
    q_ref[...] = (rope(q) * (DIFF_QK_DIM ** -0.5)).astype(bf16)
    k = _dot(h, w_ref[:, o + D_QK:o + 2 * D_QK])
    k_ref[...] = rope(k).astype(bf16)
    v_ref[...] = _dot(h, w_ref[:, o + 2 * D_QK:o + 2 * D_QK + D_DIFF]).astype(bf16)


def _mixin(t, mods, g, w, cos, sin, nbs, mod_map):
    m, d = t.shape
    const = lambda i: (0, 0)
    row = lambda i: (i, 0)
    return pl.pallas_call(
        _mixin_kernel,
        grid=(m // TM,),
        in_specs=[pl.BlockSpec((TM, d), row),
                  pl.BlockSpec((1, N_MOD, d), lambda i: (mod_map(i), 0, 0)),
                  pl.BlockSpec((1, d), const),
                  pl.BlockSpec((d, P_IN), const),
                  pl.BlockSpec((TM, D_QK), lambda i: (i % nbs, 0)),
                  pl.BlockSpec((TM, D_QK), lambda i: (i % nbs, 0))],
        out_specs=[pl.BlockSpec((TM, RWKV_IN), row), pl.BlockSpec((TM, D_CONV), row),
                   pl.BlockSpec((TM, D_QK), row), pl.BlockSpec((TM, D_QK), row), pl.BlockSpec((TM, D_DIFF), row)],
        out_shape=[jax.ShapeDtypeStruct((m, RWKV_IN), f32), jax.ShapeDtypeStruct((m, D_CONV), f32),
                   jax.ShapeDtypeStruct((m, D_QK), bf16), jax.ShapeDtypeStruct((m, D_QK), bf16),
                   jax.ShapeDtypeStruct((m, D_DIFF), bf16)],
        compiler_params=_cparams(1),
        name="mixin",
    )(t, mods, g.reshape(1, d), w, cos, sin)


def _attn_kernel(q_ref, k_ref, v_ref, lam_ref, g_ref, o_ref, *, lam_init):
    lv = lam_ref[...]
    lam = (jnp.exp(jnp.sum(lv[0:1] * lv[1:2], keepdims=True)) - jnp.exp(jnp.sum(lv[2:3] * lv[3:4], keepdims=True))
           + lam_init)
    q = q_ref[...]
    attn = None
    for m in range(2):
        sl = slice(m * DIFF_QK_DIM, (m + 1) * DIFF_QK_DIM)
        s = _dot_nt(q[:, sl], k_ref[:, sl])
        e = jnp.exp(s - jnp.max(s, axis=-1, keepdims=True))
        inv = 1.0 / jnp.sum(e, axis=-1, keepdims=True)
        attn = e * inv if m == 0 else attn - e * (lam * inv)
    o = _dot(attn.astype(bf16), v_ref[...])
    o = o * lax.rsqrt(jnp.mean(o * o, axis=-1, keepdims=True) + 1e-5) * g_ref[...]
    o_ref[...] = o * (1.0 - lam_init)


def _attn(q, k, v, lam_vecs, norm_g, lam_init, nbatch, n_q, q_off, n_k, k_off, seq, tq, out_rows, out_seq, out_off):
    nq = n_q // tq
    return pl.pallas_call(
        functools.partial(_attn_kernel, lam_init=lam_init),
        grid=(nbatch, DIFF_HEADS, nq),
        in_specs=[pl.BlockSpec((tq, DIFF_V_DIM), lambda b, h, i: ((b * seq + q_off) // tq + i, h)),
                  pl.BlockSpec((n_k, DIFF_V_DIM), lambda b, h, i: ((b * seq + k_off) // n_k, h)),
                  pl.BlockSpec((n_k, DIFF_V_DIM), lambda b, h, i: ((b * seq + k_off) // n_k, h)),
                  pl.BlockSpec((4, DIFF_QK_DIM), lambda b, h, i: (0, 0)),
                  pl.BlockSpec((1, DIFF_V_DIM), lambda b, h, i: (0, 0))],
        out_specs=pl.BlockSpec((tq, DIFF_V_DIM), lambda b, h, i: ((b * out_seq + out_off) // tq + i, h)),
        out_shape=jax.ShapeDtypeStruct((out_rows, D_DIFF), f32),
        compiler_params=_cparams(3),
        name="diffattn",
    )(q, k, v, lam_vecs, norm_g.reshape(1, DIFF_V_DIM))


def _segment_flags(i, nbc, nbs):
    j = i % nbs
    first = jnp.logical_or(j == 0, j == nbc)
    last = jnp.logical_or(j == nbc - 1, j == nbs - 1)
    return first, last


def _halo_specs(width, n_rows):
    per = TM // HALO
    nblk = n_rows // HALO
    prev = pl.BlockSpec((HALO, width), lambda i: (jnp.maximum(i * per - 1, 0), 0))
    nxt = pl.BlockSpec((HALO, width), lambda i: (jnp.minimum((i + 1) * per, nblk - 1), 0))
    return prev, nxt


def _conv_kernel(h_ref, hp_ref, hn_ref, w_ref, b_ref, g_ref, bb_ref, o_ref, buf, *, nbc, nbs):
    first, last = _segment_flags(pl.program_id(0), nbc, nbs)
    buf[0:HALO, :] = jnp.where(first, 0.0, hp_ref[...])
    buf[HALO:HALO + TM, :] = h_ref[...]
    buf[HALO + TM:HALO + TM + HALO, :] = jnp.where(last, 0.0, hn_ref[...])
    sub = 64
    for r0 in range(0, TM, sub):
        acc = jnp.zeros((sub, D_CONV), f32) + b_ref[...]
        for kk in range(CONV_WIDTH):
            start = r0 + HALO - CONV_HALF + kk
            acc = acc + buf[start:start + sub, :] * w_ref[kk:kk + 1, :]
        mu = jnp.mean(acc, axis=-1, keepdims=True)
        xc = acc - mu
        var = jnp.mean(xc * xc, axis=-1, keepdims=True)
        y = xc * lax.rsqrt(var + 1e-5) * g_ref[...] + bb_ref[...]
        o_ref[r0:r0 + sub, :] = y * _sigmoid(y)


def _conv(hc, dw_w, dw_b, ln_g, ln_b, nbc, nbs):
    m = hc.shape[0]
    prev, nxt = _halo_specs(D_CONV, m)
    const = lambda i: (0, 0)
    return pl.pallas_call(
        functools.partial(_conv_kernel, nbc=nbc, nbs=nbs),
        grid=(m // TM,),
        in_specs=[pl.BlockSpec((TM, D_CONV), lambda i: (i, 0)), prev, nxt,
                  pl.BlockSpec((CONV_WIDTH, D_CONV), const), pl.BlockSpec((1, D_CONV), const),
                  pl.BlockSpec((1, D_CONV), const), pl.BlockSpec((1, D_CONV), const)],
        out_specs=pl.BlockSpec((TM, D_CONV), lambda i: (i, 0)),
        out_shape=jax.ShapeDtypeStruct((m, D_CONV), f32),
        scratch_shapes=[pltpu.VMEM((TM + 2 * HALO, D_CONV), f32)],
        compiler_params=_cparams(1),
        name="convmod",
    )(hc, hc, hc, dw_w, dw_b.reshape(1, D_CONV), ln_g.reshape(1, D_CONV), ln_b.reshape(1, D_CONV))


def _rwkv_chunk_kernel(f_ref, fp_ref, fn_ref, mu_ref, w0_ref, w2_ref, a0_ref, a2_ref, g2_ref, kk_ref, ka_ref, rk_ref,
                       rp_ref, y0_ref, mm_ref, nn_ref, bv_ref, gg_ref,
                       s_r, s_v, s_a, s_lw, s_k, s_b, *, nbc, nbs):
    first, last = _segment_flags(pl.program_id(0), nbc, nbs)
    f = f_ref[...]
    prow = jnp.where(first, 0.0, fp_ref[HALO - 1:HALO, :])
    nrow = jnp.where(last, 0.0, fn_ref[0:1, :])
    rows = lax.broadcasted_iota(jnp.int32, f.shape, 0)
    prev = jnp.where(rows == 0, prow, pltpu.roll(f, 1, 0))
    nxt = jnp.where(rows == TM - 1, nrow, pltpu.roll(f, TM - 1, 0))
    fs = f + mu_ref[0:1, :] * (prev - f) + mu_ref[1:2, :] * (nxt - f)

    r = fs[:, 0:D_RWKV]
    k = fs[:, D_RWKV:2 * D_RWKV]
    v = fs[:, 2 * D_RWKV:3 * D_RWKV]
    o_w = 3 * D_RWKV
    o_a = o_w + 2 * DECAY_LORA
    o_g = o_a + 2 * AAA_LORA
    ones_bd = _head_ones(D_RWKV, HEAD_DIM)
    gg_ref[...] = _dot(_sigmoid(fs[:, o_g:o_g + GATE_LORA]), g2_ref[...], HI)
    kk = k * kk_ref[...]
    kk = kk * lax.rsqrt(_dot(kk * kk, ones_bd, HI) + NORM_EPS)
    ksum = None
    for d in range(2):
        wd = fs[:, o_w + d * DECAY_LORA:o_w + (d + 1) * DECAY_LORA]
        ad = fs[:, o_a + d * AAA_LORA:o_a + (d + 1) * AAA_LORA]
        w_raw = w0_ref[d:d + 1, :] + _dot(jnp.tanh(wd), w2_ref[d], HI)
        z = -w_raw
        softplus = jnp.maximum(z, 0.0) + jnp.log(1.0 + jnp.exp(-jnp.abs(z)))
        s_lw[d] = -jnp.exp(-softplus - 0.5)
        a = _sigmoid(a0_ref[d:d + 1, :] + _dot(ad, a2_ref[d], HI))
        kd = k * (1.0 + (a - 1.0) * ka_ref[...])
        ksum = kd if d == 0 else ksum + kd
        s_k[d] = kd
        s_b[d] = kk * a
    s_r[...] = r
    s_v[...] = v
    s_a[...] = -kk
    bv_ref[...] = _dot(r * ksum * rk_ref[...], ones_bd, HI) * v

    ri = lax.broadcasted_iota(jnp.int32, (CHUNK, CHUNK), 0)
    ci = lax.broadcasted_iota(jnp.int32, (CHUNK, CHUNK), 1)
    eye = jnp.where(ri == ci, 1.0, 0.0).astype(f32)
    tri = tuple(jnp.where(msk, 1.0, 0.0).astype(f32) for msk in (ci <= ri, ci >= ri))
    strict = (ci < ri, ci > ri)
    incl = (ci <= ri, ci >= ri)

    def chunk_body(c, carry):
        rs = pl.ds(pl.multiple_of(c * CHUNK, CHUNK), CHUNK)
        r_c = s_r[rs, :]
        v_c = s_v[rs, :]
        a_c = s_a[rs, :]
        for d in range(2):
            lw = s_lw[d, rs, :]
            cum = _dot(tri[d], lw, HI)
            er = jnp.exp(cum)
            ei = jnp.exp(-cum)
            at = a_c * jnp.exp(cum - lw)
            rt = r_c * er
            bt = s_b[d, rs, :] * ei
            kt = s_k[d, rs, :] * ei
            wc = er[CHUNK - 1:CHUNK, :] if d == 0 else er[0:1, :]
            for h in range(RWKV_HEADS):
                hs = slice(h * HEAD_DIM, (h + 1) * HEAD_DIM)
                at_h, rt_h, bt_h, kt_h, v_h = at[:, hs], rt[:, hs], bt[:, hs], kt[:, hs], v_c[:, hs]
                gm = _dot_nt(jnp.concatenate([at_h, rt_h], axis=0), jnp.concatenate([bt_h, kt_h], axis=0), HI)
                lab = jnp.where(strict[d], gm[:CHUNK, :CHUNK], 0.0)
                lak = jnp.where(strict[d], gm[:CHUNK, CHUNK:], 0.0)
                lrb = jnp.where(incl[d], gm[CHUNK:, :CHUNK], 0.0)
                lrk = jnp.where(incl[d], gm[CHUNK:, CHUNK:], 0.0)
                x = eye + lab
                p = lab
                for _ in range(int(math.log2(CHUNK)) - 1):
                    p = _dot(p, p, HI)
                    x = x + _dot(x, p, HI)
                au = _dot(x, jnp.concatenate([at_h, _dot(lak, v_h, HI)], axis=1), HI)
                ry = _dot(lrb, au, HI)
                rp_ref[d, rs, hs] = rt_h + ry[:, :HEAD_DIM]
                y0_ref[d, rs, hs] = ry[:, HEAD_DIM:] + _dot(lrk, v_h, HI)
                mn = _dot_tn(au, bt_h, HI)
                wc_h = wc[:, hs]
                mm_ref[d, c, h] = (eye + mn[:HEAD_DIM, :]) * wc_h
                nn_ref[d, c, h] = (mn[HEAD_DIM:, :] + _dot_tn(v_h, kt_h, HI)) * wc_h
        return carry

    lax.fori_loop(0, TM // CHUNK, chunk_body, 0)


def _rwkv_chunks(fr, mu, w0, w2, a0, a2, g2, kk, ka, rk, nbc, nbs):
    m = fr.shape[0]
    nct = TM // CHUNK
    prev, nxt = _halo_specs(RWKV_IN, m)
    c2 = lambda i: (0, 0)
    c3 = lambda i: (0, 0, 0)
    row = lambda i: (i, 0)
    dirrow = lambda i: (0, i, 0)
    ops = lambda i: (0, i, 0, 0, 0)
    return pl.pallas_call(
        functools.partial(_rwkv_chunk_kernel, nbc=nbc, nbs=nbs),
        grid=(m // TM,),
        in_specs=[pl.BlockSpec((TM, RWKV_IN), row), prev, nxt,
                  pl.BlockSpec((2, RWKV_IN), c2), pl.BlockSpec((2, D_RWKV), c2),
                  pl.BlockSpec((2, DECAY_LORA, D_RWKV), c3), pl.BlockSpec((2, D_RWKV), c2),
                  pl.BlockSpec((2, AAA_LORA, D_RWKV), c3), pl.BlockSpec((GATE_LORA, D_RWKV), c2),
                  pl.BlockSpec((1, D_RWKV), c2), pl.BlockSpec((1, D_RWKV), c2), pl.BlockSpec((1, D_RWKV), c2)],
        out_specs=[pl.BlockSpec((2, TM, D_RWKV), dirrow), pl.BlockSpec((2, TM, D_RWKV), dirrow),
                   pl.BlockSpec((2, nct, RWKV_HEADS, HEAD_DIM, HEAD_DIM), ops),
                   pl.BlockSpec((2, nct, RWKV_HEADS, HEAD_DIM, HEAD_DIM), ops),
                   pl.BlockSpec((TM, D_RWKV), row), pl.BlockSpec((TM, D_RWKV), row)],
        out_shape=[jax.ShapeDtypeStruct((2, m, D_RWKV), f32), jax.ShapeDtypeStruct((2, m, D_RWKV), f32),
                   jax.ShapeDtypeStruct((2, m // CHUNK, RWKV_HEADS, HEAD_DIM, HEAD_DIM), f32),
                   jax.ShapeDtypeStruct((2, m // CHUNK, RWKV_HEADS, HEAD_DIM, HEAD_DIM), f32),
                   jax.ShapeDtypeStruct((m, D_RWKV), f32), jax.ShapeDtypeStruct((m, D_RWKV), f32)],
        scratch_shapes=[pltpu.VMEM((TM, D_RWKV), f32), pltpu.VMEM((TM, D_RWKV), f32), pltpu.VMEM((TM, D_RWKV), f32),
                        pltpu.VMEM((2, TM, D_RWKV), f32), pltpu.VMEM((2, TM, D_RWKV), f32),
                        pltpu.VMEM((2, TM, D_RWKV), f32)],
        compiler_params=_cparams(1),
        name="rwkv_chunks",
    )(fr, fr, fr, mu, w0, w2, a0, a2, g2, kk.reshape(1, D_RWKV), ka.reshape(1, D_RWKV), rk.reshape(1, D_RWKV))


def _rwkv_scan_kernel(rp_ref, y0_ref, mm_ref, nn_ref, bv_ref, gg_ref, lng_ref, lnb_ref, o_ref, y_acc, st_ref, *, ncc,
                      nca):
    y_acc[...] = jnp.zeros(y_acc.shape, f32)
    st_ref[...] = jnp.zeros(st_ref.shape, f32)

    def body(step, carry):
        c_fwd = step
        c_bwd = jnp.where(step < ncc, ncc - 1 - step, nca - 1 - (step - ncc))
        for d, c in ((0, c_fwd), (1, c_bwd)):
            rs = pl.ds(pl.multiple_of(c * CHUNK, CHUNK), CHUNK)
            rp = rp_ref[d, rs, :]
            y0 = y0_ref[d, rs, :]
            ys = []
            for h in range(RWKV_HEADS):
                hs = slice(h * HEAD_DIM, (h + 1) * HEAD_DIM)
                st = st_ref[d * RWKV_HEADS + h]
                ys.append(y0[:, hs] + _dot_nt(rp[:, hs], st, HI))
                st_ref[d * RWKV_HEADS + h] = _dot(st, mm_ref[d, c, h], HI) + nn_ref[d, c, h]
            y_acc[rs, :] += jnp.concatenate(ys, axis=1)
        return carry

    lax.fori_loop(0, nca, body, 0)

    ones_bd = _head_ones(D_RWKV, HEAD_DIM)
    y = y_acc[...]
    mu = _dot(y, ones_bd, HI) * (1.0 / HEAD_DIM)
    yc = y - mu
    var = _dot(yc * yc, ones_bd, HI) * (1.0 / HEAD_DIM)
    yn = yc * lax.rsqrt(var + GN_EPS) * lng_ref[...] + lnb_ref[...]
    o_ref[...] = (yn + bv_ref[...]) * gg_ref[...]


def _rwkv_scan(rp, y0, mm, nn, bv, gg, ln_g, ln_b, nbatch, seq, ncc):
    nca = seq // CHUNK
    c2 = lambda b: (0, 0)
    dirrow = lambda b: (0, b, 0)
    ops = lambda b: (0, b, 0, 0, 0)
    row = lambda b: (b, 0)
    return pl.pallas_call(
        functools.partial(_rwkv_scan_kernel, ncc=ncc, nca=nca),
        grid=(nbatch,),
        in_specs=[pl.BlockSpec((2, seq, D_RWKV), dirrow, pipeline_mode=pl.Buffered(1)),
                  pl.BlockSpec((2, seq, D_RWKV), dirrow, pipeline_mode=pl.Buffered(1)),
                  pl.BlockSpec((2, nca, RWKV_HEADS, HEAD_DIM, HEAD_DIM), ops, pipeline_mode=pl.Buffered(1)),
                  pl.BlockSpec((2, nca, RWKV_HEADS, HEAD_DIM, HEAD_DIM), ops, pipeline_mode=pl.Buffered(1)),
                  pl.BlockSpec((seq, D_RWKV), row), pl.BlockSpec((seq, D_RWKV), row),
                  pl.BlockSpec((1, D_RWKV), c2), pl.BlockSpec((1, D_RWKV), c2)],
        out_specs=pl.BlockSpec((seq, D_RWKV), row),
        out_shape=jax.ShapeDtypeStruct((nbatch * seq, D_RWKV), f32),
        scratch_shapes=[pltpu.VMEM((seq, D_RWKV), f32), pltpu.VMEM((2 * RWKV_HEADS, HEAD_DIM, HEAD_DIM), f32)],
        compiler_params=_cparams(1),
        name="rwkv_scan",
    )(rp, y0, mm, nn, bv, gg, ln_g.reshape(1, D_RWKV), ln_b.reshape(1, D_RWKV))


def _mixout_kernel(t_ref, a_ref, b_ref, c_ref, w_ref, m_ref, o_ref):
    out = _dot(a_ref[...].astype(bf16), w_ref[0:D_RWKV, :])
    out = out + _dot(b_ref[...].astype(bf16), w_ref[D_RWKV:D_RWKV + D_CONV, :])
    out = out + _dot(c_ref[...].astype(bf16), w_ref[D_RWKV + D_CONV:, :])
    o_ref[...] = t_ref[...] + m_ref[0, 5:6, :] * out


def _mixout(t, oa, ob, oc, w, mods, grid, row_map, mod_map, c_map, out_rows):
    d = t.shape[1]
    n_axes = len(grid)
    if n_axes == 1:
        out_map = lambda i: (i, 0)
    else:
        nj = grid[1]
        out_map = lambda b, j: (b * nj + j, 0)
    rmap = lambda *a: (row_map(*a), 0)
    return pl.pallas_call(
        _mixout_kernel,
        grid=grid,
        in_specs=[pl.BlockSpec((TM, d), rmap), pl.BlockSpec((TM, D_RWKV), rmap), pl.BlockSpec((TM, D_CONV), rmap),
                  pl.BlockSpec((TM, D_DIFF), lambda *a: (c_map(*a), 0)),
                  pl.BlockSpec((D_RWKV + D_CONV + D_DIFF, d), lambda *a: (0, 0)),
                  pl.BlockSpec((1, N_MOD, d), lambda *a: (mod_map(*a), 0, 0))],
        out_specs=pl.BlockSpec((TM, d), out_map),
        out_shape=jax.ShapeDtypeStruct((out_rows, d), f32),
        compiler_params=_cparams(n_axes),
        name="mixout",
    )(t, oa, ob, oc, w, mods)


def _rope_tables(n_ctx, n_lat):
    n_rows = n_lat // GRID_W
    row = jnp.repeat(jnp.arange(n_rows, dtype=jnp.int32), GRID_W)
    col = jnp.tile(jnp.arange(GRID_W, dtype=jnp.int32), n_rows)
    inv = 1.0 / (ROPE_THETA ** (jnp.arange(ROPE_FREQS, dtype=f32) * 2.0 / AXIS_DIM))
    ang = jnp.stack([row, col], axis=-1).astype(f32)[..., None] * inv
    cos, sin = jnp.cos(ang), jnp.sin(ang)
    cos64 = jnp.concatenate([cos[:, 0], cos[:, 0], cos[:, 1], cos[:, 1]], axis=-1)
    sin64 = jnp.concatenate([-sin[:, 0], sin[:, 0], -sin[:, 1], sin[:, 1]], axis=-1)
    reps = D_QK // DIFF_QK_DIM
    cos_t = jnp.concatenate([jnp.ones((n_ctx, D_QK), f32), jnp.tile(cos64, (1, reps))], axis=0)
    sin_t = jnp.concatenate([jnp.zeros((n_ctx, D_QK), f32), jnp.tile(sin64, (1, reps))], axis=0)
    return cos_t, sin_t


def kernel(x, c, ctx, c_ctx, ada_w, ada_b, norm_g, ffn_w_in, ffn_w_out, mix_w_in, mix_w_out, rwkv_mu, rwkv_w0, rwkv_w2,
           rwkv_a0, rwkv_a2, rwkv_g2, rwkv_kk, rwkv_ka, rwkv_rk, rwkv_ln_g, rwkv_ln_b, conv_dw_w, conv_dw_b, conv_ln_g,
           conv_ln_b, diff_lam, diff_norm_g, final_g):
    nb, n_lat, d = x.shape
    n_ctx = ctx.shape[1]
    depth = ada_w.shape[0]
    seq = n_ctx + n_lat
    assert n_ctx % TM == 0 and n_lat % TM == 0 and n_lat % GRID_W == 0 and seq % n_ctx == 0
    assert nb + 1 <= COND_ROWS
    nbc, nbx, nbs = n_ctx // TM, n_lat // TM, seq // TM
    m = nb * seq

    t = jnp.concatenate([ctx, x], axis=1).reshape(m, d)
    cond = jnp.zeros((COND_ROWS, d), f32).at[:nb].set(c).at[nb].set(c_ctx)
    mods = _adaln(cond, ada_w, ada_b).reshape(depth, COND_ROWS, N_MOD, d)
    cos_t, sin_t = _rope_tables(n_ctx, n_lat)
    w_in16 = ffn_w_in.astype(bf16)
    w_out16 = ffn_w_out.astype(bf16)
    mix_in16 = mix_w_in.astype(bf16)
    mix_out16 = mix_w_out.astype(bf16)

    mod_all = lambda i: jnp.where(i % nbs < nbc, nb, i // nbs)
    row_all = lambda i: i
    row_lat = lambda b, j: b * nbs + nbc + j
    mod_lat = lambda b, j: b

    out = None
    for l in range(depth):
        last = l == depth - 1
        ml = mods[l]
        t = _ffn(t, ml, norm_g[l, 0], w_in16[l, 0], w_out16[l, 0], 0, (nbs * nb,), row_all, mod_all, m)
        fr, hc, q, k, v = _mixin(t, ml, norm_g[l, 1], mix_in16[l], cos_t, sin_t, nbs, mod_all)
        rp, y0, mm, nn, bv, gg = _rwkv_chunks(fr, rwkv_mu[l], rwkv_w0[l], rwkv_w2[l], rwkv_a0[l], rwkv_a2[l], rwkv_g2[l],
                                              rwkv_kk[l], rwkv_ka[l], rwkv_rk[l].reshape(D_RWKV), nbc, nbs)
        oa = _rwkv_scan(rp, y0, mm, nn, bv, gg, rwkv_ln_g[l], rwkv_ln_b[l], nb, seq, n_ctx // CHUNK)
        ob = _conv(hc, conv_dw_w[l], conv_dw_b[l], conv_ln_g[l], conv_ln_b[l], nbc, nbs)
        lam_init = 0.8 - 0.6 * math.exp(-0.3 * l)
        if last:
            oc = _attn(q, k, v, diff_lam[l], diff_norm_g[l], lam_init, nb, n_lat, n_ctx, seq, 0, seq, TM,
                       nb * n_lat, n_lat, 0)
            t = _mixout(t, oa, ob, oc, mix_out16[l], ml, (nb, nbx), row_lat, mod_lat, lambda b, j: b * nbx + j,
                        nb * n_lat)
            out = _ffn(t, ml, norm_g[l, 2], w_in16[l, 1], w_out16[l, 1], 6, (nb * nbx,), row_all, lambda i: i // nbx,
                       nb * n_lat, final_g=final_g)
        else:
            oc_x = _attn(q, k, v, diff_lam[l], diff_norm_g[l], lam_init, nb, n_lat, n_ctx, seq, 0, seq, TM,
                         nb * n_lat, n_lat, 0)
            oc_c = _attn(q, k, v, diff_lam[l], diff_norm_g[l], lam_init, nb, n_ctx, 0, n_ctx, 0, seq, TM,
                         nb * n_ctx, n_ctx, 0)
            oc = jnp.concatenate([oc_c.reshape(nb, n_ctx, D_DIFF), oc_x.reshape(nb, n_lat, D_DIFF)], axis=1)
            oc = oc.reshape(m, D_DIFF)
            t = _mixout(t, oa, ob, oc, mix_out16[l], ml, (nbs * nb,), row_all, mod_all, row_all, m)
            t = _ffn(t, ml, norm_g[l, 2], w_in16[l, 1], w_out16[l, 1], 6, (nbs * nb,), row_all, mod_all, m)
    return out.reshape(nb, n_lat, d)
```

```python
import functools
import math

import jax
import jax.numpy as jnp
from jax import lax
from jax.experimental import pallas as pl
from jax.experimental.pallas import tpu as pltpu

f32 = jnp.float32
bf16 = jnp.bfloat16

N_MOD = 9
EPS = 1e-6
GRID_W = 64
RWKV_HEADS = 4
HEAD_DIM = 64
D_RWKV = RWKV_HEADS * HEAD_DIM
DECAY_LORA = 64
AAA_LORA = 64
GATE_LORA = 128
GN_EPS = 64e-5
NORM_EPS = 1e-12
D_CONV = 256
CONV_WIDTH = 31
CONV_HALF = CONV_WIDTH // 2
DIFF_HEADS = 4
DIFF_QK_DIM = 64
DIFF_V_DIM = 2 * DIFF_QK_DIM
D_DIFF = DIFF_HEADS * DIFF_V_DIM
ROPE_THETA = 10000.0
AXIS_DIM = DIFF_QK_DIM // 2
ROPE_FREQS = AXIS_DIM // 2
RWKV_IN = 3 * D_RWKV + 2 * DECAY_LORA + 2 * AAA_LORA + GATE_LORA
CONV_IN = 2 * D_CONV
D_QK = DIFF_HEADS * 2 * DIFF_QK_DIM
DIFF_IN = 2 * D_QK + D_DIFF
P_IN = RWKV_IN + CONV_IN + DIFF_IN

TM = 256
CHUNK = 64
HALO = 16
COND_ROWS = 16
VMEM_LIMIT = 56 * 1024 * 1024


def _dot(a, b, prec=None):
    return jnp.dot(a, b, preferred_element_type=f32, precision=prec)


def _dot_nt(a, b, prec=None):
    return lax.dot_general(a, b, (((1,), (1,)), ((), ())), preferred_element_type=f32, precision=prec)


def _dot_tn(a, b, prec=None):
    return lax.dot_general(a, b, (((0,), (0,)), ((), ())), preferred_element_type=f32, precision=prec)


def _split3(a):
    a1 = a.astype(bf16)
    r1 = a - a1.astype(f32)
    a2 = r1.astype(bf16)
    a3 = (r1 - a2.astype(f32)).astype(bf16)
    return a1, a2, a3


def _dot_exact_rhs(a, b16):
    a1, a2, a3 = _split3(a)
    return _dot(a1, b16) + _dot(a2, b16) + _dot(a3, b16)


def _dot_exact_lhs(a16, b):
    b1, b2, b3 = _split3(b)
    return _dot(a16, b1) + _dot(a16, b2) + _dot(a16, b3)


def _sigmoid(x):
    return 1.0 / (1.0 + jnp.exp(-x))


def _cparams(n_axes):
    return pltpu.CompilerParams(dimension_semantics=("arbitrary",) * n_axes, vmem_limit_bytes=VMEM_LIMIT)


def _head_ones(n, width):
    r = lax.broadcasted_iota(jnp.int32, (n, n), 0) // width
    c = lax.broadcasted_iota(jnp.int32, (n, n), 1) // width
    return jnp.where(r == c, 1.0, 0.0).astype(bf16)


def _adaln_kernel(c_ref, w_ref, b_ref, o_ref):
    cond = c_ref[...]
    cond = cond * _sigmoid(cond)
    o_ref[0] = _dot(cond.astype(bf16), w_ref[0].astype(bf16)) + b_ref[0]


def _adaln(cond, ada_w, ada_b):
    depth, d, nd = ada_w.shape
    tn = nd // 4
    return pl.pallas_call(
        _adaln_kernel,
        grid=(depth, nd // tn),
        in_specs=[pl.BlockSpec((COND_ROWS, d), lambda l, j: (0, 0)),
                  pl.BlockSpec((1, d, tn), lambda l, j: (l, 0, j)),
                  pl.BlockSpec((1, 1, tn), lambda l, j: (l, 0, j))],
        out_specs=pl.BlockSpec((1, COND_ROWS, tn), lambda l, j: (l, 0, j)),
        out_shape=jax.ShapeDtypeStruct((depth, COND_ROWS, nd), f32),
        compiler_params=_cparams(2),
        name="adaln",
    )(cond, ada_w, ada_b.reshape(depth, 1, nd))


def _modulated_norm(x, g, shift, scale):
    y = x * lax.rsqrt(jnp.mean(x * x, axis=-1, keepdims=True) + EPS) * g
    return y * (1.0 + scale) + shift


def _ffn_kernel(x_ref, m_ref, g_ref, wi_ref, wo_ref, *rest, s0, ff, fc, final):
    o_ref = rest[-1]
    x = x_ref[...]
    h = _modulated_norm(x, g_ref[...], m_ref[0, s0:s0 + 1, :], m_ref[0, s0 + 1:s0 + 2, :]).astype(bf16)
    acc = jnp.zeros(x.shape, f32)
    for j in range(ff // fc):
        gt = _dot(h, wi_ref[:, j * fc:(j + 1) * fc])
        up = _dot(h, wi_ref[:, ff + j * fc:ff + (j + 1) * fc])
        act = (gt * _sigmoid(gt) * up).astype(bf16)
        acc = acc + _dot(act, wo_ref[j * fc:(j + 1) * fc, :])
    out = x + 0.5 * m_ref[0, s0 + 2:s0 + 3, :] * acc
    if final:
        fg_ref = rest[0]
        out = out * lax.rsqrt(jnp.mean(out * out, axis=-1, keepdims=True) + EPS) * fg_ref[...]
    o_ref[...] = out


def _ffn(t, mods, g, w_in, w_out, s0, grid, row_map, mod_map, out_rows, final_g=None):
    d = t.shape[1]
    ff = w_out.shape[0]
    fc = 256 if ff % 256 == 0 else ff
    const = lambda *a: (0, 0)
    in_specs = [pl.BlockSpec((TM, d), lambda *a: (row_map(*a), 0)),
                pl.BlockSpec((1, N_MOD, d), lambda *a: (mod_map(*a), 0, 0)),
                pl.BlockSpec((1, d), const),
                pl.BlockSpec((d, 2 * ff), const, pipeline_mode=pl.Buffered(1)),
                pl.BlockSpec((ff, d), const, pipeline_mode=pl.Buffered(1))]
    args = [t, mods, g.reshape(1, d), w_in, w_out]
    if final_g is not None:
        in_specs.append(pl.BlockSpec((1, d), const))
        args.append(final_g.reshape(1, d))
    n_axes = len(grid)
    if n_axes == 1:
        out_map = lambda i: (i, 0)
    else:
        nj = grid[1]
        out_map = lambda b, j: (b * nj + j, 0)
    return pl.pallas_call(
        functools.partial(_ffn_kernel, s0=s0, ff=ff, fc=fc, final=final_g is not None),
        grid=grid, in_specs=in_specs,
        out_specs=pl.BlockSpec((TM, d), out_map),
        out_shape=jax.ShapeDtypeStruct((out_rows, d), f32),
        compiler_params=_cparams(n_axes),
        name="ffn",
    )(*args)


def _mixin_kernel(x_ref, m_ref, g_ref, w_ref, cos_ref, sin_ref, fr_ref, hc_ref, q_ref, k_ref, v_ref):
    x = x_ref[...]
    h = _modulated_norm(x, g_ref[...], m_ref[0, 3:4, :], m_ref[0, 4:5, :]).astype(bf16)
    fr_ref[...] = _dot(h, w_ref[:, 0:RWKV_IN])
    o = RWKV_IN
    val = _dot(h, w_ref[:, o:o + D_CONV])
    gate = _dot(h, w_ref[:, o + D_CONV:o + CONV_IN])
    hc_ref[...] = val * _sigmoid(gate)
    o = RWKV_IN + CONV_IN
    cos = cos_ref[...]
    sin = sin_ref[...]
    lane = lax.broadcasted_iota(jnp.int32, cos.shape, 1)
    low_half = ((lane // ROPE_FREQS) % 2) == 0

    def rope(t):
        partner = jnp.where(low_half, pltpu.roll(t, D_QK - ROPE_FREQS, 1), pltpu.roll(t, ROPE_FREQS, 1))
        return t * cos + partner * sin

    q = _dot(h, w_ref[:, o:o + D_QK])
    q_ref[...] = (rope(q) * (DIFF_QK_DIM ** -0.5)).astype(bf16)
    k = _dot(h, w_ref[:, o + D_QK:o + 2 * D_QK])
    k_ref[...] = rope(k).astype(bf16)
    v_ref[...] = _dot(h, w_ref[:, o + 2 * D_QK:o + 2 * D_QK + D_DIFF]).astype(bf16)


def _mixin(t, mods, g, w, cos, sin, nbs, mod_map):
    m, d = t.shape
    const = lambda i: (0, 0)
    row = lambda i: (i, 0)
    return pl.pallas_call(
        _mixin_kernel,
        grid=(m // TM,),
        in_specs=[pl.BlockSpec((TM, d), row),
                  pl.BlockSpec((1, N_MOD, d), lambda i: (mod_map(i), 0, 0)),
                  pl.BlockSpec((1, d), const),
                  pl.BlockSpec((d, P_IN), const),
                  pl.BlockSpec((TM, D_QK), lambda i: (i % nbs, 0)),
                  pl.BlockSpec((TM, D_QK), lambda i: (i % nbs, 0))],
        out_specs=[pl.BlockSpec((TM, RWKV_IN), row), pl.BlockSpec((TM, D_CONV), row),
                   pl.BlockSpec((TM, D_QK), row), pl.BlockSpec((TM, D_QK), row), pl.BlockSpec((TM, D_DIFF), row)],
        out_shape=[jax.ShapeDtypeStruct((m, RWKV_IN), f32), jax.ShapeDtypeStruct((m, D_CONV), f32),
                   jax.ShapeDtypeStruct((m, D_QK), bf16), jax.ShapeDtypeStruct((m, D_QK), bf16),
                   jax.ShapeDtypeStruct((m, D_DIFF), bf16)],
        compiler_params=_cparams(1),
        name="mixin",
    )(t, mods, g.reshape(1, d), w, cos, sin)


def _attn_kernel(q_ref, k_ref, v_ref, lam_ref, g_ref, o_ref, *, lam_init):
    lv = lam_ref[...]
    lam = (jnp.exp(jnp.sum(lv[0:1] * lv[1:2], keepdims=True)) - jnp.exp(jnp.sum(lv[2:3] * lv[3:4], keepdims=True))
           + lam_init)
    q = q_ref[...]
    attn = None
    for m in range(2):
        sl = slice(m * DIFF_QK_DIM, (m + 1) * DIFF_QK_DIM)
        s = _dot_nt(q[:, sl], k_ref[:, sl])
        e = jnp.exp(s - jnp.max(s, axis=-1, keepdims=True))
        inv = 1.0 / jnp.sum(e, axis=-1, keepdims=True)
        attn = e * inv if m == 0 else attn - e * (lam * inv)
    o = _dot(attn.astype(bf16), v_ref[...])
    o = o * lax.rsqrt(jnp.mean(o * o, axis=-1, keepdims=True) + 1e-5) * g_ref[...]
    o_ref[...] = o * (1.0 - lam_init)


def _attn(q, k, v, lam_vecs, norm_g, lam_init, nbatch, n_q, q_off, n_k, k_off, seq, tq, out_rows, out_seq, out_off):
    nq = n_q // tq
    return pl.pallas_call(
        functools.partial(_attn_kernel, lam_init=lam_init),
        grid=(nbatch, DIFF_HEADS, nq),
        in_specs=[pl.BlockSpec((tq, DIFF_V_DIM), lambda b, h, i: ((b * seq + q_off) // tq + i, h)),
                  pl.BlockSpec((n_k, DIFF_V_DIM), lambda b, h, i: ((b * seq + k_off) // n_k, h)),
                  pl.BlockSpec((n_k, DIFF_V_DIM), lambda b, h, i: ((b * seq + k_off) // n_k, h)),
                  pl.BlockSpec((4, DIFF_QK_DIM), lambda b, h, i: (0, 0)),
                  pl.BlockSpec((1, DIFF_V_DIM), lambda b, h, i: (0, 0))],
        out_specs=pl.BlockSpec((tq, DIFF_V_DIM), lambda b, h, i: ((b * out_seq + out_off) // tq + i, h)),
        out_shape=jax.ShapeDtypeStruct((out_rows, D_DIFF), f32),
        compiler_params=_cparams(3),
        name="diffattn",
    )(q, k, v, lam_vecs, norm_g.reshape(1, DIFF_V_DIM))


def _segment_flags(i, nbc, nbs):
    j = i % nbs
    first = jnp.logical_or(j == 0, j == nbc)
    last = jnp.logical_or(j == nbc - 1, j == nbs - 1)
    return first, last


def _halo_specs(width, n_rows):
    per = TM // HALO
    nblk = n_rows // HALO
    prev = pl.BlockSpec((HALO, width), lambda i: (jnp.maximum(i * per - 1, 0), 0))
    nxt = pl.BlockSpec((HALO, width), lambda i: (jnp.minimum((i + 1) * per, nblk - 1), 0))
    return prev, nxt


def _conv_kernel(h_ref, hp_ref, hn_ref, w_ref, b_ref, g_ref, bb_ref, o_ref, buf, *, nbc, nbs):
    first, last = _segment_flags(pl.program_id(0), nbc, nbs)
    buf[0:HALO, :] = jnp.where(first, 0.0, hp_ref[...])
    buf[HALO:HALO + TM, :] = h_ref[...]
    buf[HALO + TM:HALO + TM + HALO, :] = jnp.where(last, 0.0, hn_ref[...])
    sub = 64
    for r0 in range(0, TM, sub):
        acc = jnp.zeros((sub, D_CONV), f32) + b_ref[...]
        for kk in range(CONV_WIDTH):
            start = r0 + HALO - CONV_HALF + kk
            acc = acc + buf[start:start + sub, :] * w_ref[kk:kk + 1, :]
        mu = jnp.mean(acc, axis=-1, keepdims=True)
        xc = acc - mu
        var = jnp.mean(xc * xc, axis=-1, keepdims=True)
        y = xc * lax.rsqrt(var + 1e-5) * g_ref[...] + bb_ref[...]
        o_ref[r0:r0 + sub, :] = y * _sigmoid(y)


def _conv(hc, dw_w, dw_b, ln_g, ln_b, nbc, nbs):
    m = hc.shape[0]
    prev, nxt = _halo_specs(D_CONV, m)
    const = lambda i: (0, 0)
    return pl.pallas_call(
        functools.partial(_conv_kernel, nbc=nbc, nbs=nbs),
        grid=(m // TM,),
        in_specs=[pl.BlockSpec((TM, D_CONV), lambda i: (i, 0)), prev, nxt,
                  pl.BlockSpec((CONV_WIDTH, D_CONV), const), pl.BlockSpec((1, D_CONV), const),
                  pl.BlockSpec((1, D_CONV), const), pl.BlockSpec((1, D_CONV), const)],
        out_specs=pl.BlockSpec((TM, D_CONV), lambda i: (i, 0)),
        out_shape=jax.ShapeDtypeStruct((m, D_CONV), f32),
        scratch_shapes=[pltpu.VMEM((TM + 2 * HALO, D_CONV), f32)],
        compiler_params=_cparams(1),
        name="convmod",
    )(hc, hc, hc, dw_w, dw_b.reshape(1, D_CONV), ln_g.reshape(1, D_CONV), ln_b.reshape(1, D_CONV))


def _rwkv_chunk_kernel(f_ref, fp_ref, fn_ref, mu_ref, w0_ref, w2_ref, a0_ref, a2_ref, g2_ref, kk_ref, ka_ref, rk_ref,
                       rp_ref, y0_ref, mm_ref, nn_ref, bv_ref, gg_ref,
                       s_r, s_v, s_a, s_lw, s_k, s_b, *, nbc, nbs):
    first, last = _segment_flags(pl.program_id(0), nbc, nbs)
    f = f_ref[...]
    prow = jnp.where(first, 0.0, fp_ref[HALO - 1:HALO, :])
    nrow = jnp.where(last, 0.0, fn_ref[0:1, :])
    rows = lax.broadcasted_iota(jnp.int32, f.shape, 0)
    prev = jnp.where(rows == 0, prow, pltpu.roll(f, 1, 0))
    nxt = jnp.where(rows == TM - 1, nrow, pltpu.roll(f, TM - 1, 0))
    fs = f + mu_ref[0:1, :] * (prev - f) + mu_ref[1:2, :] * (nxt - f)

    r = fs[:, 0:D_RWKV]
    k = fs[:, D_RWKV:2 * D_RWKV]
    v = fs[:, 2 * D_RWKV:3 * D_RWKV]
    o_w = 3 * D_RWKV
    o_a = o_w + 2 * DECAY_LORA
    o_g = o_a + 2 * AAA_LORA
    ones_bd = _head_ones(D_RWKV, HEAD_DIM)
    gg_ref[...] = _dot(_sigmoid(fs[:, o_g:o_g + GATE_LORA]).astype(bf16), g2_ref[...].astype(bf16))
    kk = k * kk_ref[...]
    kk = kk * lax.rsqrt(_dot_exact_rhs(kk * kk, ones_bd) + NORM_EPS)
    ksum = None
    for d in range(2):
        wd = fs[:, o_w + d * DECAY_LORA:o_w + (d + 1) * DECAY_LORA]
        ad = fs[:, o_a + d * AAA_LORA:o_a + (d + 1) * AAA_LORA]
        w_raw = w0_ref[d:d + 1, :] + _dot(jnp.tanh(wd).astype(bf16), w2_ref[d].astype(bf16))
        z = -w_raw
        softplus = jnp.maximum(z, 0.0) + jnp.log(1.0 + jnp.exp(-jnp.abs(z)))
        s_lw[d] = -jnp.exp(-softplus - 0.5)
        a = _sigmoid(a0_ref[d:d + 1, :] + _dot(ad.astype(bf16), a2_ref[d].astype(bf16)))
        kd = k * (1.0 + (a - 1.0) * ka_ref[...])
        ksum = kd if d == 0 else ksum + kd
        s_k[d] = kd
        s_b[d] = kk * a
    s_r[...] = r
    s_v[...] = v
    s_a[...] = -kk
    bv_ref[...] = _dot_exact_rhs(r * ksum * rk_ref[...], ones_bd) * v

    ri = lax.broadcasted_iota(jnp.int32, (CHUNK, CHUNK), 0)
    ci = lax.broadcasted_iota(jnp.int32, (CHUNK, CHUNK), 1)
    eye = jnp.where(ri == ci, 1.0, 0.0).astype(f32)
    tri = tuple(jnp.where(msk, 1.0, 0.0).astype(bf16) for msk in (ci <= ri, ci >= ri))
    strict = (ci < ri, ci > ri)
    incl = (ci <= ri, ci >= ri)

    def chunk_body(c, carry):
        rs = pl.ds(pl.multiple_of(c * CHUNK, CHUNK), CHUNK)
        r_c = s_r[rs, :]
        v_c = s_v[rs, :]
        a_c = s_a[rs, :]
        v16_c = v_c.astype(bf16)
        at16, rt, bt16, kt16, wc = [], [], [], [], []
        for d in range(2):
            lw = s_lw[d, rs, :]
            cum = _dot_exact_lhs(tri[d], lw)
            er = jnp.exp(cum)
            ei = jnp.exp(-cum)
            at16.append((a_c * jnp.exp(cum - lw)).astype(bf16))
            rt.append(r_c * er)
            bt16.append((s_b[d, rs, :] * ei).astype(bf16))
            kt16.append((s_k[d, rs, :] * ei).astype(bf16))
            wc.append(er[CHUNK - 1:CHUNK, :] if d == 0 else er[0:1, :])

        insts = [(d, h) for d in range(2) for h in range(RWKV_HEADS)]
        hsl = lambda h: slice(h * HEAD_DIM, (h + 1) * HEAD_DIM)
        gms = [_dot_nt(jnp.concatenate([at16[d][:, hsl(h)], rt[d][:, hsl(h)].astype(bf16)], axis=0),
                       jnp.concatenate([bt16[d][:, hsl(h)], kt16[d][:, hsl(h)]], axis=0)) for d, h in insts]
        lab = [jnp.where(strict[d], g[:CHUNK, :CHUNK], 0.0) for (d, h), g in zip(insts, gms)]
        lak16 = [jnp.where(strict[d], g[:CHUNK, CHUNK:], 0.0).astype(bf16) for (d, h), g in zip(insts, gms)]
        lrb16 = [jnp.where(incl[d], g[CHUNK:, :CHUNK], 0.0).astype(bf16) for (d, h), g in zip(insts, gms)]
        lrk16 = [jnp.where(incl[d], g[CHUNK:, CHUNK:], 0.0).astype(bf16) for (d, h), g in zip(insts, gms)]
        lakv = [_dot(lk, v16_c[:, hsl(h)]) for (d, h), lk in zip(insts, lak16)]
        lrkv = [_dot(lk, v16_c[:, hsl(h)]) for (d, h), lk in zip(insts, lrk16)]
        vtk = [_dot_tn(v16_c[:, hsl(h)], kt16[d][:, hsl(h)]) for d, h in insts]
        q16 = [l.astype(bf16) for l in lab]
        z = [jnp.concatenate([at16[d][:, hsl(h)].astype(f32), lv], axis=1) for (d, h), lv in zip(insts, lakv)]
        n_stage = int(math.log2(CHUNK))
        zw = 2 * HEAD_DIM
        for stage in range(n_stage):
            z_hi = [zz.astype(bf16) for zz in z]
            z_lo = [(zz - zh.astype(f32)).astype(bf16) for zz, zh in zip(z, z_hi)]
            if stage < n_stage - 1:
                prod = [_dot(q, jnp.concatenate([q, zh, zl], axis=1)) for q, zh, zl in zip(q16, z_hi, z_lo)]
                q16 = [pr[:, :CHUNK].astype(bf16) for pr in prod]
                z = [zz + (pr[:, CHUNK:CHUNK + zw] + pr[:, CHUNK + zw:]) for zz, pr in zip(z, prod)]
            else:
                prod = [_dot(q, jnp.concatenate([zh, zl], axis=1)) for q, zh, zl in zip(q16, z_hi, z_lo)]
                z = [zz + (pr[:, :zw] + pr[:, zw:]) for zz, pr in zip(z, prod)]
        au16 = [zz.astype(bf16) for zz in z]
        ry = [_dot(lr, au) for lr, au in zip(lrb16, au16)]
        mn = [_dot_tn(au, bt16[d][:, hsl(h)]) for (d, h), au in zip(insts, au16)]
        for i, (d, h) in enumerate(insts):
            hs = hsl(h)
            rp_ref[d, rs, hs] = rt[d][:, hs] + ry[i][:, :HEAD_DIM]
            y0_ref[d, rs, hs] = ry[i][:, HEAD_DIM:] + lrkv[i]
            wc_h = wc[d][:, hs]
            mm_ref[d, c, h] = (eye + mn[i][:HEAD_DIM, :]) * wc_h
            nn_ref[d, c, h] = (mn[i][HEAD_DIM:, :] + vtk[i]) * wc_h
        return carry

    lax.fori_loop(0, TM // CHUNK, chunk_body, 0)


def _rwkv_chunks(fr, mu, w0, w2, a0, a2, g2, kk, ka, rk, nbc, nbs):
    m = fr.shape[0]
    nct = TM // CHUNK
    prev, nxt = _halo_specs(RWKV_IN, m)
    c2 = lambda i: (0, 0)
    c3 = lambda i: (0, 0, 0)
    row = lambda i: (i, 0)
    dirrow = lambda i: (0, i, 0)
    ops = lambda i: (0, i, 0, 0, 0)
    return pl.pallas_call(
        functools.partial(_rwkv_chunk_kernel, nbc=nbc, nbs=nbs),
        grid=(m // TM,),
        in_specs=[pl.BlockSpec((TM, RWKV_IN), row), prev, nxt,
                  pl.BlockSpec((2, RWKV_IN), c2), pl.BlockSpec((2, D_RWKV), c2),
                  pl.BlockSpec((2, DECAY_LORA, D_RWKV), c3), pl.BlockSpec((2, D_RWKV), c2),
                  pl.BlockSpec((2, AAA_LORA, D_RWKV), c3), pl.BlockSpec((GATE_LORA, D_RWKV), c2),
                  pl.BlockSpec((1, D_RWKV), c2), pl.BlockSpec((1, D_RWKV), c2), pl.BlockSpec((1, D_RWKV), c2)],
        out_specs=[pl.BlockSpec((2, TM, D_RWKV), dirrow), pl.BlockSpec((2, TM, D_RWKV), dirrow),
                   pl.BlockSpec((2, nct, RWKV_HEADS, HEAD_DIM, HEAD_DIM), ops),
                   pl.BlockSpec((2, nct, RWKV_HEADS, HEAD_DIM, HEAD_DIM), ops),
                   pl.BlockSpec((TM, D_RWKV), row), pl.BlockSpec((TM, D_RWKV), row)],
        out_shape=[jax.ShapeDtypeStruct((2, m, D_RWKV), f32), jax.ShapeDtypeStruct((2, m, D_RWKV), f32),
                   jax.ShapeDtypeStruct((2, m // CHUNK, RWKV_HEADS, HEAD_DIM, HEAD_DIM), f32),
                   jax.ShapeDtypeStruct((2, m // CHUNK, RWKV_HEADS, HEAD_DIM, HEAD_DIM), f32),
                   jax.ShapeDtypeStruct((m, D_RWKV), f32), jax.ShapeDtypeStruct((m, D_RWKV), f32)],
        scratch_shapes=[pltpu.VMEM((TM, D_RWKV), f32), pltpu.VMEM((TM, D_RWKV), f32), pltpu.VMEM((TM, D_RWKV), f32),
                        pltpu.VMEM((2, TM, D_RWKV), f32), pltpu.VMEM((2, TM, D_RWKV), f32),
                        pltpu.VMEM((2, TM, D_RWKV), f32)],
        compiler_params=_cparams(1),
        name="rwkv_chunks",
    )(fr, fr, fr, mu, w0, w2, a0, a2, g2, kk.reshape(1, D_RWKV), ka.reshape(1, D_RWKV), rk.reshape(1, D_RWKV))


def _rwkv_scan_kernel(rp_ref, y0_ref, mm_ref, nn_ref, bv_ref, gg_ref, lng_ref, lnb_ref, o_ref, y_acc, st_ref, *, ncc,
                      nca):
    y_acc[...] = jnp.zeros(y_acc.shape, f32)
    st_ref[...] = jnp.zeros(st_ref.shape, f32)

    def body(step, carry):
        c_fwd = step
        c_bwd = jnp.where(step < ncc, ncc - 1 - step, nca - 1 - (step - ncc))
        for d, c in ((0, c_fwd), (1, c_bwd)):
            rs = pl.ds(pl.multiple_of(c * CHUNK, CHUNK), CHUNK)
            rp = rp_ref[d, rs, :]
            y0 = y0_ref[d, rs, :]
            ys = []
            for h in range(RWKV_HEADS):
                hs = slice(h * HEAD_DIM, (h + 1) * HEAD_DIM)
                st = st_ref[d * RWKV_HEADS + h]
                s1 = st.astype(bf16)
                s2 = (st - s1.astype(f32)).astype(bf16)
                mt = mm_ref[d, c, h]
                m1 = mt.astype(bf16)
                m2 = (mt - m1.astype(f32)).astype(bf16)
                ys.append(y0[:, hs] + _dot_nt(rp[:, hs].astype(bf16), s1))
                st_ref[d * RWKV_HEADS + h] = _dot(s1, m1) + _dot(s1, m2) + _dot(s2, m1) + nn_ref[d, c, h]
            y_acc[rs, :] += jnp.concatenate(ys, axis=1)
        return carry

    lax.fori_loop(0, nca, body, 0)

    ones_bd = _head_ones(D_RWKV, HEAD_DIM)
    y = y_acc[...]
    mu = _dot_exact_rhs(y, ones_bd) * (1.0 / HEAD_DIM)
    yc = y - mu
    var = _dot_exact_rhs(yc * yc, ones_bd) * (1.0 / HEAD_DIM)
    yn = yc * lax.rsqrt(var + GN_EPS) * lng_ref[...] + lnb_ref[...]
    o_ref[...] = (yn + bv_ref[...]) * gg_ref[...]


def _rwkv_scan(rp, y0, mm, nn, bv, gg, ln_g, ln_b, nbatch, seq, ncc):
    nca = seq // CHUNK
    c2 = lambda b: (0, 0)
    dirrow = lambda b: (0, b, 0)
    ops = lambda b: (0, b, 0, 0, 0)
    row = lambda b: (b, 0)
    return pl.pallas_call(
        functools.partial(_rwkv_scan_kernel, ncc=ncc, nca=nca),
        grid=(nbatch,),
        in_specs=[pl.BlockSpec((2, seq, D_RWKV), dirrow, pipeline_mode=pl.Buffered(1)),
                  pl.BlockSpec((2, seq, D_RWKV), dirrow, pipeline_mode=pl.Buffered(1)),
                  pl.BlockSpec((2, nca, RWKV_HEADS, HEAD_DIM, HEAD_DIM), ops, pipeline_mode=pl.Buffered(1)),
                  pl.BlockSpec((2, nca, RWKV_HEADS, HEAD_DIM, HEAD_DIM), ops, pipeline_mode=pl.Buffered(1)),
                  pl.BlockSpec((seq, D_RWKV), row), pl.BlockSpec((seq, D_RWKV), row),
                  pl.BlockSpec((1, D_RWKV), c2), pl.BlockSpec((1, D_RWKV), c2)],
        out_specs=pl.BlockSpec((seq, D_RWKV), row),
        out_shape=jax.ShapeDtypeStruct((nbatch * seq, D_RWKV), f32),
        scratch_shapes=[pltpu.VMEM((seq, D_RWKV), f32), pltpu.VMEM((2 * RWKV_HEADS, HEAD_DIM, HEAD_DIM), f32)],
        compiler_params=_cparams(1),
        name="rwkv_scan",
    )(rp, y0, mm, nn, bv, gg, ln_g.reshape(1, D_RWKV), ln_b.reshape(1, D_RWKV))


def _mixout_kernel(t_ref, a_ref, b_ref, c_ref, w_ref, m_ref, o_ref):
    out = _dot(a_ref[...].astype(bf16), w_ref[0:D_RWKV, :])
    out = out + _dot(b_ref[...].astype(bf16), w_ref[D_RWKV:D_RWKV + D_CONV, :])
    out = out + _dot(c_ref[...].astype(bf16), w_ref[D_RWKV + D_CONV:, :])
    o_ref[...] = t_ref[...] + m_ref[0, 5:6, :] * out


def _mixout(t, oa, ob, oc, w, mods, grid, row_map, mod_map, c_map, out_rows):
    d = t.shape[1]
    n_axes = len(grid)
    if n_axes == 1:
        out_map = lambda i: (i, 0)
    else:
        nj = grid[1]
        out_map = lambda b, j: (b * nj + j, 0)
    rmap = lambda *a: (row_map(*a), 0)
    return pl.pallas_call(
        _mixout_kernel,
        grid=grid,
        in_specs=[pl.BlockSpec((TM, d), rmap), pl.BlockSpec((TM, D_RWKV), rmap), pl.BlockSpec((TM, D_CONV), rmap),
                  pl.BlockSpec((TM, D_DIFF), lambda *a: (c_map(*a), 0)),
                  pl.BlockSpec((D_RWKV + D_CONV + D_DIFF, d), lambda *a: (0, 0)),
                  pl.BlockSpec((1, N_MOD, d), lambda *a: (mod_map(*a), 0, 0))],
        out_specs=pl.BlockSpec((TM, d), out_map),
        out_shape=jax.ShapeDtypeStruct((out_rows, d), f32),
        compiler_params=_cparams(n_axes),
        name="mixout",
    )(t, oa, ob, oc, w, mods)


def _rope_tables(n_ctx, n_lat):
    n_rows = n_lat // GRID_W
    row = jnp.repeat(jnp.arange(n_rows, dtype=jnp.int32), GRID_W)
    col = jnp.tile(jnp.arange(GRID_W, dtype=jnp.int32), n_rows)
    inv = 1.0 / (ROPE_THETA ** (jnp.arange(ROPE_FREQS, dtype=f32) * 2.0 / AXIS_DIM))
    ang = jnp.stack([row, col], axis=-1).astype(f32)[..., None] * inv
    cos, sin = jnp.cos(ang), jnp.sin(ang)
    cos64 = jnp.concatenate([cos[:, 0], cos[:, 0], cos[:, 1], cos[:, 1]], axis=-1)
    sin64 = jnp.concatenate([-sin[:, 0], sin[:, 0], -sin[:, 1], sin[:, 1]], axis=-1)
    reps = D_QK // DIFF_QK_DIM
    cos_t = jnp.concatenate([jnp.ones((n_ctx, D_QK), f32), jnp.tile(cos64, (1, reps))], axis=0)
    sin_t = jnp.concatenate([jnp.zeros((n_ctx, D_QK), f32), jnp.tile(sin64, (1, reps))], axis=0)
    return cos_t, sin_t


def kernel(x, c, ctx, c_ctx, ada_w, ada_b, norm_g, ffn_w_in, ffn_w_out, mix_w_in, mix_w_out, rwkv_mu, rwkv_w0, rwkv_w2,
           rwkv_a0, rwkv_a2, rwkv_g2, rwkv_kk, rwkv_ka, rwkv_rk, rwkv_ln_g, rwkv_ln_b, conv_dw_w, conv_dw_b, conv_ln_g,
           conv_ln_b, diff_lam, diff_norm_g, final_g):
    nb, n_lat, d = x.shape
    n_ctx = ctx.shape[1]
    depth = ada_w.shape[0]
    seq = n_ctx + n_lat
    assert n_ctx % TM == 0 and n_lat % TM == 0 and n_lat % GRID_W == 0 and seq % n_ctx == 0
    assert nb + 1 <= COND_ROWS
    nbc, nbx, nbs = n_ctx // TM, n_lat // TM, seq // TM
    m = nb * seq

    t = jnp.concatenate([ctx, x], axis=1).reshape(m, d)
    cond = jnp.zeros((COND_ROWS, d), f32).at[:nb].set(c).at[nb].set(c_ctx)
    mods = _adaln(cond, ada_w, ada_b).reshape(depth, COND_ROWS, N_MOD, d)
    cos_t, sin_t = _rope_tables(n_ctx, n_lat)
    w_in16 = ffn_w_in.astype(bf16)
    w_out16 = ffn_w_out.astype(bf16)
    mix_in16 = mix_w_in.astype(bf16)
    mix_out16 = mix_w_out.astype(bf16)

    mod_all = lambda i: jnp.where(i % nbs < nbc, nb, i // nbs)
    row_all = lambda i: i
    row_lat = lambda b, j: b * nbs + nbc + j
    mod_lat = lambda b, j: b

    out = None
    for l in range(depth):
        last = l == depth - 1
        ml = mods[l]
        t = _ffn(t, ml, norm_g[l, 0], w_in16[l, 0], w_out16[l, 0], 0, (nbs * nb,), row_all, mod_all, m)
        fr, hc, q, k, v = _mixin(t, ml, norm_g[l, 1], mix_in16[l], cos_t, sin_t, nbs, mod_all)
        rp, y0, mm, nn, bv, gg = _rwkv_chunks(fr, rwkv_mu[l], rwkv_w0[l], rwkv_w2[l], rwkv_a0[l], rwkv_a2[l], rwkv_g2[l],
                                              rwkv_kk[l], rwkv_ka[l], rwkv_rk[l].reshape(D_RWKV), nbc, nbs)
        oa = _rwkv_scan(rp, y0, mm, nn, bv, gg, rwkv_ln_g[l], rwkv_ln_b[l], nb, seq, n_ctx // CHUNK)
        ob = _conv(hc, conv_dw_w[l], conv_dw_b[l], conv_ln_g[l], conv_ln_b[l], nbc, nbs)
        lam_init = 0.8 - 0.6 * math.exp(-0.3 * l)
        if last:
            oc = _attn(q, k, v, diff_lam[l], diff_norm_g[l], lam_init, nb, n_lat, n_ctx, seq, 0, seq, TM,
                       nb * n_lat, n_lat, 0)
            t = _mixout(t, oa, ob, oc, mix_out16[l], ml, (nb, nbx), row_lat, mod_lat, lambda b, j: b * nbx + j,
                        nb * n_lat)
            out = _ffn(t, ml, norm_g[l, 2], w_in16[l, 1], w_out16[l, 1], 6, (nb * nbx,), row_all, lambda i: i // nbx,
                       nb * n_lat, final_g=final_g)
        else:
            oc_x = _attn(q, k, v, diff_lam[l], diff_norm_g[l], lam_init, nb, n_lat, n_ctx, seq, 0, seq, TM,
                         nb * n_lat, n_lat, 0)
            oc_c = _attn(q, k, v, diff_lam[l], diff_norm_g[l], lam_init, nb, n_ctx, 0, n_ctx, 0, seq, TM,
                         nb * n_ctx, n_ctx, 0)
            oc = jnp.concatenate([oc_c.reshape(nb, n_ctx, D_DIFF), oc_x.reshape(nb, n_lat, D_DIFF)], axis=1)
            oc = oc.reshape(m, D_DIFF)
            t = _mixout(t, oa, ob, oc, mix_out16[l], ml, (nbs * nb,), row_all, mod_all, row_all, m)
            t = _ffn(t, ml, norm_g[l, 2], w_in16[l, 1], w_out16[l, 1], 6, (nbs * nb,), row_all, mod_all, m)
    return out.reshape(nb, n_lat, d)
```

```python
import functools
import math

import jax
import jax.numpy as jnp
from jax import lax
from jax.experimental import pallas as pl
from jax.experimental.pallas import tpu as pltpu

f32 = jnp.float32
bf16 = jnp.bfloat16

N_MOD = 9
EPS = 1e-6
GRID_W = 64
RWKV_HEADS = 4
HEAD_DIM = 64
D_RWKV = RWKV_HEADS * HEAD_DIM
DECAY_LORA = 64
AAA_LORA = 64
GATE_LORA = 128
GN_EPS = 64e-5
NORM_EPS = 1e-12
D_CONV = 256
CONV_WIDTH = 31
CONV_HALF = CONV_WIDTH // 2
DIFF_HEADS = 4
DIFF_QK_DIM = 64
DIFF_V_DIM = 2 * DIFF_QK_DIM
D_DIFF = DIFF_HEADS * DIFF_V_DIM
ROPE_THETA = 10000.0
AXIS_DIM = DIFF_QK_DIM // 2
ROPE_FREQS = AXIS_DIM // 2
RWKV_IN = 3 * D_RWKV + 2 * DECAY_LORA + 2 * AAA_LORA + GATE_LORA
CONV_IN = 2 * D_CONV
D_QK = DIFF_HEADS * 2 * DIFF_QK_DIM
DIFF_IN = 2 * D_QK + D_DIFF
P_IN = RWKV_IN + CONV_IN + DIFF_IN

TM = 256
CHUNK = 64
HALO = 16
COND_ROWS = 16
VMEM_LIMIT = 56 * 1024 * 1024


def _dot(a, b, prec=None):
    return jnp.dot(a, b, preferred_element_type=f32, precision=prec)


def _dot_nt(a, b, prec=None):
    return lax.dot_general(a, b, (((1,), (1,)), ((), ())), preferred_element_type=f32, precision=prec)


def _dot_tn(a, b, prec=None):
    return lax.dot_general(a, b, (((0,), (0,)), ((), ())), preferred_element_type=f32, precision=prec)


def _split3(a):
    a1 = a.astype(bf16)
    r1 = a - a1.astype(f32)
    a2 = r1.astype(bf16)
    a3 = (r1 - a2.astype(f32)).astype(bf16)
    return a1, a2, a3


def _dot_exact_rhs(a, b16):
    a1, a2, a3 = _split3(a)
    return _dot(a1, b16) + _dot(a2, b16) + _dot(a3, b16)


def _dot_exact_lhs(a16, b):
    b1, b2, b3 = _split3(b)
    return _dot(a16, b1) + _dot(a16, b2) + _dot(a16, b3)


def _sigmoid(x):
    return 1.0 / (1.0 + jnp.exp(-x))


def _cparams(n_axes):
    return pltpu.CompilerParams(dimension_semantics=("arbitrary",) * n_axes, vmem_limit_bytes=VMEM_LIMIT)


def _head_ones(n, width):
    r = lax.broadcasted_iota(jnp.int32, (n, n), 0) // width
    c = lax.broadcasted_iota(jnp.int32, (n, n), 1) // width
    return jnp.where(r == c, 1.0, 0.0).astype(bf16)


def _adaln_kernel(c_ref, w_ref, b_ref, o_ref):
    cond = c_ref[...]
    cond = cond * _sigmoid(cond)
    o_ref[0] = _dot(cond.astype(bf16), w_ref[0].astype(bf16)) + b_ref[0]


def _adaln(cond, ada_w, ada_b):
    depth, d, nd = ada_w.shape
    tn = nd // 4
    return pl.pallas_call(
        _adaln_kernel,
        grid=(depth, nd // tn),
        in_specs=[pl.BlockSpec((COND_ROWS, d), lambda l, j: (0, 0)),
                  pl.BlockSpec((1, d, tn), lambda l, j: (l, 0, j)),
                  pl.BlockSpec((1, 1, tn), lambda l, j: (l, 0, j))],
        out_specs=pl.BlockSpec((1, COND_ROWS, tn), lambda l, j: (l, 0, j)),
        out_shape=jax.ShapeDtypeStruct((depth, COND_ROWS, nd), f32),
        compiler_params=_cparams(2),
        name="adaln",
    )(cond, ada_w, ada_b.reshape(depth, 1, nd))


def _modulated_norm(x, g, shift, scale):
    y = x * lax.rsqrt(jnp.mean(x * x, axis=-1, keepdims=True) + EPS) * g
    return y * (1.0 + scale) + shift


def _dense_tile(rows_per_seq):
    for tm in (1024, 768, 512, 256):
        if rows_per_seq % tm == 0:
            return tm
    raise ValueError(rows_per_seq)


def _mod_row(mb_ref, mc_ref, idx, tm, n_ctx, blocks_per_seq):
    if n_ctx == 0:
        return mb_ref[0, idx:idx + 1, :]
    row = (pl.program_id(0) % blocks_per_seq) * tm + lax.broadcasted_iota(jnp.int32, (tm, 1), 0)
    return jnp.where(row < n_ctx, mc_ref[0, idx:idx + 1, :], mb_ref[0, idx:idx + 1, :])


def _ffn_kernel(x_ref, mb_ref, mc_ref, g_ref, wi_ref, wo_ref, *rest, s0, ff, fc, final, tm, n_ctx, bps):
    o_ref = rest[-1]
    x = x_ref[...]
    mod = lambda idx: _mod_row(mb_ref, mc_ref, idx, tm, n_ctx, bps)
    h = _modulated_norm(x, g_ref[...], mod(s0), mod(s0 + 1)).astype(bf16)
    acc = jnp.zeros(x.shape, f32)
    for j in range(ff // fc):
        gt = _dot(h, wi_ref[:, j * fc:(j + 1) * fc])
        up = _dot(h, wi_ref[:, ff + j * fc:ff + (j + 1) * fc])
        act = (gt * _sigmoid(gt) * up).astype(bf16)
        acc = acc + _dot(act, wo_ref[j * fc:(j + 1) * fc, :])
    out = x + 0.5 * mod(s0 + 2) * acc
    if final:
        fg_ref = rest[0]
        out = out * lax.rsqrt(jnp.mean(out * out, axis=-1, keepdims=True) + EPS) * fg_ref[...]
    o_ref[...] = out


def _ffn(t, mods, g, w_in, w_out, s0, rows_per_seq, n_ctx, ctx_row, final_g=None):
    m, d = t.shape
    ff = w_out.shape[0]
    fc = 256 if ff % 256 == 0 else ff
    tm = _dense_tile(rows_per_seq)
    bps = rows_per_seq // tm
    const = lambda i: (0, 0)
    in_specs = [pl.BlockSpec((tm, d), lambda i: (i, 0)),
                pl.BlockSpec((1, N_MOD, d), lambda i: (i // bps, 0, 0)),
                pl.BlockSpec((1, N_MOD, d), lambda i: (ctx_row, 0, 0)),
                pl.BlockSpec((1, d), const),
                pl.BlockSpec((d, 2 * ff), const, pipeline_mode=pl.Buffered(1)),
                pl.BlockSpec((ff, d), const, pipeline_mode=pl.Buffered(1))]
    args = [t, mods, mods, g.reshape(1, d), w_in, w_out]
    if final_g is not None:
        in_specs.append(pl.BlockSpec((1, d), const))
        args.append(final_g.reshape(1, d))
    return pl.pallas_call(
        functools.partial(_ffn_kernel, s0=s0, ff=ff, fc=fc, final=final_g is not None, tm=tm, n_ctx=n_ctx, bps=bps),
        grid=(m // tm,), in_specs=in_specs,
        out_specs=pl.BlockSpec((tm, d), lambda i: (i, 0)),
        out_shape=jax.ShapeDtypeStruct((m, d), f32),
        compiler_params=_cparams(1),
        name="ffn",
    )(*args)


def _mixin_kernel(x_ref, mb_ref, mc_ref, g_ref, w_ref, cos_ref, sin_ref, fr_ref, hc_ref, q_ref, k_ref, v_ref, *, tm,
                  n_ctx, bps):
    x = x_ref[...]
    mod = lambda idx: _mod_row(mb_ref, mc_ref, idx, tm, n_ctx, bps)
    h = _modulated_norm(x, g_ref[...], mod(3), mod(4)).astype(bf16)
    fr_ref[...] = _dot(h, w_ref[:, 0:RWKV_IN])
    o = RWKV_IN
    val = _dot(h, w_ref[:, o:o + D_CONV])
    gate = _dot(h, w_ref[:, o + D_CONV:o + CONV_IN])
    hc_ref[...] = val * _sigmoid(gate)
    o = RWKV_IN + CONV_IN
    cos = cos_ref[...]
    sin = sin_ref[...]
    lane = lax.broadcasted_iota(jnp.int32, cos.shape, 1)
    low_half = ((lane // ROPE_FREQS) % 2) == 0

    def rope(t):
        partner = jnp.where(low_half, pltpu.roll(t, D_QK - ROPE_FREQS, 1), pltpu.roll(t, ROPE_FREQS, 1))
        return t * cos + partner * sin

    q = _dot(h, w_ref[:, o:o + D_QK])
    q_ref[...] = (rope(q) * (DIFF_QK_DIM ** -0.5)).astype(bf16)
    k = _dot(h, w_ref[:, o + D_QK:o + 2 * D_QK])
    k_ref[...] = rope(k).astype(bf16)
    v_ref[...] = _dot(h, w_ref[:, o + 2 * D_QK:o + 2 * D_QK + D_DIFF]).astype(bf16)


def _mixin(t, mods, g, w, cos, sin, seq, n_ctx, ctx_row):
    m, d = t.shape
    tm = _dense_tile(seq)
    bps = seq // tm
    const = lambda i: (0, 0)
    row = lambda i: (i, 0)
    return pl.pallas_call(
        functools.partial(_mixin_kernel, tm=tm, n_ctx=n_ctx, bps=bps),
        grid=(m // tm,),
        in_specs=[pl.BlockSpec((tm, d), row),
                  pl.BlockSpec((1, N_MOD, d), lambda i: (i // bps, 0, 0)),
                  pl.BlockSpec((1, N_MOD, d), lambda i: (ctx_row, 0, 0)),
                  pl.BlockSpec((1, d), const),
                  pl.BlockSpec((d, P_IN), const, pipeline_mode=pl.Buffered(1)),
                  pl.BlockSpec((tm, D_QK), lambda i: (i % bps, 0)),
                  pl.BlockSpec((tm, D_QK), lambda i: (i % bps, 0))],
        out_specs=[pl.BlockSpec((tm, RWKV_IN), row), pl.BlockSpec((tm, D_CONV), row),
                   pl.BlockSpec((tm, D_QK), row), pl.BlockSpec((tm, D_QK), row), pl.BlockSpec((tm, D_DIFF), row)],
        out_shape=[jax.ShapeDtypeStruct((m, RWKV_IN), f32), jax.ShapeDtypeStruct((m, D_CONV), f32),
                   jax.ShapeDtypeStruct((m, D_QK), bf16), jax.ShapeDtypeStruct((m, D_QK), bf16),
                   jax.ShapeDtypeStruct((m, D_DIFF), bf16)],
        compiler_params=_cparams(1),
        name="mixin",
    )(t, mods, mods, g.reshape(1, d), w, cos, sin)


def _attn_kernel(q_ref, k_ref, v_ref, lam_ref, g_ref, o_ref, *, lam_init):
    lv = lam_ref[...]
    lam = (jnp.exp(jnp.sum(lv[0:1] * lv[1:2], keepdims=True)) - jnp.exp(jnp.sum(lv[2:3] * lv[3:4], keepdims=True))
           + lam_init)
    q = q_ref[...]
    attn = None
    for m in range(2):
        sl = slice(m * DIFF_QK_DIM, (m + 1) * DIFF_QK_DIM)
        s = _dot_nt(q[:, sl], k_ref[:, sl])
        e = jnp.exp(s - jnp.max(s, axis=-1, keepdims=True))
        inv = 1.0 / jnp.sum(e, axis=-1, keepdims=True)
        attn = e * inv if m == 0 else attn - e * (lam * inv)
    o = _dot(attn.astype(bf16), v_ref[...])
    o = o * lax.rsqrt(jnp.mean(o * o, axis=-1, keepdims=True) + 1e-5) * g_ref[...]
    o_ref[...] = o * (1.0 - lam_init)


def _attn(q, k, v, lam_vecs, norm_g, lam_init, nbatch, n_q, q_off, n_k, k_off, seq, tq, out_rows, out_seq, out_off):
    nq = n_q // tq
    return pl.pallas_call(
        functools.partial(_attn_kernel, lam_init=lam_init),
        grid=(nbatch, DIFF_HEADS, nq),
        in_specs=[pl.BlockSpec((tq, DIFF_V_DIM), lambda b, h, i: ((b * seq + q_off) // tq + i, h)),
                  pl.BlockSpec((n_k, DIFF_V_DIM), lambda b, h, i: ((b * seq + k_off) // n_k, h)),
                  pl.BlockSpec((n_k, DIFF_V_DIM), lambda b, h, i: ((b * seq + k_off) // n_k, h)),
                  pl.BlockSpec((4, DIFF_QK_DIM), lambda b, h, i: (0, 0)),
                  pl.BlockSpec((1, DIFF_V_DIM), lambda b, h, i: (0, 0))],
        out_specs=pl.BlockSpec((tq, DIFF_V_DIM), lambda b, h, i: ((b * out_seq + out_off) // tq + i, h)),
        out_shape=jax.ShapeDtypeStruct((out_rows, D_DIFF), f32),
        compiler_params=_cparams(3),
        name="diffattn",
    )(q, k, v, lam_vecs, norm_g.reshape(1, DIFF_V_DIM))


def _segment_flags(i, nbc, nbs):
    j = i % nbs
    first = jnp.logical_or(j == 0, j == nbc)
    last = jnp.logical_or(j == nbc - 1, j == nbs - 1)
    return first, last


def _halo_specs(width, n_rows):
    per = TM // HALO
    nblk = n_rows // HALO
    prev = pl.BlockSpec((HALO, width), lambda i: (jnp.maximum(i * per - 1, 0), 0))
    nxt = pl.BlockSpec((HALO, width), lambda i: (jnp.minimum((i + 1) * per, nblk - 1), 0))
    return prev, nxt


def _conv_kernel(h_ref, hp_ref, hn_ref, w_ref, b_ref, g_ref, bb_ref, o_ref, buf, *, nbc, nbs):
    first, last = _segment_flags(pl.program_id(0), nbc, nbs)
    buf[0:HALO, :] = jnp.where(first, 0.0, hp_ref[...])
    buf[HALO:HALO + TM, :] = h_ref[...]
    buf[HALO + TM:HALO + TM + HALO, :] = jnp.where(last, 0.0, hn_ref[...])
    sub = 64
    for r0 in range(0, TM, sub):
        acc = jnp.zeros((sub, D_CONV), f32) + b_ref[...]
        for kk in range(CONV_WIDTH):
            start = r0 + HALO - CONV_HALF + kk
            acc = acc + buf[start:start + sub, :] * w_ref[kk:kk + 1, :]
        mu = jnp.mean(acc, axis=-1, keepdims=True)
        xc = acc - mu
        var = jnp.mean(xc * xc, axis=-1, keepdims=True)
        y = xc * lax.rsqrt(var + 1e-5) * g_ref[...] + bb_ref[...]
        o_ref[r0:r0 + sub, :] = y * _sigmoid(y)


def _conv(hc, dw_w, dw_b, ln_g, ln_b, nbc, nbs):
    m = hc.shape[0]
    prev, nxt = _halo_specs(D_CONV, m)
    const = lambda i: (0, 0)
    return pl.pallas_call(
        functools.partial(_conv_kernel, nbc=nbc, nbs=nbs),
        grid=(m // TM,),
        in_specs=[pl.BlockSpec((TM, D_CONV), lambda i: (i, 0)), prev, nxt,
                  pl.BlockSpec((CONV_WIDTH, D_CONV), const), pl.BlockSpec((1, D_CONV), const),
                  pl.BlockSpec((1, D_CONV), const), pl.BlockSpec((1, D_CONV), const)],
        out_specs=pl.BlockSpec((TM, D_CONV), lambda i: (i, 0)),
        out_shape=jax.ShapeDtypeStruct((m, D_CONV), f32),
        scratch_shapes=[pltpu.VMEM((TM + 2 * HALO, D_CONV), f32)],
        compiler_params=_cparams(1),
        name="convmod",
    )(hc, hc, hc, dw_w, dw_b.reshape(1, D_CONV), ln_g.reshape(1, D_CONV), ln_b.reshape(1, D_CONV))


def _rwkv_chunk_kernel(f_ref, fp_ref, fn_ref, mu_ref, w0_ref, w2_ref, a0_ref, a2_ref, g2_ref, kk_ref, ka_ref, rk_ref,
                       rp_ref, y0_ref, mm_ref, nn_ref, bv_ref, gg_ref,
                       s_r, s_v, s_a, s_lw, s_k, s_b, *, nbc, nbs):
    first, last = _segment_flags(pl.program_id(0), nbc, nbs)
    f = f_ref[...]
    prow = jnp.where(first, 0.0, fp_ref[HALO - 1:HALO, :])
    nrow = jnp.where(last, 0.0, fn_ref[0:1, :])
    rows = lax.broadcasted_iota(jnp.int32, f.shape, 0)
    prev = jnp.where(rows == 0, prow, pltpu.roll(f, 1, 0))
    nxt = jnp.where(rows == TM - 1, nrow, pltpu.roll(f, TM - 1, 0))
    fs = f + mu_ref[0:1, :] * (prev - f) + mu_ref[1:2, :] * (nxt - f)

    r = fs[:, 0:D_RWKV]
    k = fs[:, D_RWKV:2 * D_RWKV]
    v = fs[:, 2 * D_RWKV:3 * D_RWKV]
    o_w = 3 * D_RWKV
    o_a = o_w + 2 * DECAY_LORA
    o_g = o_a + 2 * AAA_LORA
    ones_bd = _head_ones(D_RWKV, HEAD_DIM)
    gg_ref[...] = _dot(_sigmoid(fs[:, o_g:o_g + GATE_LORA]).astype(bf16), g2_ref[...].astype(bf16))
    kk = k * kk_ref[...]
    kk = kk * lax.rsqrt(_dot_exact_rhs(kk * kk, ones_bd) + NORM_EPS)
    ksum = None
    for d in range(2):
        wd = fs[:, o_w + d * DECAY_LORA:o_w + (d + 1) * DECAY_LORA]
        ad = fs[:, o_a + d * AAA_LORA:o_a + (d + 1) * AAA_LORA]
        w_raw = w0_ref[d:d + 1, :] + _dot(jnp.tanh(wd).astype(bf16), w2_ref[d].astype(bf16))
        z = -w_raw
        softplus = jnp.maximum(z, 0.0) + jnp.log(1.0 + jnp.exp(-jnp.abs(z)))
        s_lw[d] = -jnp.exp(-softplus - 0.5)
        a = _sigmoid(a0_ref[d:d + 1, :] + _dot(ad.astype(bf16), a2_ref[d].astype(bf16)))
        kd = k * (1.0 + (a - 1.0) * ka_ref[...])
        ksum = kd if d == 0 else ksum + kd
        s_k[d] = kd
        s_b[d] = kk * a
    s_r[...] = r
    s_v[...] = v
    s_a[...] = -kk
    bv_ref[...] = _dot_exact_rhs(r * ksum * rk_ref[...], ones_bd) * v

    ri = lax.broadcasted_iota(jnp.int32, (CHUNK, CHUNK), 0)
    ci = lax.broadcasted_iota(jnp.int32, (CHUNK, CHUNK), 1)
    eye = jnp.where(ri == ci, 1.0, 0.0).astype(f32)
    tri = tuple(jnp.where(msk, 1.0, 0.0).astype(bf16) for msk in (ci <= ri, ci >= ri))
    strict = (ci < ri, ci > ri)
    incl = (ci <= ri, ci >= ri)

    def chunk_body(c, carry):
        rs = pl.ds(pl.multiple_of(c * CHUNK, CHUNK), CHUNK)
        r_c = s_r[rs, :]
        v_c = s_v[rs, :]
        a_c = s_a[rs, :]
        v16_c = v_c.astype(bf16)
        at16, rt, bt16, kt16, wc = [], [], [], [], []
        for d in range(2):
            lw = s_lw[d, rs, :]
            cum = _dot_exact_lhs(tri[d], lw)
            er = jnp.exp(cum)
            ei = jnp.exp(-cum)
            at16.append((a_c * jnp.exp(cum - lw)).astype(bf16))
            rt.append(r_c * er)
            bt16.append((s_b[d, rs, :] * ei).astype(bf16))
            kt16.append((s_k[d, rs, :] * ei).astype(bf16))
            wc.append(er[CHUNK - 1:CHUNK, :] if d == 0 else er[0:1, :])

        insts = [(d, h) for d in range(2) for h in range(RWKV_HEADS)]
        hsl = lambda h: slice(h * HEAD_DIM, (h + 1) * HEAD_DIM)
        gms = [_dot_nt(jnp.concatenate([at16[d][:, hsl(h)], rt[d][:, hsl(h)].astype(bf16)], axis=0),
                       jnp.concatenate([bt16[d][:, hsl(h)], kt16[d][:, hsl(h)]], axis=0)) for d, h in insts]
        lab = [jnp.where(strict[d], g[:CHUNK, :CHUNK], 0.0) for (d, h), g in zip(insts, gms)]
        lak16 = [jnp.where(strict[d], g[:CHUNK, CHUNK:], 0.0).astype(bf16) for (d, h), g in zip(insts, gms)]
        lrb16 = [jnp.where(incl[d], g[CHUNK:, :CHUNK], 0.0).astype(bf16) for (d, h), g in zip(insts, gms)]
        lrk16 = [jnp.where(incl[d], g[CHUNK:, CHUNK:], 0.0).astype(bf16) for (d, h), g in zip(insts, gms)]
        lakv = [_dot(lk, v16_c[:, hsl(h)]) for (d, h), lk in zip(insts, lak16)]
        lrkv = [_dot(lk, v16_c[:, hsl(h)]) for (d, h), lk in zip(insts, lrk16)]
        vtk = [_dot_tn(v16_c[:, hsl(h)], kt16[d][:, hsl(h)]) for d, h in insts]
        q16 = [l.astype(bf16) for l in lab]
        z = [jnp.concatenate([at16[d][:, hsl(h)].astype(f32), lv], axis=1) for (d, h), lv in zip(insts, lakv)]
        n_stage = int(math.log2(CHUNK))
        zw = 2 * HEAD_DIM
        for stage in range(n_stage):
            z_hi = [zz.astype(bf16) for zz in z]
            z_lo = [(zz - zh.astype(f32)).astype(bf16) for zz, zh in zip(z, z_hi)]
            if stage < n_stage - 1:
                prod = [_dot(q, jnp.concatenate([q, zh, zl], axis=1)) for q, zh, zl in zip(q16, z_hi, z_lo)]
                q16 = [pr[:, :CHUNK].astype(bf16) for pr in prod]
                z = [zz + (pr[:, CHUNK:CHUNK + zw] + pr[:, CHUNK + zw:]) for zz, pr in zip(z, prod)]
            else:
                prod = [_dot(q, jnp.concatenate([zh, zl], axis=1)) for q, zh, zl in zip(q16, z_hi, z_lo)]
                z = [zz + (pr[:, :zw] + pr[:, zw:]) for zz, pr in zip(z, prod)]
        au16 = [zz.astype(bf16) for zz in z]
        ry = [_dot(lr, au) for lr, au in zip(lrb16, au16)]
        mn = [_dot_tn(au, bt16[d][:, hsl(h)]) for (d, h), au in zip(insts, au16)]
        for i, (d, h) in enumerate(insts):
            hs = hsl(h)
            rp_ref[d, rs, hs] = rt[d][:, hs] + ry[i][:, :HEAD_DIM]
            y0_ref[d, rs, hs] = ry[i][:, HEAD_DIM:] + lrkv[i]
            wc_h = wc[d][:, hs]
            mm_ref[d, c, h] = (eye + mn[i][:HEAD_DIM, :]) * wc_h
            nn_ref[d, c, h] = (mn[i][HEAD_DIM:, :] + vtk[i]) * wc_h
        return carry

    lax.fori_loop(0, TM // CHUNK, chunk_body, 0)


def _rwkv_chunks(fr, mu, w0, w2, a0, a2, g2, kk, ka, rk, nbc, nbs):
    m = fr.shape[0]
    nct = TM // CHUNK
    prev, nxt = _halo_specs(RWKV_IN, m)
    c2 = lambda i: (0, 0)
    c3 = lambda i: (0, 0, 0)
    row = lambda i: (i, 0)
    dirrow = lambda i: (0, i, 0)
    ops = lambda i: (0, i, 0, 0, 0)
    return pl.pallas_call(
        functools.partial(_rwkv_chunk_kernel, nbc=nbc, nbs=nbs),
        grid=(m // TM,),
        in_specs=[pl.BlockSpec((TM, RWKV_IN), row), prev, nxt,
                  pl.BlockSpec((2, RWKV_IN), c2), pl.BlockSpec((2, D_RWKV), c2),
                  pl.BlockSpec((2, DECAY_LORA, D_RWKV), c3), pl.BlockSpec((2, D_RWKV), c2),
                  pl.BlockSpec((2, AAA_LORA, D_RWKV), c3), pl.BlockSpec((GATE_LORA, D_RWKV), c2),
                  pl.BlockSpec((1, D_RWKV), c2), pl.BlockSpec((1, D_RWKV), c2), pl.BlockSpec((1, D_RWKV), c2)],
        out_specs=[pl.BlockSpec((2, TM, D_RWKV), dirrow), pl.BlockSpec((2, TM, D_RWKV), dirrow),
                   pl.BlockSpec((2, nct, RWKV_HEADS, HEAD_DIM, HEAD_DIM), ops),
                   pl.BlockSpec((2, nct, RWKV_HEADS, HEAD_DIM, HEAD_DIM), ops),
                   pl.BlockSpec((TM, D_RWKV), row), pl.BlockSpec((TM, D_RWKV), row)],
        out_shape=[jax.ShapeDtypeStruct((2, m, D_RWKV), f32), jax.ShapeDtypeStruct((2, m, D_RWKV), f32),
                   jax.ShapeDtypeStruct((2, m // CHUNK, RWKV_HEADS, HEAD_DIM, HEAD_DIM), f32),
                   jax.ShapeDtypeStruct((2, m // CHUNK, RWKV_HEADS, HEAD_DIM, HEAD_DIM), f32),
                   jax.ShapeDtypeStruct((m, D_RWKV), f32), jax.ShapeDtypeStruct((m, D_RWKV), f32)],
        scratch_shapes=[pltpu.VMEM((TM, D_RWKV), f32), pltpu.VMEM((TM, D_RWKV), f32), pltpu.VMEM((TM, D_RWKV), f32),
                        pltpu.VMEM((2, TM, D_RWKV), f32), pltpu.VMEM((2, TM, D_RWKV), f32),
                        pltpu.VMEM((2, TM, D_RWKV), f32)],
        compiler_params=_cparams(1),
        name="rwkv_chunks",
    )(fr, fr, fr, mu, w0, w2, a0, a2, g2, kk.reshape(1, D_RWKV), ka.reshape(1, D_RWKV), rk.reshape(1, D_RWKV))


def _rwkv_state_kernel(mf_ref, mb_ref, nf_ref, nb_ref, sf_ref, sb_ref, st_ref, *, nbatch):
    @pl.when(pl.program_id(0) == 0)
    def _():
        st_ref[...] = jnp.zeros(st_ref.shape, f32)

    group = 2 * RWKV_HEADS
    probs = [(d, b, h) for b in range(nbatch) for d in range(2) for h in range(RWKV_HEADS)]
    for g0 in range(0, len(probs), group):
        grp = probs[g0:g0 + group]
        st = [st_ref[d, b, h] for d, b, h in grp]
        for (d, b, h), s in zip(grp, st):
            (sf_ref if d == 0 else sb_ref)[b, 0, h] = s
        mt = [(mf_ref if d == 0 else mb_ref)[0, b, 0, h] for d, b, h in grp]
        s1 = [s.astype(bf16) for s in st]
        s2 = [(s - a.astype(f32)).astype(bf16) for s, a in zip(st, s1)]
        m1 = [x.astype(bf16) for x in mt]
        m2 = [(x - a.astype(f32)).astype(bf16) for x, a in zip(mt, m1)]
        new = [_dot(a1, b1) + _dot(a1, b2) + _dot(a2, b1) for a1, a2, b1, b2 in zip(s1, s2, m1, m2)]
        for (d, b, h), s in zip(grp, new):
            st_ref[d, b, h] = s + (nf_ref if d == 0 else nb_ref)[0, b, 0, h]


def _rwkv_state(mm, nn, nbatch, nca, ncc):
    shp = (2, nbatch, nca, RWKV_HEADS, HEAD_DIM, HEAD_DIM)
    blk = (1, nbatch, 1, RWKV_HEADS, HEAD_DIM, HEAD_DIM)
    c_bwd = lambda s: jnp.where(s < ncc, ncc - 1 - s, nca - 1 - (s - ncc))
    fwd = pl.BlockSpec(blk, lambda s: (0, 0, s, 0, 0, 0))
    bwd = pl.BlockSpec(blk, lambda s: (1, 0, c_bwd(s), 0, 0, 0))
    oshape = jax.ShapeDtypeStruct(shp[1:], f32)
    return pl.pallas_call(
        functools.partial(_rwkv_state_kernel, nbatch=nbatch),
        grid=(nca,),
        in_specs=[fwd, bwd, fwd, bwd],
        out_specs=[pl.BlockSpec(blk[1:], lambda s: (0, s, 0, 0, 0)),
                   pl.BlockSpec(blk[1:], lambda s: (0, c_bwd(s), 0, 0, 0))],
        out_shape=[oshape, oshape],
        scratch_shapes=[pltpu.VMEM((2, nbatch, RWKV_HEADS, HEAD_DIM, HEAD_DIM), f32)],
        compiler_params=_cparams(1),
        name="rwkv_state",
    )(mm.reshape(shp), mm.reshape(shp), nn.reshape(shp), nn.reshape(shp))


def _rwkv_out_kernel(rp_ref, y0_ref, sf_ref, sb_ref, bv_ref, gg_ref, lng_ref, lnb_ref, o_ref):
    rows = []
    for c in range(TM // CHUNK):
        rs = slice(c * CHUNK, (c + 1) * CHUNK)
        cols = []
        for h in range(RWKV_HEADS):
            hs = slice(h * HEAD_DIM, (h + 1) * HEAD_DIM)
            yh = y0_ref[0, rs, hs] + y0_ref[1, rs, hs]
            yh = yh + _dot_nt(rp_ref[0, rs, hs].astype(bf16), sf_ref[c, h].astype(bf16))
            yh = yh + _dot_nt(rp_ref[1, rs, hs].astype(bf16), sb_ref[c, h].astype(bf16))
            cols.append(yh)
        rows.append(jnp.concatenate(cols, axis=1))
    y = jnp.concatenate(rows, axis=0)
    ones_bd = _head_ones(D_RWKV, HEAD_DIM)
    mu = _dot_exact_rhs(y, ones_bd) * (1.0 / HEAD_DIM)
    yc = y - mu
    var = _dot_exact_rhs(yc * yc, ones_bd) * (1.0 / HEAD_DIM)
    yn = yc * lax.rsqrt(var + GN_EPS) * lng_ref[...] + lnb_ref[...]
    o_ref[...] = (yn + bv_ref[...]) * gg_ref[...]


def _rwkv_out(rp, y0, sf, sb, bv, gg, ln_g, ln_b):
    m = bv.shape[0]
    nct = TM // CHUNK
    c2 = lambda i: (0, 0)
    row = lambda i: (i, 0)
    dirrow = lambda i: (0, i, 0)
    st = lambda i: (i, 0, 0, 0)
    sshape = (m // CHUNK, RWKV_HEADS, HEAD_DIM, HEAD_DIM)
    return pl.pallas_call(
        _rwkv_out_kernel,
        grid=(m // TM,),
        in_specs=[pl.BlockSpec((2, TM, D_RWKV), dirrow), pl.BlockSpec((2, TM, D_RWKV), dirrow),
                  pl.BlockSpec((nct, RWKV_HEADS, HEAD_DIM, HEAD_DIM), st),
                  pl.BlockSpec((nct, RWKV_HEADS, HEAD_DIM, HEAD_DIM), st),
                  pl.BlockSpec((TM, D_RWKV), row), pl.BlockSpec((TM, D_RWKV), row),
                  pl.BlockSpec((1, D_RWKV), c2), pl.BlockSpec((1, D_RWKV), c2)],
        out_specs=pl.BlockSpec((TM, D_RWKV), row),
        out_shape=jax.ShapeDtypeStruct((m, D_RWKV), f32),
        compiler_params=_cparams(1),
        name="rwkv_out",
    )(rp, y0, sf.reshape(sshape), sb.reshape(sshape), bv, gg, ln_g.reshape(1, D_RWKV), ln_b.reshape(1, D_RWKV))


def _mixout_kernel(t_ref, a_ref, b_ref, c_ref, w_ref, m_ref, o_ref):
    out = _dot(a_ref[...].astype(bf16), w_ref[0:D_RWKV, :])
    out = out + _dot(b_ref[...].astype(bf16), w_ref[D_RWKV:D_RWKV + D_CONV, :])
    out = out + _dot(c_ref[...].astype(bf16), w_ref[D_RWKV + D_CONV:, :])
    o_ref[...] = t_ref[...] + m_ref[0, 5:6, :] * out


def _mixout(t, oa, ob, oc, w, mods, grid, row_map, mod_map, c_map, out_rows):
    d = t.shape[1]
    n_axes = len(grid)
    if n_axes == 1:
        out_map = lambda i: (i, 0)
    else:
        nj = grid[1]
        out_map = lambda b, j: (b * nj + j, 0)
    rmap = lambda *a: (row_map(*a), 0)
    return pl.pallas_call(
        _mixout_kernel,
        grid=grid,
        in_specs=[pl.BlockSpec((TM, d), rmap), pl.BlockSpec((TM, D_RWKV), rmap), pl.BlockSpec((TM, D_CONV), rmap),
                  pl.BlockSpec((TM, D_DIFF), lambda *a: (c_map(*a), 0)),
                  pl.BlockSpec((D_RWKV + D_CONV + D_DIFF, d), lambda *a: (0, 0)),
                  pl.BlockSpec((1, N_MOD, d), lambda *a: (mod_map(*a), 0, 0))],
        out_specs=pl.BlockSpec((TM, d), out_map),
        out_shape=jax.ShapeDtypeStruct((out_rows, d), f32),
        compiler_params=_cparams(n_axes),
        name="mixout",
    )(t, oa, ob, oc, w, mods)


def _rope_tables(n_ctx, n_lat):
    n_rows = n_lat // GRID_W
    row = jnp.repeat(jnp.arange(n_rows, dtype=jnp.int32), GRID_W)
    col = jnp.tile(jnp.arange(GRID_W, dtype=jnp.int32), n_rows)
    inv = 1.0 / (ROPE_THETA ** (jnp.arange(ROPE_FREQS, dtype=f32) * 2.0 / AXIS_DIM))
    ang = jnp.stack([row, col], axis=-1).astype(f32)[..., None] * inv
    cos, sin = jnp.cos(ang), jnp.sin(ang)
    cos64 = jnp.concatenate([cos[:, 0], cos[:, 0], cos[:, 1], cos[:, 1]], axis=-1)
    sin64 = jnp.concatenate([-sin[:, 0], sin[:, 0], -sin[:, 1], sin[:, 1]], axis=-1)
    reps = D_QK // DIFF_QK_DIM
    cos_t = jnp.concatenate([jnp.ones((n_ctx, D_QK), f32), jnp.tile(cos64, (1, reps))], axis=0)
    sin_t = jnp.concatenate([jnp.zeros((n_ctx, D_QK), f32), jnp.tile(sin64, (1, reps))], axis=0)
    return cos_t, sin_t


def kernel(x, c, ctx, c_ctx, ada_w, ada_b, norm_g, ffn_w_in, ffn_w_out, mix_w_in, mix_w_out, rwkv_mu, rwkv_w0, rwkv_w2,
           rwkv_a0, rwkv_a2, rwkv_g2, rwkv_kk, rwkv_ka, rwkv_rk, rwkv_ln_g, rwkv_ln_b, conv_dw_w, conv_dw_b, conv_ln_g,
           conv_ln_b, diff_lam, diff_norm_g, final_g):
    nb, n_lat, d = x.shape
    n_ctx = ctx.shape[1]
    depth = ada_w.shape[0]
    seq = n_ctx + n_lat
    assert n_ctx % TM == 0 and n_lat % TM == 0 and n_lat % GRID_W == 0 and seq % n_ctx == 0
    assert nb + 1 <= COND_ROWS
    nbc, nbx, nbs = n_ctx // TM, n_lat // TM, seq // TM
    m = nb * seq

    t = jnp.concatenate([ctx, x], axis=1).reshape(m, d)
    cond = jnp.zeros((COND_ROWS, d), f32).at[:nb].set(c).at[nb].set(c_ctx)
    mods = _adaln(cond, ada_w, ada_b).reshape(depth, COND_ROWS, N_MOD, d)
    cos_t, sin_t = _rope_tables(n_ctx, n_lat)
    w_in16 = ffn_w_in.astype(bf16)
    w_out16 = ffn_w_out.astype(bf16)
    mix_in16 = mix_w_in.astype(bf16)
    mix_out16 = mix_w_out.astype(bf16)

    mod_all = lambda i: jnp.where(i % nbs < nbc, nb, i // nbs)
    row_all = lambda i: i
    row_lat = lambda b, j: b * nbs + nbc + j
    mod_lat = lambda b, j: b

    out = None
    for l in range(depth):
        last = l == depth - 1
        ml = mods[l]
        t = _ffn(t, ml, norm_g[l, 0], w_in16[l, 0], w_out16[l, 0], 0, seq, n_ctx, nb)
        fr, hc, q, k, v = _mixin(t, ml, norm_g[l, 1], mix_in16[l], cos_t, sin_t, seq, n_ctx, nb)
        rp, y0, mm, nn, bv, gg = _rwkv_chunks(fr, rwkv_mu[l], rwkv_w0[l], rwkv_w2[l], rwkv_a0[l], rwkv_a2[l], rwkv_g2[l],
                                              rwkv_kk[l], rwkv_ka[l], rwkv_rk[l].reshape(D_RWKV), nbc, nbs)
        sf, sb = _rwkv_state(mm, nn, nb, seq // CHUNK, n_ctx // CHUNK)
        oa = _rwkv_out(rp, y0, sf, sb, bv, gg, rwkv_ln_g[l], rwkv_ln_b[l])
        ob = _conv(hc, conv_dw_w[l], conv_dw_b[l], conv_ln_g[l], conv_ln_b[l], nbc, nbs)
        lam_init = 0.8 - 0.6 * math.exp(-0.3 * l)
        if last:
            oc = _attn(q, k, v, diff_lam[l], diff_norm_g[l], lam_init, nb, n_lat, n_ctx, seq, 0, seq, TM,
                       nb * n_lat, n_lat, 0)
            t = _mixout(t, oa, ob, oc, mix_out16[l], ml, (nb, nbx), row_lat, mod_lat, lambda b, j: b * nbx + j,
                        nb * n_lat)
            out = _ffn(t, ml, norm_g[l, 2], w_in16[l, 1], w_out16[l, 1], 6, n_lat, 0, nb, final_g=final_g)
        else:
            oc_x = _attn(q, k, v, diff_lam[l], diff_norm_g[l], lam_init, nb, n_lat, n_ctx, seq, 0, seq, TM,
                         nb * n_lat, n_lat, 0)
            oc_c = _attn(q, k, v, diff_lam[l], diff_norm_g[l], lam_init, nb, n_ctx, 0, n_ctx, 0, seq, TM,
                         nb * n_ctx, n_ctx, 0)
            oc = jnp.concatenate([oc_c.reshape(nb, n_ctx, D_DIFF), oc_x.reshape(nb, n_lat, D_DIFF)], axis=1)
            oc = oc.reshape(m, D_DIFF)
            t = _mixout(t, oa, ob, oc, mix_out16[l], ml, (nbs * nb,), row_all, mod_all, row_all, m)
            t = _ffn(t, ml, norm_g[l, 2], w_in16[l, 1], w_out16[l, 1], 6, seq, n_ctx, nb)
    return out.reshape(nb, n_lat, d)
```

```python
import functools
import math

import jax
import jax.numpy as jnp
from jax import lax
from jax.experimental import pallas as pl
from jax.experimental.pallas import tpu as pltpu

f32 = jnp.float32
bf16 = jnp.bfloat16

N_MOD = 9
EPS = 1e-6
GRID_W = 64
RWKV_HEADS = 4
HEAD_DIM = 64
D_RWKV = RWKV_HEADS * HEAD_DIM
PAIR = 2 * HEAD_DIM
N_PAIR = RWKV_HEADS // 2
DECAY_LORA = 64
AAA_LORA = 64
GATE_LORA = 128
GN_EPS = 64e-5
NORM_EPS = 1e-12
D_CONV = 256
CONV_WIDTH = 31
CONV_HALF = CONV_WIDTH // 2
DIFF_HEADS = 4
DIFF_QK_DIM = 64
DIFF_V_DIM = 2 * DIFF_QK_DIM
D_DIFF = DIFF_HEADS * DIFF_V_DIM
ROPE_THETA = 10000.0
AXIS_DIM = DIFF_QK_DIM // 2
ROPE_FREQS = AXIS_DIM // 2
RWKV_IN = 3 * D_RWKV + 2 * DECAY_LORA + 2 * AAA_LORA + GATE_LORA
CONV_IN = 2 * D_CONV
D_QK = DIFF_HEADS * 2 * DIFF_QK_DIM
DIFF_IN = 2 * D_QK + D_DIFF
P_IN = RWKV_IN + CONV_IN + DIFF_IN

TM = 256
CHUNK = 64
HALO = 16
COND_ROWS = 16
VMEM_LIMIT = 56 * 1024 * 1024


def _dot(a, b, prec=None):
    return jnp.dot(a, b, preferred_element_type=f32, precision=prec)


def _dot_nt(a, b, prec=None):
    return lax.dot_general(a, b, (((1,), (1,)), ((), ())), preferred_element_type=f32, precision=prec)


def _dot_tn(a, b, prec=None):
    return lax.dot_general(a, b, (((0,), (0,)), ((), ())), preferred_element_type=f32, precision=prec)


def _split3(a):
    a1 = a.astype(bf16)
    r1 = a - a1.astype(f32)
    a2 = r1.astype(bf16)
    a3 = (r1 - a2.astype(f32)).astype(bf16)
    return a1, a2, a3


def _dot_exact_rhs(a, b16):
    a1, a2, a3 = _split3(a)
    return _dot(a1, b16) + _dot(a2, b16) + _dot(a3, b16)


def _dot_exact_lhs(a16, b):
    b1, b2, b3 = _split3(b)
    return _dot(a16, b1) + _dot(a16, b2) + _dot(a16, b3)


def _pair_blockdiag(x):
    first = (lax.broadcasted_iota(jnp.int32, x.shape, 1) % PAIR) < HEAD_DIM
    zero = jnp.zeros_like(x)
    return jnp.concatenate([jnp.where(first, x, zero), jnp.where(first, zero, x)], axis=0)


def _sigmoid(x):
    return 1.0 / (1.0 + jnp.exp(-x))


def _cparams(n_axes):
    return pltpu.CompilerParams(dimension_semantics=("arbitrary",) * n_axes, vmem_limit_bytes=VMEM_LIMIT)


def _head_ones(n, width):
    r = lax.broadcasted_iota(jnp.int32, (n, n), 0) // width
    c = lax.broadcasted_iota(jnp.int32, (n, n), 1) // width
    return jnp.where(r == c, 1.0, 0.0).astype(bf16)


def _adaln_kernel(c_ref, w_ref, b_ref, o_ref):
    cond = c_ref[...]
    cond = cond * _sigmoid(cond)
    o_ref[0] = _dot(cond.astype(bf16), w_ref[0].astype(bf16)) + b_ref[0]


def _adaln(cond, ada_w, ada_b):
    depth, d, nd = ada_w.shape
    tn = nd // 4
    return pl.pallas_call(
        _adaln_kernel,
        grid=(depth, nd // tn),
        in_specs=[pl.BlockSpec((COND_ROWS, d), lambda l, j: (0, 0)),
                  pl.BlockSpec((1, d, tn), lambda l, j: (l, 0, j)),
                  pl.BlockSpec((1, 1, tn), lambda l, j: (l, 0, j))],
        out_specs=pl.BlockSpec((1, COND_ROWS, tn), lambda l, j: (l, 0, j)),
        out_shape=jax.ShapeDtypeStruct((depth, COND_ROWS, nd), f32),
        compiler_params=_cparams(2),
        name="adaln",
    )(cond, ada_w, ada_b.reshape(depth, 1, nd))


def _modulated_norm(x, g, shift, scale):
    y = x * lax.rsqrt(jnp.mean(x * x, axis=-1, keepdims=True) + EPS) * g
    return y * (1.0 + scale) + shift


def _dense_tile(rows_per_seq):
    for tm in (1024, 768, 512, 256):
        if rows_per_seq % tm == 0:
            return tm
    raise ValueError(rows_per_seq)


def _mod_row(mb_ref, mc_ref, idx, tm, n_ctx, blocks_per_seq):
    if n_ctx == 0:
        return mb_ref[0, idx:idx + 1, :]
    row = (pl.program_id(0) % blocks_per_seq) * tm + lax.broadcasted_iota(jnp.int32, (tm, 1), 0)
    return jnp.where(row < n_ctx, mc_ref[0, idx:idx + 1, :], mb_ref[0, idx:idx + 1, :])


def _ffn_kernel(x_ref, mb_ref, mc_ref, g_ref, wi_ref, wo_ref, *rest, s0, ff, fc, final, tm, n_ctx, bps):
    o_ref = rest[-1]
    x = x_ref[...]
    mod = lambda idx: _mod_row(mb_ref, mc_ref, idx, tm, n_ctx, bps)
    h = _modulated_norm(x, g_ref[...], mod(s0), mod(s0 + 1)).astype(bf16)
    acc = jnp.zeros(x.shape, f32)
    for j in range(ff // fc):
        gt = _dot(h, wi_ref[:, j * fc:(j + 1) * fc])
        up = _dot(h, wi_ref[:, ff + j * fc:ff + (j + 1) * fc])
        act = (gt * _sigmoid(gt) * up).astype(bf16)
        acc = acc + _dot(act, wo_ref[j * fc:(j + 1) * fc, :])
    out = x + 0.5 * mod(s0 + 2) * acc
    if final:
        fg_ref = rest[0]
        out = out * lax.rsqrt(jnp.mean(out * out, axis=-1, keepdims=True) + EPS) * fg_ref[...]
    o_ref[...] = out


def _ffn(t, mods, g, w_in, w_out, s0, rows_per_seq, n_ctx, ctx_row, final_g=None):
    m, d = t.shape
    ff = w_out.shape[0]
    fc = 256 if ff % 256 == 0 else ff
    tm = _dense_tile(rows_per_seq)
    bps = rows_per_seq // tm
    const = lambda i: (0, 0)
    in_specs = [pl.BlockSpec((tm, d), lambda i: (i, 0)),
                pl.BlockSpec((1, N_MOD, d), lambda i: (i // bps, 0, 0)),
                pl.BlockSpec((1, N_MOD, d), lambda i: (ctx_row, 0, 0)),
                pl.BlockSpec((1, d), const),
                pl.BlockSpec((d, 2 * ff), const, pipeline_mode=pl.Buffered(1)),
                pl.BlockSpec((ff, d), const, pipeline_mode=pl.Buffered(1))]
    args = [t, mods, mods, g.reshape(1, d), w_in, w_out]
    if final_g is not None:
        in_specs.append(pl.BlockSpec((1, d), const))
        args.append(final_g.reshape(1, d))
    return pl.pallas_call(
        functools.partial(_ffn_kernel, s0=s0, ff=ff, fc=fc, final=final_g is not None, tm=tm, n_ctx=n_ctx, bps=bps),
        grid=(m // tm,), in_specs=in_specs,
        out_specs=pl.BlockSpec((tm, d), lambda i: (i, 0)),
        out_shape=jax.ShapeDtypeStruct((m, d), f32),
        compiler_params=_cparams(1),
        name="ffn",
    )(*args)


def _mixin_kernel(x_ref, mb_ref, mc_ref, g_ref, w_ref, cos_ref, sin_ref, fr_ref, hc_ref, q_ref, k_ref, v_ref, *, tm,
                  n_ctx, bps):
    x = x_ref[...]
    mod = lambda idx: _mod_row(mb_ref, mc_ref, idx, tm, n_ctx, bps)
    h = _modulated_norm(x, g_ref[...], mod(3), mod(4)).astype(bf16)
    fr_ref[...] = _dot(h, w_ref[:, 0:RWKV_IN])
    o = RWKV_IN
    val = _dot(h, w_ref[:, o:o + D_CONV])
    gate = _dot(h, w_ref[:, o + D_CONV:o + CONV_IN])
    hc_ref[...] = val * _sigmoid(gate)
    o = RWKV_IN + CONV_IN
    cos = cos_ref[...]
    sin = sin_ref[...]
    lane = lax.broadcasted_iota(jnp.int32, cos.shape, 1)
    low_half = ((lane // ROPE_FREQS) % 2) == 0

    def rope(t):
        partner = jnp.where(low_half, pltpu.roll(t, D_QK - ROPE_FREQS, 1), pltpu.roll(t, ROPE_FREQS, 1))
        return t * cos + partner * sin

    q = _dot(h, w_ref[:, o:o + D_QK])
    q_ref[...] = (rope(q) * (DIFF_QK_DIM ** -0.5)).astype(bf16)
    k = _dot(h, w_ref[:, o + D_QK:o + 2 * D_QK])
    k_ref[...] = rope(k).astype(bf16)
    v_ref[...] = _dot(h, w_ref[:, o + 2 * D_QK:o + 2 * D_QK + D_DIFF]).astype(bf16)


def _mixin(t, mods, g, w, cos, sin, seq, n_ctx, ctx_row):
    m, d = t.shape
    tm = _dense_tile(seq)
    bps = seq // tm
    const = lambda i: (0, 0)
    row = lambda i: (i, 0)
    return pl.pallas_call(
        functools.partial(_mixin_kernel, tm=tm, n_ctx=n_ctx, bps=bps),
        grid=(m // tm,),
        in_specs=[pl.BlockSpec((tm, d), row),
                  pl.BlockSpec((1, N_MOD, d), lambda i: (i // bps, 0, 0)),
                  pl.BlockSpec((1, N_MOD, d), lambda i: (ctx_row, 0, 0)),
                  pl.BlockSpec((1, d), const),
                  pl.BlockSpec((d, P_IN), const, pipeline_mode=pl.Buffered(1)),
                  pl.BlockSpec((tm, D_QK), lambda i: (i % bps, 0)),
                  pl.BlockSpec((tm, D_QK), lambda i: (i % bps, 0))],
        out_specs=[pl.BlockSpec((tm, RWKV_IN), row), pl.BlockSpec((tm, D_CONV), row),
                   pl.BlockSpec((tm, D_QK), row), pl.BlockSpec((tm, D_QK), row), pl.BlockSpec((tm, D_DIFF), row)],
        out_shape=[jax.ShapeDtypeStruct((m, RWKV_IN), f32), jax.ShapeDtypeStruct((m, D_CONV), f32),
                   jax.ShapeDtypeStruct((m, D_QK), bf16), jax.ShapeDtypeStruct((m, D_QK), bf16),
                   jax.ShapeDtypeStruct((m, D_DIFF), bf16)],
        compiler_params=_cparams(1),
        name="mixin",
    )(t, mods, mods, g.reshape(1, d), w, cos, sin)


def _attn_kernel(q_ref, k_ref, v_ref, lam_ref, g_ref, o_ref, *, lam_init):
    lv = lam_ref[...]
    lam = (jnp.exp(jnp.sum(lv[0:1] * lv[1:2], keepdims=True)) - jnp.exp(jnp.sum(lv[2:3] * lv[3:4], keepdims=True))
           + lam_init)
    q = q_ref[...]
    attn = None
    for m in range(2):
        sl = slice(m * DIFF_QK_DIM, (m + 1) * DIFF_QK_DIM)
        s = _dot_nt(q[:, sl], k_ref[:, sl])
        e = jnp.exp(s - jnp.max(s, axis=-1, keepdims=True))
        inv = 1.0 / jnp.sum(e, axis=-1, keepdims=True)
        attn = e * inv if m == 0 else attn - e * (lam * inv)
    o = _dot(attn.astype(bf16), v_ref[...])
    o = o * lax.rsqrt(jnp.mean(o * o, axis=-1, keepdims=True) + 1e-5) * g_ref[...]
    o_ref[...] = o * (1.0 - lam_init)


def _attn(q, k, v, lam_vecs, norm_g, lam_init, nbatch, n_q, q_off, n_k, k_off, seq, tq, out_rows, out_seq, out_off):
    nq = n_q // tq
    return pl.pallas_call(
        functools.partial(_attn_kernel, lam_init=lam_init),
        grid=(nbatch, DIFF_HEADS, nq),
        in_specs=[pl.BlockSpec((tq, DIFF_V_DIM), lambda b, h, i: ((b * seq + q_off) // tq + i, h)),
                  pl.BlockSpec((n_k, DIFF_V_DIM), lambda b, h, i: ((b * seq + k_off) // n_k, h)),
                  pl.BlockSpec((n_k, DIFF_V_DIM), lambda b, h, i: ((b * seq + k_off) // n_k, h)),
                  pl.BlockSpec((4, DIFF_QK_DIM), lambda b, h, i: (0, 0)),
                  pl.BlockSpec((1, DIFF_V_DIM), lambda b, h, i: (0, 0))],
        out_specs=pl.BlockSpec((tq, DIFF_V_DIM), lambda b, h, i: ((b * out_seq + out_off) // tq + i, h)),
        out_shape=jax.ShapeDtypeStruct((out_rows, D_DIFF), f32),
        compiler_params=_cparams(3),
        name="diffattn",
    )(q, k, v, lam_vecs, norm_g.reshape(1, DIFF_V_DIM))


def _segment_flags(i, nbc, nbs):
    j = i % nbs
    first = jnp.logical_or(j == 0, j == nbc)
    last = jnp.logical_or(j == nbc - 1, j == nbs - 1)
    return first, last


def _halo_specs(width, n_rows):
    per = TM // HALO
    nblk = n_rows // HALO
    prev = pl.BlockSpec((HALO, width), lambda i: (jnp.maximum(i * per - 1, 0), 0))
    nxt = pl.BlockSpec((HALO, width), lambda i: (jnp.minimum((i + 1) * per, nblk - 1), 0))
    return prev, nxt


def _conv_kernel(h_ref, hp_ref, hn_ref, w_ref, b_ref, g_ref, bb_ref, o_ref, buf, *, nbc, nbs):
    first, last = _segment_flags(pl.program_id(0), nbc, nbs)
    buf[0:HALO, :] = jnp.where(first, 0.0, hp_ref[...])
    buf[HALO:HALO + TM, :] = h_ref[...]
    buf[HALO + TM:HALO + TM + HALO, :] = jnp.where(last, 0.0, hn_ref[...])
    sub = 64
    for r0 in range(0, TM, sub):
        acc = jnp.zeros((sub, D_CONV), f32) + b_ref[...]
        for kk in range(CONV_WIDTH):
            start = r0 + HALO - CONV_HALF + kk
            acc = acc + buf[start:start + sub, :] * w_ref[kk:kk + 1, :]
        mu = jnp.mean(acc, axis=-1, keepdims=True)
        xc = acc - mu
        var = jnp.mean(xc * xc, axis=-1, keepdims=True)
        y = xc * lax.rsqrt(var + 1e-5) * g_ref[...] + bb_ref[...]
        o_ref[r0:r0 + sub, :] = y * _sigmoid(y)


def _conv(hc, dw_w, dw_b, ln_g, ln_b, nbc, nbs):
    m = hc.shape[0]
    prev, nxt = _halo_specs(D_CONV, m)
    const = lambda i: (0, 0)
    return pl.pallas_call(
        functools.partial(_conv_kernel, nbc=nbc, nbs=nbs),
        grid=(m // TM,),
        in_specs=[pl.BlockSpec((TM, D_CONV), lambda i: (i, 0)), prev, nxt,
                  pl.BlockSpec((CONV_WIDTH, D_CONV), const), pl.BlockSpec((1, D_CONV), const),
                  pl.BlockSpec((1, D_CONV), const), pl.BlockSpec((1, D_CONV), const)],
        out_specs=pl.BlockSpec((TM, D_CONV), lambda i: (i, 0)),
        out_shape=jax.ShapeDtypeStruct((m, D_CONV), f32),
        scratch_shapes=[pltpu.VMEM((TM + 2 * HALO, D_CONV), f32)],
        compiler_params=_cparams(1),
        name="convmod",
    )(hc, hc, hc, dw_w, dw_b.reshape(1, D_CONV), ln_g.reshape(1, D_CONV), ln_b.reshape(1, D_CONV))


def _rwkv_chunk_kernel(f_ref, fp_ref, fn_ref, mu_ref, w0_ref, w2_ref, a0_ref, a2_ref, g2_ref, kk_ref, ka_ref, rk_ref,
                       rp_ref, y0_ref, mm_ref, nn_ref, bv_ref, gg_ref,
                       s_r, s_v, s_a, s_lw, s_k, s_b, *, nbc, nbs):
    first, last = _segment_flags(pl.program_id(0), nbc, nbs)
    f = f_ref[...]
    prow = jnp.where(first, 0.0, fp_ref[HALO - 1:HALO, :])
    nrow = jnp.where(last, 0.0, fn_ref[0:1, :])
    rows = lax.broadcasted_iota(jnp.int32, f.shape, 0)
    prev = jnp.where(rows == 0, prow, pltpu.roll(f, 1, 0))
    nxt = jnp.where(rows == TM - 1, nrow, pltpu.roll(f, TM - 1, 0))
    fs = f + mu_ref[0:1, :] * (prev - f) + mu_ref[1:2, :] * (nxt - f)

    r = fs[:, 0:D_RWKV]
    k = fs[:, D_RWKV:2 * D_RWKV]
    v = fs[:, 2 * D_RWKV:3 * D_RWKV]
    o_w = 3 * D_RWKV
    o_a = o_w + 2 * DECAY_LORA
    o_g = o_a + 2 * AAA_LORA
    ones_bd = _head_ones(D_RWKV, HEAD_DIM)
    gg_ref[...] = _dot(_sigmoid(fs[:, o_g:o_g + GATE_LORA]).astype(bf16), g2_ref[...].astype(bf16))
    kk = k * kk_ref[...]
    kk = kk * lax.rsqrt(_dot_exact_rhs(kk * kk, ones_bd) + NORM_EPS)
    ksum = None
    for d in range(2):
        wd = fs[:, o_w + d * DECAY_LORA:o_w + (d + 1) * DECAY_LORA]
        ad = fs[:, o_a + d * AAA_LORA:o_a + (d + 1) * AAA_LORA]
        w_raw = w0_ref[d:d + 1, :] + _dot(jnp.tanh(wd).astype(bf16), w2_ref[d].astype(bf16))
        z = -w_raw
        softplus = jnp.maximum(z, 0.0) + jnp.log(1.0 + jnp.exp(-jnp.abs(z)))
        s_lw[d] = -jnp.exp(-softplus - 0.5)
        a = _sigmoid(a0_ref[d:d + 1, :] + _dot(ad.astype(bf16), a2_ref[d].astype(bf16)))
        kd = k * (1.0 + (a - 1.0) * ka_ref[...])
        ksum = kd if d == 0 else ksum + kd
        s_k[d] = kd
        s_b[d] = kk * a
    s_r[...] = r
    s_v[...] = v
    s_a[...] = -kk
    bv_ref[...] = _dot_exact_rhs(r * ksum * rk_ref[...], ones_bd) * v

    ri = lax.broadcasted_iota(jnp.int32, (CHUNK, CHUNK), 0)
    ci = lax.broadcasted_iota(jnp.int32, (CHUNK, CHUNK), 1)
    tri = tuple(jnp.where(msk, 1.0, 0.0).astype(bf16) for msk in (ci <= ri, ci >= ri))
    rp_i = lax.broadcasted_iota(jnp.int32, (CHUNK, PAIR), 0)
    cp_i = lax.broadcasted_iota(jnp.int32, (CHUNK, PAIR), 1) % CHUNK
    eye = jnp.where(rp_i == cp_i, 1.0, 0.0).astype(f32)
    strict = (cp_i < rp_i, cp_i > rp_i)
    incl = (cp_i <= rp_i, cp_i >= rp_i)
    low = lax.broadcasted_iota(jnp.int32, (CHUNK, PAIR), 1) < HEAD_DIM

    def chunk_body(c, carry):
        rs = pl.ds(pl.multiple_of(c * CHUNK, CHUNK), CHUNK)
        r_c = s_r[rs, :]
        v_c = s_v[rs, :]
        a_c = s_a[rs, :]
        v16_c = v_c.astype(bf16)
        at16, rt, bt16, kt16, wc = [], [], [], [], []
        for d in range(2):
            lw = s_lw[d, rs, :]
            cum = _dot_exact_lhs(tri[d], lw)
            er = jnp.exp(cum)
            ei = jnp.exp(-cum)
            at16.append((a_c * jnp.exp(cum - lw)).astype(bf16))
            rt.append(r_c * er)
            bt16.append((s_b[d, rs, :] * ei).astype(bf16))
            kt16.append((s_k[d, rs, :] * ei).astype(bf16))
            wc.append(er[CHUNK - 1:CHUNK, :] if d == 0 else er[0:1, :])

        insts = [(d, p) for d in range(2) for p in range(RWKV_HEADS // 2)]
        psl = lambda p: slice(p * PAIR, (p + 1) * PAIR)
        gms = [_dot_nt(jnp.concatenate([at16[d][:, psl(p)], rt[d][:, psl(p)].astype(bf16)], axis=0),
                       jnp.concatenate([_pair_blockdiag(bt16[d][:, psl(p)]), _pair_blockdiag(kt16[d][:, psl(p)])],
                                       axis=0)) for d, p in insts]
        lab = [jnp.where(strict[d], g[:CHUNK, :PAIR], 0.0) for (d, p), g in zip(insts, gms)]
        lakrk16 = [jnp.concatenate([jnp.where(strict[d], g[:CHUNK, PAIR:], 0.0),
                                    jnp.where(incl[d], g[CHUNK:, PAIR:], 0.0)], axis=0).astype(bf16)
                   for (d, p), g in zip(insts, gms)]
        lrb16 = [jnp.where(incl[d], g[CHUNK:, :PAIR], 0.0).astype(bf16) for (d, p), g in zip(insts, gms)]
        uv = [_dot(lk, _pair_blockdiag(v16_c[:, psl(p)])) for (d, p), lk in zip(insts, lakrk16)]
        vtk = [_dot_tn(v16_c[:, psl(p)], kt16[d][:, psl(p)]) for d, p in insts]
        q = lab
        z = [jnp.concatenate([at16[d][:, psl(p)].astype(f32), u[:CHUNK]], axis=1) for (d, p), u in zip(insts, uv)]
        n_stage = int(math.log2(CHUNK))
        zw = 2 * PAIR
        for stage in range(n_stage):
            z_hi = [zz.astype(bf16) for zz in z]
            z_lo = [(zz - zh.astype(f32)).astype(bf16) for zz, zh in zip(z, z_hi)]
            q_hi = [qq.astype(bf16) for qq in q]
            q_lo = [(qq - qh.astype(f32)).astype(bf16) for qq, qh in zip(q, q_hi)]
            if stage < n_stage - 1:
                pa = [_dot(qh, _pair_blockdiag(jnp.concatenate([zh, zl, qh, ql], axis=1)))
                      for qh, ql, zh, zl in zip(q_hi, q_lo, z_hi, z_lo)]
                pb = [_dot(ql, _pair_blockdiag(jnp.concatenate([zh, qh], axis=1)))
                      for qh, ql, zh in zip(q_hi, q_lo, z_hi)]
                q = [a[:, 2 * zw:2 * zw + PAIR] + a[:, 2 * zw + PAIR:] + b[:, zw:] for a, b in zip(pa, pb)]
            else:
                pa = [_dot(qh, _pair_blockdiag(jnp.concatenate([zh, zl], axis=1)))
                      for qh, zh, zl in zip(q_hi, z_hi, z_lo)]
                pb = [_dot(ql, _pair_blockdiag(zh)) for ql, zh in zip(q_lo, z_hi)]
            z = [zz + (a[:, :zw] + a[:, zw:2 * zw] + b[:, :zw]) for zz, a, b in zip(z, pa, pb)]
        au16 = [zz.astype(bf16) for zz in z]
        ry = [_dot(lr, _pair_blockdiag(au)) for lr, au in zip(lrb16, au16)]
        mn = [_dot_tn(au, bt16[d][:, psl(p)]) for (d, p), au in zip(insts, au16)]
        for i, (d, p) in enumerate(insts):
            ps = psl(p)
            rp_ref[d, rs, ps] = rt[d][:, ps] + ry[i][:, :PAIR]
            y0_ref[d, rs, ps] = ry[i][:, PAIR:] + uv[i][CHUNK:]
            wc_p = wc[d][:, ps]
            m_pair = jnp.where(low, mn[i][0:CHUNK], mn[i][CHUNK:2 * CHUNK])
            n_pair = jnp.where(low, mn[i][2 * CHUNK:3 * CHUNK], mn[i][3 * CHUNK:])
            k_pair = jnp.where(low, vtk[i][:CHUNK], vtk[i][CHUNK:])
            mm_ref[d, c, p] = (eye + m_pair) * wc_p
            nn_ref[d, c, p] = (n_pair + k_pair) * wc_p
        return carry

    lax.fori_loop(0, TM // CHUNK, chunk_body, 0)


def _rwkv_chunks(fr, mu, w0, w2, a0, a2, g2, kk, ka, rk, nbc, nbs):
    m = fr.shape[0]
    nct = TM // CHUNK
    prev, nxt = _halo_specs(RWKV_IN, m)
    c2 = lambda i: (0, 0)
    c3 = lambda i: (0, 0, 0)
    row = lambda i: (i, 0)
    dirrow = lambda i: (0, i, 0)
    ops = lambda i: (0, i, 0, 0, 0)
    return pl.pallas_call(
        functools.partial(_rwkv_chunk_kernel, nbc=nbc, nbs=nbs),
        grid=(m // TM,),
        in_specs=[pl.BlockSpec((TM, RWKV_IN), row), prev, nxt,
                  pl.BlockSpec((2, RWKV_IN), c2), pl.BlockSpec((2, D_RWKV), c2),
                  pl.BlockSpec((2, DECAY_LORA, D_RWKV), c3), pl.BlockSpec((2, D_RWKV), c2),
                  pl.BlockSpec((2, AAA_LORA, D_RWKV), c3), pl.BlockSpec((GATE_LORA, D_RWKV), c2),
                  pl.BlockSpec((1, D_RWKV), c2), pl.BlockSpec((1, D_RWKV), c2), pl.BlockSpec((1, D_RWKV), c2)],
        out_specs=[pl.BlockSpec((2, TM, D_RWKV), dirrow), pl.BlockSpec((2, TM, D_RWKV), dirrow),
                   pl.BlockSpec((2, nct, N_PAIR, HEAD_DIM, PAIR), ops),
                   pl.BlockSpec((2, nct, N_PAIR, HEAD_DIM, PAIR), ops),
                   pl.BlockSpec((TM, D_RWKV), row), pl.BlockSpec((TM, D_RWKV), row)],
        out_shape=[jax.ShapeDtypeStruct((2, m, D_RWKV), f32), jax.ShapeDtypeStruct((2, m, D_RWKV), f32),
                   jax.ShapeDtypeStruct((2, m // CHUNK, N_PAIR, HEAD_DIM, PAIR), f32),
                   jax.ShapeDtypeStruct((2, m // CHUNK, N_PAIR, HEAD_DIM, PAIR), f32),
                   jax.ShapeDtypeStruct((m, D_RWKV), f32), jax.ShapeDtypeStruct((m, D_RWKV), f32)],
        scratch_shapes=[pltpu.VMEM((TM, D_RWKV), f32), pltpu.VMEM((TM, D_RWKV), f32), pltpu.VMEM((TM, D_RWKV), f32),
                        pltpu.VMEM((2, TM, D_RWKV), f32), pltpu.VMEM((2, TM, D_RWKV), f32),
                        pltpu.VMEM((2, TM, D_RWKV), f32)],
        compiler_params=_cparams(1),
        name="rwkv_chunks",
    )(fr, fr, fr, mu, w0, w2, a0, a2, g2, kk.reshape(1, D_RWKV), ka.reshape(1, D_RWKV), rk.reshape(1, D_RWKV))


def _rwkv_state_kernel(mf_ref, mb_ref, nf_ref, nb_ref, sf_ref, sb_ref, st_ref, *, nbatch):
    @pl.when(pl.program_id(0) == 0)
    def _():
        st_ref[...] = jnp.zeros(st_ref.shape, f32)

    group = 8
    probs = [(d, b, p) for b in range(nbatch) for d in range(2) for p in range(N_PAIR)]
    for g0 in range(0, len(probs), group):
        grp = probs[g0:g0 + group]
        st = [st_ref[d, b, p] for d, b, p in grp]
        for (d, b, p), s in zip(grp, st):
            (sf_ref if d == 0 else sb_ref)[b, 0, p] = s
        mt = [(mf_ref if d == 0 else mb_ref)[0, b, 0, p] for d, b, p in grp]
        s1 = [s.astype(bf16) for s in st]
        s2 = [(s - a.astype(f32)).astype(bf16) for s, a in zip(st, s1)]
        m1 = [x.astype(bf16) for x in mt]
        m2 = [(x - a.astype(f32)).astype(bf16) for x, a in zip(mt, m1)]
        p12 = [_dot(a1, _pair_blockdiag(jnp.concatenate([b1, b2], axis=1))) for a1, b1, b2 in zip(s1, m1, m2)]
        p21 = [_dot(a2, _pair_blockdiag(b1)) for a2, b1 in zip(s2, m1)]
        for (d, b, p), x12, x21 in zip(grp, p12, p21):
            st_ref[d, b, p] = (x12[:, :PAIR] + x12[:, PAIR:] + x21) + (nf_ref if d == 0 else nb_ref)[0, b, 0, p]


def _rwkv_state(mm, nn, nbatch, nca, ncc):
    shp = (2, nbatch, nca, N_PAIR, HEAD_DIM, PAIR)
    blk = (1, nbatch, 1, N_PAIR, HEAD_DIM, PAIR)
    c_bwd = lambda s: jnp.where(s < ncc, ncc - 1 - s, nca - 1 - (s - ncc))
    fwd = pl.BlockSpec(blk, lambda s: (0, 0, s, 0, 0, 0))
    bwd = pl.BlockSpec(blk, lambda s: (1, 0, c_bwd(s), 0, 0, 0))
    oshape = jax.ShapeDtypeStruct(shp[1:], f32)
    return pl.pallas_call(
        functools.partial(_rwkv_state_kernel, nbatch=nbatch),
        grid=(nca,),
        in_specs=[fwd, bwd, fwd, bwd],
        out_specs=[pl.BlockSpec(blk[1:], lambda s: (0, s, 0, 0, 0)),
                   pl.BlockSpec(blk[1:], lambda s: (0, c_bwd(s), 0, 0, 0))],
        out_shape=[oshape, oshape],
        scratch_shapes=[pltpu.VMEM((2, nbatch, N_PAIR, HEAD_DIM, PAIR), f32)],
        compiler_params=_cparams(1),
        name="rwkv_state",
    )(mm.reshape(shp), mm.reshape(shp), nn.reshape(shp), nn.reshape(shp))


def _rwkv_out_kernel(rp_ref, y0_ref, sf_ref, sb_ref, bv_ref, gg_ref, lng_ref, lnb_ref, o_ref):
    rows = []
    for c in range(TM // CHUNK):
        rs = slice(c * CHUNK, (c + 1) * CHUNK)
        cols = []
        for p in range(N_PAIR):
            ps = slice(p * PAIR, (p + 1) * PAIR)
            lhs = jnp.concatenate([rp_ref[0, rs, ps], rp_ref[1, rs, ps]], axis=1).astype(bf16)
            rhs = jnp.concatenate([_pair_blockdiag(sf_ref[c, p].astype(bf16)),
                                   _pair_blockdiag(sb_ref[c, p].astype(bf16))], axis=1)
            cols.append(y0_ref[0, rs, ps] + y0_ref[1, rs, ps] + _dot_nt(lhs, rhs))
        rows.append(jnp.concatenate(cols, axis=1))
    y = jnp.concatenate(rows, axis=0)
    ones_bd = _head_ones(D_RWKV, HEAD_DIM)
    mu = _dot_exact_rhs(y, ones_bd) * (1.0 / HEAD_DIM)
    yc = y - mu
    var = _dot_exact_rhs(yc * yc, ones_bd) * (1.0 / HEAD_DIM)
    yn = yc * lax.rsqrt(var + GN_EPS) * lng_ref[...] + lnb_ref[...]
    o_ref[...] = (yn + bv_ref[...]) * gg_ref[...]


def _rwkv_out(rp, y0, sf, sb, bv, gg, ln_g, ln_b):
    m = bv.shape[0]
    nct = TM // CHUNK
    c2 = lambda i: (0, 0)
    row = lambda i: (i, 0)
    dirrow = lambda i: (0, i, 0)
    st = lambda i: (i, 0, 0, 0)
    sshape = (m // CHUNK, N_PAIR, HEAD_DIM, PAIR)
    return pl.pallas_call(
        _rwkv_out_kernel,
        grid=(m // TM,),
        in_specs=[pl.BlockSpec((2, TM, D_RWKV), dirrow), pl.BlockSpec((2, TM, D_RWKV), dirrow),
                  pl.BlockSpec((nct, N_PAIR, HEAD_DIM, PAIR), st),
                  pl.BlockSpec((nct, N_PAIR, HEAD_DIM, PAIR), st),
                  pl.BlockSpec((TM, D_RWKV), row), pl.BlockSpec((TM, D_RWKV), row),
                  pl.BlockSpec((1, D_RWKV), c2), pl.BlockSpec((1, D_RWKV), c2)],
        out_specs=pl.BlockSpec((TM, D_RWKV), row),
        out_shape=jax.ShapeDtypeStruct((m, D_RWKV), f32),
        compiler_params=_cparams(1),
        name="rwkv_out",
    )(rp, y0, sf.reshape(sshape), sb.reshape(sshape), bv, gg, ln_g.reshape(1, D_RWKV), ln_b.reshape(1, D_RWKV))


def _mixout_kernel(t_ref, a_ref, b_ref, c_ref, w_ref, m_ref, o_ref):
    out = _dot(a_ref[...].astype(bf16), w_ref[0:D_RWKV, :])
    out = out + _dot(b_ref[...].astype(bf16), w_ref[D_RWKV:D_RWKV + D_CONV, :])
    out = out + _dot(c_ref[...].astype(bf16), w_ref[D_RWKV + D_CONV:, :])
    o_ref[...] = t_ref[...] + m_ref[0, 5:6, :] * out


def _mixout(t, oa, ob, oc, w, mods, grid, row_map, mod_map, c_map, out_rows):
    d = t.shape[1]
    n_axes = len(grid)
    if n_axes == 1:
        out_map = lambda i: (i, 0)
    else:
        nj = grid[1]
        out_map = lambda b, j: (b * nj + j, 0)
    rmap = lambda *a: (row_map(*a), 0)
    return pl.pallas_call(
        _mixout_kernel,
        grid=grid,
        in_specs=[pl.BlockSpec((TM, d), rmap), pl.BlockSpec((TM, D_RWKV), rmap), pl.BlockSpec((TM, D_CONV), rmap),
                  pl.BlockSpec((TM, D_DIFF), lambda *a: (c_map(*a), 0)),
                  pl.BlockSpec((D_RWKV + D_CONV + D_DIFF, d), lambda *a: (0, 0)),
                  pl.BlockSpec((1, N_MOD, d), lambda *a: (mod_map(*a), 0, 0))],
        out_specs=pl.BlockSpec((TM, d), out_map),
        out_shape=jax.ShapeDtypeStruct((out_rows, d), f32),
        compiler_params=_cparams(n_axes),
        name="mixout",
    )(t, oa, ob, oc, w, mods)


def _rope_tables(n_ctx, n_lat):
    n_rows = n_lat // GRID_W
    row = jnp.repeat(jnp.arange(n_rows, dtype=jnp.int32), GRID_W)
    col = jnp.tile(jnp.arange(GRID_W, dtype=jnp.int32), n_rows)
    inv = 1.0 / (ROPE_THETA ** (jnp.arange(ROPE_FREQS, dtype=f32) * 2.0 / AXIS_DIM))
    ang = jnp.stack([row, col], axis=-1).astype(f32)[..., None] * inv
    cos, sin = jnp.cos(ang), jnp.sin(ang)
    cos64 = jnp.concatenate([cos[:, 0], cos[:, 0], cos[:, 1], cos[:, 1]], axis=-1)
    sin64 = jnp.concatenate([-sin[:, 0], sin[:, 0], -sin[:, 1], sin[:, 1]], axis=-1)
    reps = D_QK // DIFF_QK_DIM
    cos_t = jnp.concatenate([jnp.ones((n_ctx, D_QK), f32), jnp.tile(cos64, (1, reps))], axis=0)
    sin_t = jnp.concatenate([jnp.zeros((n_ctx, D_QK), f32), jnp.tile(sin64, (1, reps))], axis=0)
    return cos_t, sin_t


def kernel(x, c, ctx, c_ctx, ada_w, ada_b, norm_g, ffn_w_in, ffn_w_out, mix_w_in, mix_w_out, rwkv_mu, rwkv_w0, rwkv_w2,
           rwkv_a0, rwkv_a2, rwkv_g2, rwkv_kk, rwkv_ka, rwkv_rk, rwkv_ln_g, rwkv_ln_b, conv_dw_w, conv_dw_b, conv_ln_g,
           conv_ln_b, diff_lam, diff_norm_g, final_g):
    nb, n_lat, d = x.shape
    n_ctx = ctx.shape[1]
    depth = ada_w.shape[0]
    seq = n_ctx + n_lat
    assert n_ctx % TM == 0 and n_lat % TM == 0 and n_lat % GRID_W == 0 and seq % n_ctx == 0
    assert nb + 1 <= COND_ROWS
    nbc, nbx, nbs = n_ctx // TM, n_lat // TM, seq // TM
    m = nb * seq

    t = jnp.concatenate([ctx, x], axis=1).reshape(m, d)
    cond = jnp.zeros((COND_ROWS, d), f32).at[:nb].set(c).at[nb].set(c_ctx)
    mods = _adaln(cond, ada_w, ada_b).reshape(depth, COND_ROWS, N_MOD, d)
    cos_t, sin_t = _rope_tables(n_ctx, n_lat)
    w_in16 = ffn_w_in.astype(bf16)
    w_out16 = ffn_w_out.astype(bf16)
    mix_in16 = mix_w_in.astype(bf16)
    mix_out16 = mix_w_out.astype(bf16)

    mod_all = lambda i: jnp.where(i % nbs < nbc, nb, i // nbs)
    row_all = lambda i: i
    row_lat = lambda b, j: b * nbs + nbc + j
    mod_lat = lambda b, j: b

    out = None
    for l in range(depth):
        last = l == depth - 1
        ml = mods[l]
        t = _ffn(t, ml, norm_g[l, 0], w_in16[l, 0], w_out16[l, 0], 0, seq, n_ctx, nb)
        fr, hc, q, k, v = _mixin(t, ml, norm_g[l, 1], mix_in16[l], cos_t, sin_t, seq, n_ctx, nb)
        rp, y0, mm, nn, bv, gg = _rwkv_chunks(fr, rwkv_mu[l], rwkv_w0[l], rwkv_w2[l], rwkv_a0[l], rwkv_a2[l], rwkv_g2[l],
                                              rwkv_kk[l], rwkv_ka[l], rwkv_rk[l].reshape(D_RWKV), nbc, nbs)
        sf, sb = _rwkv_state(mm, nn, nb, seq // CHUNK, n_ctx // CHUNK)
        oa = _rwkv_out(rp, y0, sf, sb, bv, gg, rwkv_ln_g[l], rwkv_ln_b[l])
        ob = _conv(hc, conv_dw_w[l], conv_dw_b[l], conv_ln_g[l], conv_ln_b[l], nbc, nbs)
        lam_init = 0.8 - 0.6 * math.exp(-0.3 * l)
        if last:
            oc = _attn(q, k, v, diff_lam[l], diff_norm_g[l], lam_init, nb, n_lat, n_ctx, seq, 0, seq, TM,
                       nb * n_lat, n_lat, 0)
            t = _mixout(t, oa, ob, oc, mix_out16[l], ml, (nb, nbx), row_lat, mod_lat, lambda b, j: b * nbx + j,
                        nb * n_lat)
            out = _ffn(t, ml, norm_g[l, 2], w_in16[l, 1], w_out16[l, 1], 6, n_lat, 0, nb, final_g=final_g)
        else:
            oc_x = _attn(q, k, v, diff_lam[l], diff_norm_g[l], lam_init, nb, n_lat, n_ctx, seq, 0, seq, TM,
                         nb * n_lat, n_lat, 0)
            oc_c = _attn(q, k, v, diff_lam[l], diff_norm_g[l], lam_init, nb, n_ctx, 0, n_ctx, 0, seq, TM,
                         nb * n_ctx, n_ctx, 0)
            oc = jnp.concatenate([oc_c.reshape(nb, n_ctx, D_DIFF), oc_x.reshape(nb, n_lat, D_DIFF)], axis=1)
            oc = oc.reshape(m, D_DIFF)
            t = _mixout(t, oa, ob, oc, mix_out16[l], ml, (nbs * nb,), row_all, mod_all, row_all, m)
            t = _ffn(t, ml, norm_g[l, 2], w_in16[l, 1], w_out16[l, 1], 6, seq, n_ctx, nb)
    return out.reshape(nb, n_lat, d)
```

```python
import functools
import math

import jax
import jax.numpy as jnp
from jax import lax
from jax.experimental import pallas as pl
from jax.experimental.pallas import tpu as pltpu

f32 = jnp.float32
bf16 = jnp.bfloat16

N_MOD = 9
EPS = 1e-6
GRID_W = 64
RWKV_HEADS = 4
HEAD_DIM = 64
D_RWKV = RWKV_HEADS * HEAD_DIM
PAIR = 2 * HEAD_DIM
N_PAIR = RWKV_HEADS // 2
DECAY_LORA = 64
AAA_LORA = 64
GATE_LORA = 128
GN_EPS = 64e-5
NORM_EPS = 1e-12
D_CONV = 256
CONV_WIDTH = 31
CONV_HALF = CONV_WIDTH // 2
DIFF_HEADS = 4
DIFF_QK_DIM = 64
DIFF_V_DIM = 2 * DIFF_QK_DIM
D_DIFF = DIFF_HEADS * DIFF_V_DIM
ROPE_THETA = 10000.0
LOG2_E = 1.4426950408889634
AXIS_DIM = DIFF_QK_DIM // 2
ROPE_FREQS = AXIS_DIM // 2
RWKV_IN = 3 * D_RWKV + 2 * DECAY_LORA + 2 * AAA_LORA + GATE_LORA
CONV_IN = 2 * D_CONV
D_QK = DIFF_HEADS * 2 * DIFF_QK_DIM
DIFF_IN = 2 * D_QK + D_DIFF
P_IN = RWKV_IN + CONV_IN + DIFF_IN

TM = 256
CHUNK = 64
HALO = 16
COND_ROWS = 16
VMEM_LIMIT = 56 * 1024 * 1024


def _dot(a, b, prec=None):
    return jnp.dot(a, b, preferred_element_type=f32, precision=prec)


def _dot_nt(a, b, prec=None):
    return lax.dot_general(a, b, (((1,), (1,)), ((), ())), preferred_element_type=f32, precision=prec)


def _dot_tn(a, b, prec=None):
    return lax.dot_general(a, b, (((0,), (0,)), ((), ())), preferred_element_type=f32, precision=prec)


def _split3(a):
    a1 = a.astype(bf16)
    r1 = a - a1.astype(f32)
    a2 = r1.astype(bf16)
    a3 = (r1 - a2.astype(f32)).astype(bf16)
    return a1, a2, a3


def _dot_exact_rhs(a, b16):
    a1, a2, a3 = _split3(a)
    return _dot(a1, b16) + _dot(a2, b16) + _dot(a3, b16)


def _dot_exact_lhs(a16, b):
    b1, b2, b3 = _split3(b)
    return _dot(a16, b1) + _dot(a16, b2) + _dot(a16, b3)


def _pair_blockdiag(x):
    first = (lax.broadcasted_iota(jnp.int32, x.shape, 1) % PAIR) < HEAD_DIM
    zero = jnp.zeros_like(x)
    return jnp.concatenate([jnp.where(first, x, zero), jnp.where(first, zero, x)], axis=0)


def _sigmoid(x):
    return 1.0 / (1.0 + jnp.exp(-x))


def _cparams(n_axes):
    return pltpu.CompilerParams(dimension_semantics=("arbitrary",) * n_axes, vmem_limit_bytes=VMEM_LIMIT)


def _head_ones(n, width):
    r = lax.broadcasted_iota(jnp.int32, (n, n), 0) // width
    c = lax.broadcasted_iota(jnp.int32, (n, n), 1) // width
    return jnp.where(r == c, 1.0, 0.0).astype(bf16)


def _adaln_kernel(c_ref, w_ref, b_ref, o_ref):
    cond = c_ref[...]
    cond = cond * _sigmoid(cond)
    o_ref[0] = _dot(cond.astype(bf16), w_ref[0].astype(bf16)) + b_ref[0]


def _adaln(cond, ada_w, ada_b):
    depth, d, nd = ada_w.shape
    tn = nd // 4
    return pl.pallas_call(
        _adaln_kernel,
        grid=(depth, nd // tn),
        in_specs=[pl.BlockSpec((COND_ROWS, d), lambda l, j: (0, 0)),
                  pl.BlockSpec((1, d, tn), lambda l, j: (l, 0, j)),
                  pl.BlockSpec((1, 1, tn), lambda l, j: (l, 0, j))],
        out_specs=pl.BlockSpec((1, COND_ROWS, tn), lambda l, j: (l, 0, j)),
        out_shape=jax.ShapeDtypeStruct((depth, COND_ROWS, nd), f32),
        compiler_params=_cparams(2),
        name="adaln",
    )(cond, ada_w, ada_b.reshape(depth, 1, nd))


def _modulated_norm(x, g, shift, scale):
    y = x * lax.rsqrt(jnp.mean(x * x, axis=-1, keepdims=True) + EPS) * g
    return y * (1.0 + scale) + shift


def _dense_tile(rows_per_seq):
    for tm in (1024, 768, 512, 256):
        if rows_per_seq % tm == 0:
            return tm
    raise ValueError(rows_per_seq)


def _mod_row(mb_ref, mc_ref, idx, tm, n_ctx, blocks_per_seq):
    if n_ctx == 0:
        return mb_ref[0, idx:idx + 1, :]
    row = (pl.program_id(0) % blocks_per_seq) * tm + lax.broadcasted_iota(jnp.int32, (tm, 1), 0)
    return jnp.where(row < n_ctx, mc_ref[0, idx:idx + 1, :], mb_ref[0, idx:idx + 1, :])


def _ffn_kernel(x_ref, mb_ref, mc_ref, g_ref, wi_ref, wo_ref, *rest, s0, ff, fc, final, tm, n_ctx, bps):
    o_ref = rest[-1]
    x = x_ref[...]
    mod = lambda idx: _mod_row(mb_ref, mc_ref, idx, tm, n_ctx, bps)
    h = _modulated_norm(x, g_ref[...], mod(s0), mod(s0 + 1)).astype(bf16)
    acc = jnp.zeros(x.shape, f32)
    for j in range(ff // fc):
        gt = _dot(h, wi_ref[:, j * fc:(j + 1) * fc])
        up = _dot(h, wi_ref[:, ff + j * fc:ff + (j + 1) * fc])
        act = (gt * _sigmoid(gt) * up).astype(bf16)
        acc = acc + _dot(act, wo_ref[j * fc:(j + 1) * fc, :])
    out = x + 0.5 * mod(s0 + 2) * acc
    if final:
        fg_ref = rest[0]
        out = out * lax.rsqrt(jnp.mean(out * out, axis=-1, keepdims=True) + EPS) * fg_ref[...]
    o_ref[...] = out


def _ffn(t, mods, g, w_in, w_out, s0, rows_per_seq, n_ctx, ctx_row, final_g=None):
    m, d = t.shape
    ff = w_out.shape[0]
    fc = 256 if ff % 256 == 0 else ff
    tm = _dense_tile(rows_per_seq)
    bps = rows_per_seq // tm
    const = lambda i: (0, 0)
    in_specs = [pl.BlockSpec((tm, d), lambda i: (i, 0)),
                pl.BlockSpec((1, N_MOD, d), lambda i: (i // bps, 0, 0)),
                pl.BlockSpec((1, N_MOD, d), lambda i: (ctx_row, 0, 0)),
                pl.BlockSpec((1, d), const),
                pl.BlockSpec((d, 2 * ff), const, pipeline_mode=pl.Buffered(1)),
                pl.BlockSpec((ff, d), const, pipeline_mode=pl.Buffered(1))]
    args = [t, mods, mods, g.reshape(1, d), w_in, w_out]
    if final_g is not None:
        in_specs.append(pl.BlockSpec((1, d), const))
        args.append(final_g.reshape(1, d))
    return pl.pallas_call(
        functools.partial(_ffn_kernel, s0=s0, ff=ff, fc=fc, final=final_g is not None, tm=tm, n_ctx=n_ctx, bps=bps),
        grid=(m // tm,), in_specs=in_specs,
        out_specs=pl.BlockSpec((tm, d), lambda i: (i, 0)),
        out_shape=jax.ShapeDtypeStruct((m, d), f32),
        compiler_params=_cparams(1),
        name="ffn",
    )(*args)


def _mixin_kernel(x_ref, mb_ref, mc_ref, g_ref, w_ref, cos_ref, sin_ref, fr_ref, hc_ref, q_ref, k_ref, v_ref, *, tm,
                  n_ctx, bps):
    x = x_ref[...]
    mod = lambda idx: _mod_row(mb_ref, mc_ref, idx, tm, n_ctx, bps)
    h = _modulated_norm(x, g_ref[...], mod(3), mod(4)).astype(bf16)
    fr_ref[...] = _dot(h, w_ref[:, 0:RWKV_IN])
    o = RWKV_IN
    val = _dot(h, w_ref[:, o:o + D_CONV])
    gate = _dot(h, w_ref[:, o + D_CONV:o + CONV_IN])
    hc_ref[...] = val * _sigmoid(gate)
    o = RWKV_IN + CONV_IN
    cos = cos_ref[...]
    sin = sin_ref[...]
    lane = lax.broadcasted_iota(jnp.int32, cos.shape, 1)
    low_half = ((lane // ROPE_FREQS) % 2) == 0

    def rope(t):
        partner = jnp.where(low_half, pltpu.roll(t, D_QK - ROPE_FREQS, 1), pltpu.roll(t, ROPE_FREQS, 1))
        return t * cos + partner * sin

    q = _dot(h, w_ref[:, o:o + D_QK])
    q_ref[...] = (rope(q) * (DIFF_QK_DIM ** -0.5 * LOG2_E)).astype(bf16)
    k = _dot(h, w_ref[:, o + D_QK:o + 2 * D_QK])
    k_ref[...] = rope(k).astype(bf16)
    v_ref[...] = _dot(h, w_ref[:, o + 2 * D_QK:o + 2 * D_QK + D_DIFF]).astype(bf16)


def _mixin(t, mods, g, w, cos, sin, seq, n_ctx, ctx_row):
    m, d = t.shape
    tm = _dense_tile(seq)
    bps = seq // tm
    const = lambda i: (0, 0)
    row = lambda i: (i, 0)
    return pl.pallas_call(
        functools.partial(_mixin_kernel, tm=tm, n_ctx=n_ctx, bps=bps),
        grid=(m // tm,),
        in_specs=[pl.BlockSpec((tm, d), row),
                  pl.BlockSpec((1, N_MOD, d), lambda i: (i // bps, 0, 0)),
                  pl.BlockSpec((1, N_MOD, d), lambda i: (ctx_row, 0, 0)),
                  pl.BlockSpec((1, d), const),
                  pl.BlockSpec((d, P_IN), const, pipeline_mode=pl.Buffered(1)),
                  pl.BlockSpec((tm, D_QK), lambda i: (i % bps, 0)),
                  pl.BlockSpec((tm, D_QK), lambda i: (i % bps, 0))],
        out_specs=[pl.BlockSpec((tm, RWKV_IN), row), pl.BlockSpec((tm, D_CONV), row),
                   pl.BlockSpec((tm, D_QK), row), pl.BlockSpec((tm, D_QK), row), pl.BlockSpec((tm, D_DIFF), row)],
        out_shape=[jax.ShapeDtypeStruct((m, RWKV_IN), f32), jax.ShapeDtypeStruct((m, D_CONV), f32),
                   jax.ShapeDtypeStruct((m, D_QK), bf16), jax.ShapeDtypeStruct((m, D_QK), bf16),
                   jax.ShapeDtypeStruct((m, D_DIFF), bf16)],
        compiler_params=_cparams(1),
        name="mixin",
    )(t, mods, mods, g.reshape(1, d), w, cos, sin)


def _attn_kernel(q_ref, k_ref, v_ref, lam_ref, g_ref, o_ref, *, lam_init, n_ctx, n_lat, tq, unroll, with_ctx):
    lv = lam_ref[...]
    lam = (jnp.exp(jnp.sum(lv[0:1] * lv[1:2], keepdims=True)) - jnp.exp(jnp.sum(lv[2:3] * lv[3:4], keepdims=True))
           + lam_init)

    def attend(q, k, v):
        es, ls = [], []
        for m in range(2):
            sl = slice(m * DIFF_QK_DIM, (m + 1) * DIFF_QK_DIM)
            s = _dot_nt(q[:, sl], k[:, sl])
            e = jnp.exp2(s - jnp.max(s, axis=-1, keepdims=True))
            es.append(e)
            ls.append(jnp.sum(e, axis=-1, keepdims=True))
        attn = es[0] - (lam * ls[0] / ls[1]) * es[1]
        o = _dot(attn.astype(bf16), v) * (1.0 / ls[0])
        o = o * lax.rsqrt(jnp.mean(o * o, axis=-1, keepdims=True) + 1e-5) * g_ref[...]
        return o * (1.0 - lam_init)

    out_off = 0
    if with_ctx:
        o_ref[0:n_ctx, :] = attend(q_ref[0:n_ctx, :], k_ref[0:n_ctx, :], v_ref[0:n_ctx, :])
        out_off = n_ctx
    align = math.gcd(n_ctx, tq)

    def body(i, carry):
        for u in range(unroll):
            t = i * unroll + u
            q_rows = pl.ds(pl.multiple_of(n_ctx + t * tq, align), tq)
            o_rows = pl.ds(pl.multiple_of(out_off + t * tq, align), tq)
            o_ref[o_rows, :] = attend(q_ref[q_rows, :], k_ref[...], v_ref[...])
        return carry

    lax.fori_loop(0, n_lat // (tq * unroll), body, 0)


def _attn(q, k, v, lam_vecs, norm_g, lam_init, nbatch, n_ctx, n_lat, with_ctx):
    seq = n_ctx + n_lat
    tq = TM
    unroll = 4 if n_lat % (4 * tq) == 0 else 1
    out_seq = seq if with_ctx else n_lat
    blk = lambda b, h: (b, h)
    return pl.pallas_call(
        functools.partial(_attn_kernel, lam_init=lam_init, n_ctx=n_ctx, n_lat=n_lat, tq=tq, unroll=unroll,
                          with_ctx=with_ctx),
        grid=(nbatch, DIFF_HEADS),
        in_specs=[pl.BlockSpec((seq, DIFF_V_DIM), blk), pl.BlockSpec((seq, DIFF_V_DIM), blk),
                  pl.BlockSpec((seq, DIFF_V_DIM), blk),
                  pl.BlockSpec((4, DIFF_QK_DIM), lambda b, h: (0, 0)),
                  pl.BlockSpec((1, DIFF_V_DIM), lambda b, h: (0, 0))],
        out_specs=pl.BlockSpec((out_seq, DIFF_V_DIM), blk),
        out_shape=jax.ShapeDtypeStruct((nbatch * out_seq, D_DIFF), f32),
        compiler_params=_cparams(2),
        name="diffattn",
    )(q, k, v, lam_vecs, norm_g.reshape(1, DIFF_V_DIM))


def _segment_flags(i, nbc, nbs):
    j = i % nbs
    first = jnp.logical_or(j == 0, j == nbc)
    last = jnp.logical_or(j == nbc - 1, j == nbs - 1)
    return first, last


def _halo_specs(width, n_rows):
    per = TM // HALO
    nblk = n_rows // HALO
    prev = pl.BlockSpec((HALO, width), lambda i: (jnp.maximum(i * per - 1, 0), 0))
    nxt = pl.BlockSpec((HALO, width), lambda i: (jnp.minimum((i + 1) * per, nblk - 1), 0))
    return prev, nxt


def _conv_kernel(h_ref, hp_ref, hn_ref, w_ref, b_ref, g_ref, bb_ref, o_ref, buf, *, nbc, nbs):
    first, last = _segment_flags(pl.program_id(0), nbc, nbs)
    buf[0:HALO, :] = jnp.where(first, 0.0, hp_ref[...])
    buf[HALO:HALO + TM, :] = h_ref[...]
    buf[HALO + TM:HALO + TM + HALO, :] = jnp.where(last, 0.0, hn_ref[...])
    sub = 64
    for r0 in range(0, TM, sub):
        acc = jnp.zeros((sub, D_CONV), f32) + b_ref[...]
        for kk in range(CONV_WIDTH):
            start = r0 + HALO - CONV_HALF + kk
            acc = acc + buf[start:start + sub, :] * w_ref[kk:kk + 1, :]
        mu = jnp.mean(acc, axis=-1, keepdims=True)
        xc = acc - mu
        var = jnp.mean(xc * xc, axis=-1, keepdims=True)
        y = xc * lax.rsqrt(var + 1e-5) * g_ref[...] + bb_ref[...]
        o_ref[r0:r0 + sub, :] = y * _sigmoid(y)


def _conv(hc, dw_w, dw_b, ln_g, ln_b, nbc, nbs):
    m = hc.shape[0]
    prev, nxt = _halo_specs(D_CONV, m)
    const = lambda i: (0, 0)
    return pl.pallas_call(
        functools.partial(_conv_kernel, nbc=nbc, nbs=nbs),
        grid=(m // TM,),
        in_specs=[pl.BlockSpec((TM, D_CONV), lambda i: (i, 0)), prev, nxt,
                  pl.BlockSpec((CONV_WIDTH, D_CONV), const), pl.BlockSpec((1, D_CONV), const),
                  pl.BlockSpec((1, D_CONV), const), pl.BlockSpec((1, D_CONV), const)],
        out_specs=pl.BlockSpec((TM, D_CONV), lambda i: (i, 0)),
        out_shape=jax.ShapeDtypeStruct((m, D_CONV), f32),
        scratch_shapes=[pltpu.VMEM((TM + 2 * HALO, D_CONV), f32)],
        compiler_params=_cparams(1),
        name="convmod",
    )(hc, hc, hc, dw_w, dw_b.reshape(1, D_CONV), ln_g.reshape(1, D_CONV), ln_b.reshape(1, D_CONV))


def _rwkv_chunk_kernel(f_ref, fp_ref, fn_ref, mu_ref, w0_ref, w2_ref, a0_ref, a2_ref, g2_ref, kk_ref, ka_ref, rk_ref,
                       rp_ref, y0_ref, mm_ref, nn_ref, bv_ref, gg_ref,
                       s_r, s_v, s_a, s_lw, s_k, s_b, *, nbc, nbs):
    first, last = _segment_flags(pl.program_id(0), nbc, nbs)
    f = f_ref[...]
    prow = jnp.where(first, 0.0, fp_ref[HALO - 1:HALO, :])
    nrow = jnp.where(last, 0.0, fn_ref[0:1, :])
    rows = lax.broadcasted_iota(jnp.int32, f.shape, 0)
    prev = jnp.where(rows == 0, prow, pltpu.roll(f, 1, 0))
    nxt = jnp.where(rows == TM - 1, nrow, pltpu.roll(f, TM - 1, 0))
    fs = f + mu_ref[0:1, :] * (prev - f) + mu_ref[1:2, :] * (nxt - f)

    r = fs[:, 0:D_RWKV]
    k = fs[:, D_RWKV:2 * D_RWKV]
    v = fs[:, 2 * D_RWKV:3 * D_RWKV]
    o_w = 3 * D_RWKV
    o_a = o_w + 2 * DECAY_LORA
    o_g = o_a + 2 * AAA_LORA
    ones_bd = _head_ones(D_RWKV, HEAD_DIM)
    gg_ref[...] = _dot(_sigmoid(fs[:, o_g:o_g + GATE_LORA]).astype(bf16), g2_ref[...].astype(bf16))
    kk = k * kk_ref[...]
    kk = kk * lax.rsqrt(_dot_exact_rhs(kk * kk, ones_bd) + NORM_EPS)
    ksum = None
    for d in range(2):
        wd = fs[:, o_w + d * DECAY_LORA:o_w + (d + 1) * DECAY_LORA]
        ad = fs[:, o_a + d * AAA_LORA:o_a + (d + 1) * AAA_LORA]
        w_raw = w0_ref[d:d + 1, :] + _dot(jnp.tanh(wd).astype(bf16), w2_ref[d].astype(bf16))
        z = -w_raw
        softplus = jnp.maximum(z, 0.0) + jnp.log(1.0 + jnp.exp(-jnp.abs(z)))
        s_lw[d] = -jnp.exp(-softplus - 0.5)
        a = _sigmoid(a0_ref[d:d + 1, :] + _dot(ad.astype(bf16), a2_ref[d].astype(bf16)))
        kd = k * (1.0 + (a - 1.0) * ka_ref[...])
        ksum = kd if d == 0 else ksum + kd
        s_k[d] = kd
        s_b[d] = kk * a
    s_r[...] = r
    s_v[...] = v
    s_a[...] = -kk
    bv_ref[...] = _dot_exact_rhs(r * ksum * rk_ref[...], ones_bd) * v

    ri = lax.broadcasted_iota(jnp.int32, (CHUNK, CHUNK), 0)
    ci = lax.broadcasted_iota(jnp.int32, (CHUNK, CHUNK), 1)
    tri = tuple(jnp.where(msk, 1.0, 0.0).astype(bf16) for msk in (ci <= ri, ci >= ri))
    rp_i = lax.broadcasted_iota(jnp.int32, (CHUNK, PAIR), 0)
    cp_i = lax.broadcasted_iota(jnp.int32, (CHUNK, PAIR), 1) % CHUNK
    eye = jnp.where(rp_i == cp_i, 1.0, 0.0).astype(f32)
    strict = (cp_i < rp_i, cp_i > rp_i)
    incl = (cp_i <= rp_i, cp_i >= rp_i)
    low = lax.broadcasted_iota(jnp.int32, (CHUNK, PAIR), 1) < HEAD_DIM

    def chunk_body(c, carry):
        rs = pl.ds(pl.multiple_of(c * CHUNK, CHUNK), CHUNK)
        r_c = s_r[rs, :]
        v_c = s_v[rs, :]
        a_c = s_a[rs, :]
        v16_c = v_c.astype(bf16)
        at16, rt, bt16, kt16, wc = [], [], [], [], []
        for d in range(2):
            lw = s_lw[d, rs, :]
            cum = _dot_exact_lhs(tri[d], lw)
            er = jnp.exp(cum)
            ei = jnp.exp(-cum)
            at16.append((a_c * jnp.exp(cum - lw)).astype(bf16))
            rt.append(r_c * er)
            bt16.append((s_b[d, rs, :] * ei).astype(bf16))
            kt16.append((s_k[d, rs, :] * ei).astype(bf16))
            wc.append(er[CHUNK - 1:CHUNK, :] if d == 0 else er[0:1, :])

        insts = [(d, p) for d in range(2) for p in range(RWKV_HEADS // 2)]
        psl = lambda p: slice(p * PAIR, (p + 1) * PAIR)
        gms = [_dot_nt(jnp.concatenate([at16[d][:, psl(p)], rt[d][:, psl(p)].astype(bf16)], axis=0),
                       jnp.concatenate([_pair_blockdiag(bt16[d][:, psl(p)]), _pair_blockdiag(kt16[d][:, psl(p)])],
                                       axis=0)) for d, p in insts]
        lab = [jnp.where(strict[d], g[:CHUNK, :PAIR], 0.0) for (d, p), g in zip(insts, gms)]
        lakrk16 = [jnp.concatenate([jnp.where(strict[d], g[:CHUNK, PAIR:], 0.0),
                                    jnp.where(incl[d], g[CHUNK:, PAIR:], 0.0)], axis=0).astype(bf16)
                   for (d, p), g in zip(insts, gms)]
        lrb16 = [jnp.where(incl[d], g[CHUNK:, :PAIR], 0.0).astype(bf16) for (d, p), g in zip(insts, gms)]
        uv = [_dot(lk, _pair_blockdiag(v16_c[:, psl(p)])) for (d, p), lk in zip(insts, lakrk16)]
        vtk = [_dot_tn(v16_c[:, psl(p)], kt16[d][:, psl(p)]) for d, p in insts]
        q = lab
        z = [jnp.concatenate([at16[d][:, psl(p)].astype(f32), u[:CHUNK]], axis=1) for (d, p), u in zip(insts, uv)]
        n_stage = int(math.log2(CHUNK))
        zw = 2 * PAIR
        for stage in range(n_stage):
            z_hi = [zz.astype(bf16) for zz in z]
            z_lo = [(zz - zh.astype(f32)).astype(bf16) for zz, zh in zip(z, z_hi)]
            q_hi = [qq.astype(bf16) for qq in q]
            q_lo = [(qq - qh.astype(f32)).astype(bf16) for qq, qh in zip(q, q_hi)]
            if stage < n_stage - 1:
                pa = [_dot(qh, _pair_blockdiag(jnp.concatenate([zh, zl, qh, ql], axis=1)))
                      for qh, ql, zh, zl in zip(q_hi, q_lo, z_hi, z_lo)]
                pb = [_dot(ql, _pair_blockdiag(jnp.concatenate([zh, qh], axis=1)))
                      for qh, ql, zh in zip(q_hi, q_lo, z_hi)]
                q = [a[:, 2 * zw:2 * zw + PAIR] + a[:, 2 * zw + PAIR:] + b[:, zw:] for a, b in zip(pa, pb)]
            else:
                pa = [_dot(qh, _pair_blockdiag(jnp.concatenate([zh, zl], axis=1)))
                      for qh, zh, zl in zip(q_hi, z_hi, z_lo)]
                pb = [_dot(ql, _pair_blockdiag(zh)) for ql, zh in zip(q_lo, z_hi)]
            z = [zz + (a[:, :zw] + a[:, zw:2 * zw] + b[:, :zw]) for zz, a, b in zip(z, pa, pb)]
        au16 = [zz.astype(bf16) for zz in z]
        ry = [_dot(lr, _pair_blockdiag(au)) for lr, au in zip(lrb16, au16)]
        mn = [_dot_tn(au, bt16[d][:, psl(p)]) for (d, p), au in zip(insts, au16)]
        for i, (d, p) in enumerate(insts):
            ps = psl(p)
            rp_ref[d, rs, ps] = rt[d][:, ps] + ry[i][:, :PAIR]
            y0_ref[d, rs, ps] = ry[i][:, PAIR:] + uv[i][CHUNK:]
            wc_p = wc[d][:, ps]
            m_pair = jnp.where(low, mn[i][0:CHUNK], mn[i][CHUNK:2 * CHUNK])
            n_pair = jnp.where(low, mn[i][2 * CHUNK:3 * CHUNK], mn[i][3 * CHUNK:])
            k_pair = jnp.where(low, vtk[i][:CHUNK], vtk[i][CHUNK:])
            mm_ref[d, c, p] = (eye + m_pair) * wc_p
            nn_ref[d, c, p] = (n_pair + k_pair) * wc_p
        return carry

    lax.fori_loop(0, TM // CHUNK, chunk_body, 0)


def _rwkv_chunks(fr, mu, w0, w2, a0, a2, g2, kk, ka, rk, nbc, nbs):
    m = fr.shape[0]
    nct = TM // CHUNK
    prev, nxt = _halo_specs(RWKV_IN, m)
    c2 = lambda i: (0, 0)
    c3 = lambda i: (0, 0, 0)
    row = lambda i: (i, 0)
    dirrow = lambda i: (0, i, 0)
    ops = lambda i: (0, i, 0, 0, 0)
    return pl.pallas_call(
        functools.partial(_rwkv_chunk_kernel, nbc=nbc, nbs=nbs),
        grid=(m // TM,),
        in_specs=[pl.BlockSpec((TM, RWKV_IN), row), prev, nxt,
                  pl.BlockSpec((2, RWKV_IN), c2), pl.BlockSpec((2, D_RWKV), c2),
                  pl.BlockSpec((2, DECAY_LORA, D_RWKV), c3), pl.BlockSpec((2, D_RWKV), c2),
                  pl.BlockSpec((2, AAA_LORA, D_RWKV), c3), pl.BlockSpec((GATE_LORA, D_RWKV), c2),
                  pl.BlockSpec((1, D_RWKV), c2), pl.BlockSpec((1, D_RWKV), c2), pl.BlockSpec((1, D_RWKV), c2)],
        out_specs=[pl.BlockSpec((2, TM, D_RWKV), dirrow), pl.BlockSpec((2, TM, D_RWKV), dirrow),
                   pl.BlockSpec((2, nct, N_PAIR, HEAD_DIM, PAIR), ops),
                   pl.BlockSpec((2, nct, N_PAIR, HEAD_DIM, PAIR), ops),
                   pl.BlockSpec((TM, D_RWKV), row), pl.BlockSpec((TM, D_RWKV), row)],
        out_shape=[jax.ShapeDtypeStruct((2, m, D_RWKV), f32), jax.ShapeDtypeStruct((2, m, D_RWKV), f32),
                   jax.ShapeDtypeStruct((2, m // CHUNK, N_PAIR, HEAD_DIM, PAIR), f32),
                   jax.ShapeDtypeStruct((2, m // CHUNK, N_PAIR, HEAD_DIM, PAIR), f32),
                   jax.ShapeDtypeStruct((m, D_RWKV), f32), jax.ShapeDtypeStruct((m, D_RWKV), f32)],
        scratch_shapes=[pltpu.VMEM((TM, D_RWKV), f32), pltpu.VMEM((TM, D_RWKV), f32), pltpu.VMEM((TM, D_RWKV), f32),
                        pltpu.VMEM((2, TM, D_RWKV), f32), pltpu.VMEM((2, TM, D_RWKV), f32),
                        pltpu.VMEM((2, TM, D_RWKV), f32)],
        compiler_params=_cparams(1),
        name="rwkv_chunks",
    )(fr, fr, fr, mu, w0, w2, a0, a2, g2, kk.reshape(1, D_RWKV), ka.reshape(1, D_RWKV), rk.reshape(1, D_RWKV))


def _rwkv_state_kernel(mf_ref, mb_ref, nf_ref, nb_ref, sf_ref, sb_ref, st_ref, *, nbatch):
    @pl.when(pl.program_id(0) == 0)
    def _():
        st_ref[...] = jnp.zeros(st_ref.shape, f32)

    group = 8
    probs = [(d, b, p) for b in range(nbatch) for d in range(2) for p in range(N_PAIR)]
    for g0 in range(0, len(probs), group):
        grp = probs[g0:g0 + group]
        st = [st_ref[d, b, p] for d, b, p in grp]
        for (d, b, p), s in zip(grp, st):
            (sf_ref if d == 0 else sb_ref)[b, 0, p] = s
        mt = [(mf_ref if d == 0 else mb_ref)[0, b, 0, p] for d, b, p in grp]
        s1 = [s.astype(bf16) for s in st]
        s2 = [(s - a.astype(f32)).astype(bf16) for s, a in zip(st, s1)]
        m1 = [x.astype(bf16) for x in mt]
        m2 = [(x - a.astype(f32)).astype(bf16) for x, a in zip(mt, m1)]
        p12 = [_dot(a1, _pair_blockdiag(jnp.concatenate([b1, b2], axis=1))) for a1, b1, b2 in zip(s1, m1, m2)]
        p21 = [_dot(a2, _pair_blockdiag(b1)) for a2, b1 in zip(s2, m1)]
        for (d, b, p), x12, x21 in zip(grp, p12, p21):
            st_ref[d, b, p] = (x12[:, :PAIR] + x12[:, PAIR:] + x21) + (nf_ref if d == 0 else nb_ref)[0, b, 0, p]


def _rwkv_state(mm, nn, nbatch, nca, ncc):
    shp = (2, nbatch, nca, N_PAIR, HEAD_DIM, PAIR)
    blk = (1, nbatch, 1, N_PAIR, HEAD_DIM, PAIR)
    c_bwd = lambda s: jnp.where(s < ncc, ncc - 1 - s, nca - 1 - (s - ncc))
    fwd = pl.BlockSpec(blk, lambda s: (0, 0, s, 0, 0, 0))
    bwd = pl.BlockSpec(blk, lambda s: (1, 0, c_bwd(s), 0, 0, 0))
    oshape = jax.ShapeDtypeStruct(shp[1:], f32)
    return pl.pallas_call(
        functools.partial(_rwkv_state_kernel, nbatch=nbatch),
        grid=(nca,),
        in_specs=[fwd, bwd, fwd, bwd],
        out_specs=[pl.BlockSpec(blk[1:], lambda s: (0, s, 0, 0, 0)),
                   pl.BlockSpec(blk[1:], lambda s: (0, c_bwd(s), 0, 0, 0))],
        out_shape=[oshape, oshape],
        scratch_shapes=[pltpu.VMEM((2, nbatch, N_PAIR, HEAD_DIM, PAIR), f32)],
        compiler_params=_cparams(1),
        name="rwkv_state",
    )(mm.reshape(shp), mm.reshape(shp), nn.reshape(shp), nn.reshape(shp))


def _rwkv_out_kernel(rp_ref, y0_ref, sf_ref, sb_ref, bv_ref, gg_ref, lng_ref, lnb_ref, o_ref):
    rows = []
    for c in range(TM // CHUNK):
        rs = slice(c * CHUNK, (c + 1) * CHUNK)
        cols = []
        for p in range(N_PAIR):
            ps = slice(p * PAIR, (p + 1) * PAIR)
            lhs = jnp.concatenate([rp_ref[0, rs, ps], rp_ref[1, rs, ps]], axis=1).astype(bf16)
            rhs = jnp.concatenate([_pair_blockdiag(sf_ref[c, p].astype(bf16)),
                                   _pair_blockdiag(sb_ref[c, p].astype(bf16))], axis=1)
            cols.append(y0_ref[0, rs, ps] + y0_ref[1, rs, ps] + _dot_nt(lhs, rhs))
        rows.append(jnp.concatenate(cols, axis=1))
    y = jnp.concatenate(rows, axis=0)
    ones_bd = _head_ones(D_RWKV, HEAD_DIM)
    mu = _dot_exact_rhs(y, ones_bd) * (1.0 / HEAD_DIM)
    yc = y - mu
    var = _dot_exact_rhs(yc * yc, ones_bd) * (1.0 / HEAD_DIM)
    yn = yc * lax.rsqrt(var + GN_EPS) * lng_ref[...] + lnb_ref[...]
    o_ref[...] = (yn + bv_ref[...]) * gg_ref[...]


def _rwkv_out(rp, y0, sf, sb, bv, gg, ln_g, ln_b):
    m = bv.shape[0]
    nct = TM // CHUNK
    c2 = lambda i: (0, 0)
    row = lambda i: (i, 0)
    dirrow = lambda i: (0, i, 0)
    st = lambda i: (i, 0, 0, 0)
    sshape = (m // CHUNK, N_PAIR, HEAD_DIM, PAIR)
    return pl.pallas_call(
        _rwkv_out_kernel,
        grid=(m // TM,),
        in_specs=[pl.BlockSpec((2, TM, D_RWKV), dirrow), pl.BlockSpec((2, TM, D_RWKV), dirrow),
                  pl.BlockSpec((nct, N_PAIR, HEAD_DIM, PAIR), st),
                  pl.BlockSpec((nct, N_PAIR, HEAD_DIM, PAIR), st),
                  pl.BlockSpec((TM, D_RWKV), row), pl.BlockSpec((TM, D_RWKV), row),
                  pl.BlockSpec((1, D_RWKV), c2), pl.BlockSpec((1, D_RWKV), c2)],
        out_specs=pl.BlockSpec((TM, D_RWKV), row),
        out_shape=jax.ShapeDtypeStruct((m, D_RWKV), f32),
        compiler_params=_cparams(1),
        name="rwkv_out",
    )(rp, y0, sf.reshape(sshape), sb.reshape(sshape), bv, gg, ln_g.reshape(1, D_RWKV), ln_b.reshape(1, D_RWKV))


def _mixout_kernel(t_ref, a_ref, b_ref, c_ref, w_ref, m_ref, o_ref):
    out = _dot(a_ref[...].astype(bf16), w_ref[0:D_RWKV, :])
    out = out + _dot(b_ref[...].astype(bf16), w_ref[D_RWKV:D_RWKV + D_CONV, :])
    out = out + _dot(c_ref[...].astype(bf16), w_ref[D_RWKV + D_CONV:, :])
    o_ref[...] = t_ref[...] + m_ref[0, 5:6, :] * out


def _mixout(t, oa, ob, oc, w, mods, grid, row_map, mod_map, c_map, out_rows):
    d = t.shape[1]
    n_axes = len(grid)
    if n_axes == 1:
        out_map = lambda i: (i, 0)
    else:
        nj = grid[1]
        out_map = lambda b, j: (b * nj + j, 0)
    rmap = lambda *a: (row_map(*a), 0)
    return pl.pallas_call(
        _mixout_kernel,
        grid=grid,
        in_specs=[pl.BlockSpec((TM, d), rmap), pl.BlockSpec((TM, D_RWKV), rmap), pl.BlockSpec((TM, D_CONV), rmap),
                  pl.BlockSpec((TM, D_DIFF), lambda *a: (c_map(*a), 0)),
                  pl.BlockSpec((D_RWKV + D_CONV + D_DIFF, d), lambda *a: (0, 0)),
                  pl.BlockSpec((1, N_MOD, d), lambda *a: (mod_map(*a), 0, 0))],
        out_specs=pl.BlockSpec((TM, d), out_map),
        out_shape=jax.ShapeDtypeStruct((out_rows, d), f32),
        compiler_params=_cparams(n_axes),
        name="mixout",
    )(t, oa, ob, oc, w, mods)


def _rope_tables(n_ctx, n_lat):
    n_rows = n_lat // GRID_W
    row = jnp.repeat(jnp.arange(n_rows, dtype=jnp.int32), GRID_W)
    col = jnp.tile(jnp.arange(GRID_W, dtype=jnp.int32), n_rows)
    inv = 1.0 / (ROPE_THETA ** (jnp.arange(ROPE_FREQS, dtype=f32) * 2.0 / AXIS_DIM))
    ang = jnp.stack([row, col], axis=-1).astype(f32)[..., None] * inv
    cos, sin = jnp.cos(ang), jnp.sin(ang)
    cos64 = jnp.concatenate([cos[:, 0], cos[:, 0], cos[:, 1], cos[:, 1]], axis=-1)
    sin64 = jnp.concatenate([-sin[:, 0], sin[:, 0], -sin[:, 1], sin[:, 1]], axis=-1)
    reps = D_QK // DIFF_QK_DIM
    cos_t = jnp.concatenate([jnp.ones((n_ctx, D_QK), f32), jnp.tile(cos64, (1, reps))], axis=0)
    sin_t = jnp.concatenate([jnp.zeros((n_ctx, D_QK), f32), jnp.tile(sin64, (1, reps))], axis=0)
    return cos_t, sin_t


def kernel(x, c, ctx, c_ctx, ada_w, ada_b, norm_g, ffn_w_in, ffn_w_out, mix_w_in, mix_w_out, rwkv_mu, rwkv_w0, rwkv_w2,
           rwkv_a0, rwkv_a2, rwkv_g2, rwkv_kk, rwkv_ka, rwkv_rk, rwkv_ln_g, rwkv_ln_b, conv_dw_w, conv_dw_b, conv_ln_g,
           conv_ln_b, diff_lam, diff_norm_g, final_g):
    nb, n_lat, d = x.shape
    n_ctx = ctx.shape[1]
    depth = ada_w.shape[0]
    seq = n_ctx + n_lat
    assert n_ctx % TM == 0 and n_lat % TM == 0 and n_lat % GRID_W == 0 and seq % n_ctx == 0
    assert nb + 1 <= COND_ROWS
    nbc, nbx, nbs = n_ctx // TM, n_lat // TM, seq // TM
    m = nb * seq

    t = jnp.concatenate([ctx, x], axis=1).reshape(m, d)
    cond = jnp.zeros((COND_ROWS, d), f32).at[:nb].set(c).at[nb].set(c_ctx)
    mods = _adaln(cond, ada_w, ada_b).reshape(depth, COND_ROWS, N_MOD, d)
    cos_t, sin_t = _rope_tables(n_ctx, n_lat)
    w_in16 = ffn_w_in.astype(bf16)
    w_out16 = ffn_w_out.astype(bf16)
    mix_in16 = mix_w_in.astype(bf16)
    mix_out16 = mix_w_out.astype(bf16)

    mod_all = lambda i: jnp.where(i % nbs < nbc, nb, i // nbs)
    row_all = lambda i: i
    row_lat = lambda b, j: b * nbs + nbc + j
    mod_lat = lambda b, j: b

    out = None
    for l in range(depth):
        last = l == depth - 1
        ml = mods[l]
        t = _ffn(t, ml, norm_g[l, 0], w_in16[l, 0], w_out16[l, 0], 0, seq, n_ctx, nb)
        fr, hc, q, k, v = _mixin(t, ml, norm_g[l, 1], mix_in16[l], cos_t, sin_t, seq, n_ctx, nb)
        rp, y0, mm, nn, bv, gg = _rwkv_chunks(fr, rwkv_mu[l], rwkv_w0[l], rwkv_w2[l], rwkv_a0[l], rwkv_a2[l], rwkv_g2[l],
                                              rwkv_kk[l], rwkv_ka[l], rwkv_rk[l].reshape(D_RWKV), nbc, nbs)
        sf, sb = _rwkv_state(mm, nn, nb, seq // CHUNK, n_ctx // CHUNK)
        oa = _rwkv_out(rp, y0, sf, sb, bv, gg, rwkv_ln_g[l], rwkv_ln_b[l])
        ob = _conv(hc, conv_dw_w[l], conv_dw_b[l], conv_ln_g[l], conv_ln_b[l], nbc, nbs)
        lam_init = 0.8 - 0.6 * math.exp(-0.3 * l)
        if last:
            oc = _attn(q, k, v, diff_lam[l], diff_norm_g[l], lam_init, nb, n_ctx, n_lat, False)
            t = _mixout(t, oa, ob, oc, mix_out16[l], ml, (nb, nbx), row_lat, mod_lat, lambda b, j: b * nbx + j,
                        nb * n_lat)
            out = _ffn(t, ml, norm_g[l, 2], w_in16[l, 1], w_out16[l, 1], 6, n_lat, 0, nb, final_g=final_g)
        else:
            oc = _attn(q, k, v, diff_lam[l], diff_norm_g[l], lam_init, nb, n_ctx, n_lat, True)
            t = _mixout(t, oa, ob, oc, mix_out16[l], ml, (nbs * nb,), row_all, mod_all, row_all, m)
            t = _ffn(t, ml, norm_g[l, 2], w_in16[l, 1], w_out16[l, 1], 6, seq, n_ctx, nb)
    return out.reshape(nb, n_lat, d)
```

```python
import functools
import math

import jax
import jax.numpy as jnp
from jax import lax
from jax.experimental import pallas as pl
from jax.experimental.pallas import tpu as pltpu

f32 = jnp.float32
bf16 = jnp.bfloat16

N_MOD = 9
EPS = 1e-6
GRID_W = 64
RWKV_HEADS = 4
HEAD_DIM = 64
D_RWKV = RWKV_HEADS * HEAD_DIM
PAIR = 2 * HEAD_DIM
N_PAIR = RWKV_HEADS // 2
DECAY_LORA = 64
AAA_LORA = 64
GATE_LORA = 128
GN_EPS = 64e-5
NORM_EPS = 1e-12
D_CONV = 256
CONV_WIDTH = 31
CONV_HALF = CONV_WIDTH // 2
DIFF_HEADS = 4
DIFF_QK_DIM = 64
DIFF_V_DIM = 2 * DIFF_QK_DIM
D_DIFF = DIFF_HEADS * DIFF_V_DIM
ROPE_THETA = 10000.0
LOG2_E = 1.4426950408889634
AXIS_DIM = DIFF_QK_DIM // 2
ROPE_FREQS = AXIS_DIM // 2
RWKV_IN = 3 * D_RWKV + 2 * DECAY_LORA + 2 * AAA_LORA + GATE_LORA
CONV_IN = 2 * D_CONV
D_QK = DIFF_HEADS * 2 * DIFF_QK_DIM
DIFF_IN = 2 * D_QK + D_DIFF
P_IN = RWKV_IN + CONV_IN + DIFF_IN

TM = 256
CHUNK = 64
HALO = 16
COND_ROWS = 16
VMEM_LIMIT = 56 * 1024 * 1024


def _dot(a, b, prec=None):
    return jnp.dot(a, b, preferred_element_type=f32, precision=prec)


def _dot_nt(a, b, prec=None):
    return lax.dot_general(a, b, (((1,), (1,)), ((), ())), preferred_element_type=f32, precision=prec)


def _dot_tn(a, b, prec=None):
    return lax.dot_general(a, b, (((0,), (0,)), ((), ())), preferred_element_type=f32, precision=prec)


def _split3(a):
    a1 = a.astype(bf16)
    r1 = a - a1.astype(f32)
    a2 = r1.astype(bf16)
    a3 = (r1 - a2.astype(f32)).astype(bf16)
    return a1, a2, a3


def _dot_exact_rhs(a, b16):
    a1, a2, a3 = _split3(a)
    return _dot(a1, b16) + _dot(a2, b16) + _dot(a3, b16)


def _dot_exact_lhs(a16, b):
    b1, b2, b3 = _split3(b)
    return _dot(a16, b1) + _dot(a16, b2) + _dot(a16, b3)


def _pair_blockdiag(x):
    first = (lax.broadcasted_iota(jnp.int32, x.shape, 1) % PAIR) < HEAD_DIM
    zero = jnp.zeros_like(x)
    return jnp.concatenate([jnp.where(first, x, zero), jnp.where(first, zero, x)], axis=0)


def _sigmoid(x):
    return 1.0 / (1.0 + jnp.exp(-x))


def _cparams(n_axes):
    return pltpu.CompilerParams(dimension_semantics=("arbitrary",) * n_axes, vmem_limit_bytes=VMEM_LIMIT)


def _head_ones(n, width):
    r = lax.broadcasted_iota(jnp.int32, (n, n), 0) // width
    c = lax.broadcasted_iota(jnp.int32, (n, n), 1) // width
    return jnp.where(r == c, 1.0, 0.0).astype(bf16)


def _adaln_kernel(c_ref, w_ref, b_ref, o_ref):
    cond = c_ref[...]
    cond = cond * _sigmoid(cond)
    o_ref[0] = _dot(cond.astype(bf16), w_ref[0].astype(bf16)) + b_ref[0]


def _adaln(cond, ada_w, ada_b):
    depth, d, nd = ada_w.shape
    tn = nd // 4
    return pl.pallas_call(
        _adaln_kernel,
        grid=(depth, nd // tn),
        in_specs=[pl.BlockSpec((COND_ROWS, d), lambda l, j: (0, 0)),
                  pl.BlockSpec((1, d, tn), lambda l, j: (l, 0, j)),
                  pl.BlockSpec((1, 1, tn), lambda l, j: (l, 0, j))],
        out_specs=pl.BlockSpec((1, COND_ROWS, tn), lambda l, j: (l, 0, j)),
        out_shape=jax.ShapeDtypeStruct((depth, COND_ROWS, nd), f32),
        compiler_params=_cparams(2),
        name="adaln",
    )(cond, ada_w, ada_b.reshape(depth, 1, nd))


def _modulated_norm(x, g, shift, scale):
    y = x * lax.rsqrt(jnp.mean(x * x, axis=-1, keepdims=True) + EPS) * g
    return y * (1.0 + scale) + shift


def _dense_tile(rows_per_seq):
    for tm in (1024, 768, 512, 256):
        if rows_per_seq % tm == 0:
            return tm
    raise ValueError(rows_per_seq)


def _mod_row(mb_ref, mc_ref, idx, tm, n_ctx, blocks_per_seq):
    if n_ctx == 0:
        return mb_ref[0, idx:idx + 1, :]
    row = (pl.program_id(0) % blocks_per_seq) * tm + lax.broadcasted_iota(jnp.int32, (tm, 1), 0)
    return jnp.where(row < n_ctx, mc_ref[0, idx:idx + 1, :], mb_ref[0, idx:idx + 1, :])


def _ffn_kernel(x_ref, mb_ref, mc_ref, g_ref, wi_ref, wo_ref, *rest, s0, ff, fc, final, tm, n_ctx, bps):
    o_ref = rest[-1]
    x = x_ref[...]
    mod = lambda idx: _mod_row(mb_ref, mc_ref, idx, tm, n_ctx, bps)
    h = _modulated_norm(x, g_ref[...], mod(s0), mod(s0 + 1)).astype(bf16)
    acc = jnp.zeros(x.shape, f32)
    for j in range(ff // fc):
        gt = _dot(h, wi_ref[:, j * fc:(j + 1) * fc])
        up = _dot(h, wi_ref[:, ff + j * fc:ff + (j + 1) * fc])
        act = (gt * _sigmoid(gt) * up).astype(bf16)
        acc = acc + _dot(act, wo_ref[j * fc:(j + 1) * fc, :])
    out = x + 0.5 * mod(s0 + 2) * acc
    if final:
        fg_ref = rest[0]
        out = out * lax.rsqrt(jnp.mean(out * out, axis=-1, keepdims=True) + EPS) * fg_ref[...]
    o_ref[...] = out


def _ffn(t, mods, g, w_in, w_out, s0, rows_per_seq, n_ctx, ctx_row, final_g=None):
    m, d = t.shape
    ff = w_out.shape[0]
    fc = 256 if ff % 256 == 0 else ff
    tm = _dense_tile(rows_per_seq)
    bps = rows_per_seq // tm
    const = lambda i: (0, 0)
    in_specs = [pl.BlockSpec((tm, d), lambda i: (i, 0)),
                pl.BlockSpec((1, N_MOD, d), lambda i: (i // bps, 0, 0)),
                pl.BlockSpec((1, N_MOD, d), lambda i: (ctx_row, 0, 0)),
                pl.BlockSpec((1, d), const),
                pl.BlockSpec((d, 2 * ff), const, pipeline_mode=pl.Buffered(1)),
                pl.BlockSpec((ff, d), const, pipeline_mode=pl.Buffered(1))]
    args = [t, mods, mods, g.reshape(1, d), w_in, w_out]
    if final_g is not None:
        in_specs.append(pl.BlockSpec((1, d), const))
        args.append(final_g.reshape(1, d))
    return pl.pallas_call(
        functools.partial(_ffn_kernel, s0=s0, ff=ff, fc=fc, final=final_g is not None, tm=tm, n_ctx=n_ctx, bps=bps),
        grid=(m // tm,), in_specs=in_specs,
        out_specs=pl.BlockSpec((tm, d), lambda i: (i, 0)),
        out_shape=jax.ShapeDtypeStruct((m, d), f32),
        compiler_params=_cparams(1),
        name="ffn",
    )(*args)


def _mixin_kernel(x_ref, mb_ref, mc_ref, g_ref, w_ref, cos_ref, sin_ref, fr_ref, hc_ref, q_ref, k_ref, v_ref, *, tm,
                  n_ctx, bps):
    x = x_ref[...]
    mod = lambda idx: _mod_row(mb_ref, mc_ref, idx, tm, n_ctx, bps)
    h = _modulated_norm(x, g_ref[...], mod(3), mod(4)).astype(bf16)
    fr_ref[...] = _dot(h, w_ref[:, 0:RWKV_IN])
    o = RWKV_IN
    val = _dot(h, w_ref[:, o:o + D_CONV])
    gate = _dot(h, w_ref[:, o + D_CONV:o + CONV_IN])
    hc_ref[...] = val * _sigmoid(gate)
    o = RWKV_IN + CONV_IN
    cos = cos_ref[...]
    sin = sin_ref[...]
    lane = lax.broadcasted_iota(jnp.int32, cos.shape, 1)
    low_half = ((lane // ROPE_FREQS) % 2) == 0

    def rope(t):
        partner = jnp.where(low_half, pltpu.roll(t, D_QK - ROPE_FREQS, 1), pltpu.roll(t, ROPE_FREQS, 1))
        return t * cos + partner * sin

    q = _dot(h, w_ref[:, o:o + D_QK])
    q_ref[...] = (rope(q) * (DIFF_QK_DIM ** -0.5 * LOG2_E)).astype(bf16)
    k = _dot(h, w_ref[:, o + D_QK:o + 2 * D_QK])
    k_ref[...] = rope(k).astype(bf16)
    v_ref[...] = _dot(h, w_ref[:, o + 2 * D_QK:o + 2 * D_QK + D_DIFF]).astype(bf16)


def _mixin(t, mods, g, w, cos, sin, seq, n_ctx, ctx_row):
    m, d = t.shape
    tm = _dense_tile(seq)
    bps = seq // tm
    const = lambda i: (0, 0)
    row = lambda i: (i, 0)
    return pl.pallas_call(
        functools.partial(_mixin_kernel, tm=tm, n_ctx=n_ctx, bps=bps),
        grid=(m // tm,),
        in_specs=[pl.BlockSpec((tm, d), row),
                  pl.BlockSpec((1, N_MOD, d), lambda i: (i // bps, 0, 0)),
                  pl.BlockSpec((1, N_MOD, d), lambda i: (ctx_row, 0, 0)),
                  pl.BlockSpec((1, d), const),
                  pl.BlockSpec((d, P_IN), const, pipeline_mode=pl.Buffered(1)),
                  pl.BlockSpec((tm, D_QK), lambda i: (i % bps, 0)),
                  pl.BlockSpec((tm, D_QK), lambda i: (i % bps, 0))],
        out_specs=[pl.BlockSpec((tm, RWKV_IN), row), pl.BlockSpec((tm, D_CONV), row),
                   pl.BlockSpec((tm, D_QK), row), pl.BlockSpec((tm, D_QK), row), pl.BlockSpec((tm, D_DIFF), row)],
        out_shape=[jax.ShapeDtypeStruct((m, RWKV_IN), f32), jax.ShapeDtypeStruct((m, D_CONV), f32),
                   jax.ShapeDtypeStruct((m, D_QK), bf16), jax.ShapeDtypeStruct((m, D_QK), bf16),
                   jax.ShapeDtypeStruct((m, D_DIFF), bf16)],
        compiler_params=_cparams(1),
        name="mixin",
    )(t, mods, mods, g.reshape(1, d), w, cos, sin)


def _attn_kernel(q_ref, k_ref, v_ref, lam_ref, g_ref, o_ref, vaug_ref, *, lam_init, n_ctx, n_lat, tq, unroll,
                 with_ctx):
    lv = lam_ref[...]
    lam = (jnp.exp(jnp.sum(lv[0:1] * lv[1:2], keepdims=True)) - jnp.exp(jnp.sum(lv[2:3] * lv[3:4], keepdims=True))
           + lam_init)

    @pl.when(jnp.logical_and(pl.program_id(0) == 0, pl.program_id(1) == 0))
    def _():
        vaug_ref[:, DIFF_V_DIM:] = jnp.ones((vaug_ref.shape[0], DIFF_V_DIM), bf16)

    vaug_ref[:, :DIFF_V_DIM] = v_ref[...]

    def attend(q, k, v):
        parts = []
        for m in range(2):
            sl = slice(m * DIFF_QK_DIM, (m + 1) * DIFF_QK_DIM)
            s = _dot_nt(q[:, sl], k[:, sl])
            e = jnp.exp2(s - jnp.max(s, axis=-1, keepdims=True))
            pv = _dot(e.astype(bf16), v)
            parts.append(pv[:, :DIFF_V_DIM] * (1.0 / pv[:, DIFF_V_DIM:]))
        o = parts[0] - lam * parts[1]
        o = o * lax.rsqrt(jnp.mean(o * o, axis=-1, keepdims=True) + 1e-5) * g_ref[...]
        return o * (1.0 - lam_init)

    out_off = 0
    if with_ctx:
        o_ref[0:n_ctx, :] = attend(q_ref[0:n_ctx, :], k_ref[0:n_ctx, :], vaug_ref[0:n_ctx, :])
        out_off = n_ctx
    align = math.gcd(n_ctx, tq)

    def body(i, carry):
        for u in range(unroll):
            t = i * unroll + u
            q_rows = pl.ds(pl.multiple_of(n_ctx + t * tq, align), tq)
            o_rows = pl.ds(pl.multiple_of(out_off + t * tq, align), tq)
            o_ref[o_rows, :] = attend(q_ref[q_rows, :], k_ref[...], vaug_ref[...])
        return carry

    lax.fori_loop(0, n_lat // (tq * unroll), body, 0)


def _attn(q, k, v, lam_vecs, norm_g, lam_init, nbatch, n_ctx, n_lat, with_ctx):
    seq = n_ctx + n_lat
    tq = TM
    unroll = 4 if n_lat % (4 * tq) == 0 else 1
    out_seq = seq if with_ctx else n_lat
    blk = lambda b, h: (b, h)
    return pl.pallas_call(
        functools.partial(_attn_kernel, lam_init=lam_init, n_ctx=n_ctx, n_lat=n_lat, tq=tq, unroll=unroll,
                          with_ctx=with_ctx),
        grid=(nbatch, DIFF_HEADS),
        in_specs=[pl.BlockSpec((seq, DIFF_V_DIM), blk), pl.BlockSpec((seq, DIFF_V_DIM), blk),
                  pl.BlockSpec((seq, DIFF_V_DIM), blk),
                  pl.BlockSpec((4, DIFF_QK_DIM), lambda b, h: (0, 0)),
                  pl.BlockSpec((1, DIFF_V_DIM), lambda b, h: (0, 0))],
        out_specs=pl.BlockSpec((out_seq, DIFF_V_DIM), blk),
        out_shape=jax.ShapeDtypeStruct((nbatch * out_seq, D_DIFF), f32),
        scratch_shapes=[pltpu.VMEM((seq, 2 * DIFF_V_DIM), bf16)],
        compiler_params=_cparams(2),
        name="diffattn",
    )(q, k, v, lam_vecs, norm_g.reshape(1, DIFF_V_DIM))


def _segment_flags(i, nbc, nbs):
    j = i % nbs
    first = jnp.logical_or(j == 0, j == nbc)
    last = jnp.logical_or(j == nbc - 1, j == nbs - 1)
    return first, last


def _halo_specs(width, n_rows):
    per = TM // HALO
    nblk = n_rows // HALO
    prev = pl.BlockSpec((HALO, width), lambda i: (jnp.maximum(i * per - 1, 0), 0))
    nxt = pl.BlockSpec((HALO, width), lambda i: (jnp.minimum((i + 1) * per, nblk - 1), 0))
    return prev, nxt


def _conv_kernel(h_ref, hp_ref, hn_ref, w_ref, b_ref, g_ref, bb_ref, o_ref, buf, shf, *, nbc, nbs):
    first, last = _segment_flags(pl.program_id(0), nbc, nbs)
    buf[0:HALO, :] = jnp.where(first, 0.0, hp_ref[...])
    buf[HALO:HALO + TM, :] = h_ref[...]
    buf[HALO + TM:HALO + TM + HALO, :] = jnp.where(last, 0.0, hn_ref[...])
    sub = 64
    sublanes = 8
    first_tap = HALO - CONV_HALF
    for r0 in range(0, TM, sub):
        acc = jnp.zeros((sub, D_CONV), f32) + b_ref[...]
        for r in range(sublanes):
            taps = [kk for kk in range(CONV_WIDTH) if (first_tap + kk) % sublanes == r]
            if not taps:
                continue
            span = sub + sublanes * max((first_tap + kk) // sublanes for kk in taps)
            shf[0:span, :] = buf[r0 + r:r0 + r + span, :]
            for kk in taps:
                a = sublanes * ((first_tap + kk) // sublanes)
                acc = acc + shf[a:a + sub, :] * w_ref[kk:kk + 1, :]
        mu = jnp.mean(acc, axis=-1, keepdims=True)
        xc = acc - mu
        var = jnp.mean(xc * xc, axis=-1, keepdims=True)
        y = xc * lax.rsqrt(var + 1e-5) * g_ref[...] + bb_ref[...]
        o_ref[r0:r0 + sub, :] = y * _sigmoid(y)


def _conv(hc, dw_w, dw_b, ln_g, ln_b, nbc, nbs):
    m = hc.shape[0]
    prev, nxt = _halo_specs(D_CONV, m)
    const = lambda i: (0, 0)
    return pl.pallas_call(
        functools.partial(_conv_kernel, nbc=nbc, nbs=nbs),
        grid=(m // TM,),
        in_specs=[pl.BlockSpec((TM, D_CONV), lambda i: (i, 0)), prev, nxt,
                  pl.BlockSpec((CONV_WIDTH, D_CONV), const), pl.BlockSpec((1, D_CONV), const),
                  pl.BlockSpec((1, D_CONV), const), pl.BlockSpec((1, D_CONV), const)],
        out_specs=pl.BlockSpec((TM, D_CONV), lambda i: (i, 0)),
        out_shape=jax.ShapeDtypeStruct((m, D_CONV), f32),
        scratch_shapes=[pltpu.VMEM((TM + 2 * HALO, D_CONV), f32), pltpu.VMEM((64 + 2 * HALO, D_CONV), f32)],
        compiler_params=_cparams(1),
        name="convmod",
    )(hc, hc, hc, dw_w, dw_b.reshape(1, D_CONV), ln_g.reshape(1, D_CONV), ln_b.reshape(1, D_CONV))


def _rwkv_chunk_kernel(f_ref, fp_ref, fn_ref, mu_ref, w0_ref, w2_ref, a0_ref, a2_ref, g2_ref, kk_ref, ka_ref, rk_ref,
                       rp_ref, y0_ref, mm_ref, nn_ref, bv_ref, gg_ref,
                       s_r, s_v, s_a, s_lw, s_k, s_b, *, nbc, nbs):
    first, last = _segment_flags(pl.program_id(0), nbc, nbs)
    f = f_ref[...]
    prow = jnp.where(first, 0.0, fp_ref[HALO - 1:HALO, :])
    nrow = jnp.where(last, 0.0, fn_ref[0:1, :])
    rows = lax.broadcasted_iota(jnp.int32, f.shape, 0)
    prev = jnp.where(rows == 0, prow, pltpu.roll(f, 1, 0))
    nxt = jnp.where(rows == TM - 1, nrow, pltpu.roll(f, TM - 1, 0))
    fs = f + mu_ref[0:1, :] * (prev - f) + mu_ref[1:2, :] * (nxt - f)

    r = fs[:, 0:D_RWKV]
    k = fs[:, D_RWKV:2 * D_RWKV]
    v = fs[:, 2 * D_RWKV:3 * D_RWKV]
    o_w = 3 * D_RWKV
    o_a = o_w + 2 * DECAY_LORA
    o_g = o_a + 2 * AAA_LORA
    ones_bd = _head_ones(D_RWKV, HEAD_DIM)
    gg_ref[...] = _dot(_sigmoid(fs[:, o_g:o_g + GATE_LORA]).astype(bf16), g2_ref[...].astype(bf16))
    kk = k * kk_ref[...]
    kk = kk * lax.rsqrt(_dot_exact_rhs(kk * kk, ones_bd) + NORM_EPS)
    ksum = None
    for d in range(2):
        wd = fs[:, o_w + d * DECAY_LORA:o_w + (d + 1) * DECAY_LORA]
        ad = fs[:, o_a + d * AAA_LORA:o_a + (d + 1) * AAA_LORA]
        w_raw = w0_ref[d:d + 1, :] + _dot(jnp.tanh(wd).astype(bf16), w2_ref[d].astype(bf16))
        z = -w_raw
        softplus = jnp.maximum(z, 0.0) + jnp.log(1.0 + jnp.exp(-jnp.abs(z)))
        s_lw[d] = -jnp.exp(-softplus - 0.5)
        a = _sigmoid(a0_ref[d:d + 1, :] + _dot(ad.astype(bf16), a2_ref[d].astype(bf16)))
        kd = k * (1.0 + (a - 1.0) * ka_ref[...])
        ksum = kd if d == 0 else ksum + kd
        s_k[d] = kd
        s_b[d] = kk * a
    s_r[...] = r
    s_v[...] = v
    s_a[...] = -kk
    bv_ref[...] = _dot_exact_rhs(r * ksum * rk_ref[...], ones_bd) * v

    ri = lax.broadcasted_iota(jnp.int32, (CHUNK, CHUNK), 0)
    ci = lax.broadcasted_iota(jnp.int32, (CHUNK, CHUNK), 1)
    tri = tuple(jnp.where(msk, 1.0, 0.0).astype(bf16) for msk in (ci <= ri, ci >= ri))
    rp_i = lax.broadcasted_iota(jnp.int32, (CHUNK, PAIR), 0)
    cp_i = lax.broadcasted_iota(jnp.int32, (CHUNK, PAIR), 1) % CHUNK
    eye = jnp.where(rp_i == cp_i, 1.0, 0.0).astype(f32)
    strict = (cp_i < rp_i, cp_i > rp_i)
    incl = (cp_i <= rp_i, cp_i >= rp_i)
    low = lax.broadcasted_iota(jnp.int32, (CHUNK, PAIR), 1) < HEAD_DIM

    def chunk_body(c, carry):
        rs = pl.ds(pl.multiple_of(c * CHUNK, CHUNK), CHUNK)
        r_c = s_r[rs, :]
        v_c = s_v[rs, :]
        a_c = s_a[rs, :]
        v16_c = v_c.astype(bf16)
        at16, rt, bt16, kt16, wc = [], [], [], [], []
        for d in range(2):
            lw = s_lw[d, rs, :]
            cum = _dot_exact_lhs(tri[d], lw)
            er = jnp.exp(cum)
            ei = jnp.exp(-cum)
            at16.append((a_c * jnp.exp(cum - lw)).astype(bf16))
            rt.append(r_c * er)
            bt16.append((s_b[d, rs, :] * ei).astype(bf16))
            kt16.append((s_k[d, rs, :] * ei).astype(bf16))
            wc.append(er[CHUNK - 1:CHUNK, :] if d == 0 else er[0:1, :])

        insts = [(d, p) for d in range(2) for p in range(RWKV_HEADS // 2)]
        psl = lambda p: slice(p * PAIR, (p + 1) * PAIR)
        gms = [_dot_nt(jnp.concatenate([at16[d][:, psl(p)], rt[d][:, psl(p)].astype(bf16)], axis=0),
                       jnp.concatenate([_pair_blockdiag(bt16[d][:, psl(p)]), _pair_blockdiag(kt16[d][:, psl(p)])],
                                       axis=0)) for d, p in insts]
        lab = [jnp.where(strict[d], g[:CHUNK, :PAIR], 0.0) for (d, p), g in zip(insts, gms)]
        lakrk16 = [jnp.concatenate([jnp.where(strict[d], g[:CHUNK, PAIR:], 0.0),
                                    jnp.where(incl[d], g[CHUNK:, PAIR:], 0.0)], axis=0).astype(bf16)
                   for (d, p), g in zip(insts, gms)]
        lrb16 = [jnp.where(incl[d], g[CHUNK:, :PAIR], 0.0).astype(bf16) for (d, p), g in zip(insts, gms)]
        uv = [_dot(lk, _pair_blockdiag(v16_c[:, psl(p)])) for (d, p), lk in zip(insts, lakrk16)]
        vtk = [_dot_tn(v16_c[:, psl(p)], kt16[d][:, psl(p)]) for d, p in insts]
        q = lab
        z = [jnp.concatenate([at16[d][:, psl(p)].astype(f32), u[:CHUNK]], axis=1) for (d, p), u in zip(insts, uv)]
        n_stage = int(math.log2(CHUNK))
        zw = 2 * PAIR
        for stage in range(n_stage):
            z_hi = [zz.astype(bf16) for zz in z]
            z_lo = [(zz - zh.astype(f32)).astype(bf16) for zz, zh in zip(z, z_hi)]
            q_hi = [qq.astype(bf16) for qq in q]
            q_lo = [(qq - qh.astype(f32)).astype(bf16) for qq, qh in zip(q, q_hi)]
            if stage < n_stage - 1:
                pa = [_dot(qh, _pair_blockdiag(jnp.concatenate([zh, zl, qh, ql], axis=1)))
                      for qh, ql, zh, zl in zip(q_hi, q_lo, z_hi, z_lo)]
                pb = [_dot(ql, _pair_blockdiag(jnp.concatenate([zh, qh], axis=1)))
                      for qh, ql, zh in zip(q_hi, q_lo, z_hi)]
                q = [a[:, 2 * zw:2 * zw + PAIR] + a[:, 2 * zw + PAIR:] + b[:, zw:] for a, b in zip(pa, pb)]
            else:
                pa = [_dot(qh, _pair_blockdiag(jnp.concatenate([zh, zl], axis=1)))
                      for qh, zh, zl in zip(q_hi, z_hi, z_lo)]
                pb = [_dot(ql, _pair_blockdiag(zh)) for ql, zh in zip(q_lo, z_hi)]
            z = [zz + (a[:, :zw] + a[:, zw:2 * zw] + b[:, :zw]) for zz, a, b in zip(z, pa, pb)]
        au16 = [zz.astype(bf16) for zz in z]
        ry = [_dot(lr, _pair_blockdiag(au)) for lr, au in zip(lrb16, au16)]
        mn = [_dot_tn(au, bt16[d][:, psl(p)]) for (d, p), au in zip(insts, au16)]
        for i, (d, p) in enumerate(insts):
            ps = psl(p)
            rp_ref[d, rs, ps] = rt[d][:, ps] + ry[i][:, :PAIR]
            y0_ref[d, rs, ps] = ry[i][:, PAIR:] + uv[i][CHUNK:]
            wc_p = wc[d][:, ps]
            m_pair = jnp.where(low, mn[i][0:CHUNK], mn[i][CHUNK:2 * CHUNK])
            n_pair = jnp.where(low, mn[i][2 * CHUNK:3 * CHUNK], mn[i][3 * CHUNK:])
            k_pair = jnp.where(low, vtk[i][:CHUNK], vtk[i][CHUNK:])
            mm_ref[d, c, p] = (eye + m_pair) * wc_p
            nn_ref[d, c, p] = (n_pair + k_pair) * wc_p
        return carry

    lax.fori_loop(0, TM // CHUNK, chunk_body, 0)


def _rwkv_chunks(fr, mu, w0, w2, a0, a2, g2, kk, ka, rk, nbc, nbs):
    m = fr.shape[0]
    nct = TM // CHUNK
    prev, nxt = _halo_specs(RWKV_IN, m)
    c2 = lambda i: (0, 0)
    c3 = lambda i: (0, 0, 0)
    row = lambda i: (i, 0)
    dirrow = lambda i: (0, i, 0)
    ops = lambda i: (0, i, 0, 0, 0)
    return pl.pallas_call(
        functools.partial(_rwkv_chunk_kernel, nbc=nbc, nbs=nbs),
        grid=(m // TM,),
        in_specs=[pl.BlockSpec((TM, RWKV_IN), row), prev, nxt,
                  pl.BlockSpec((2, RWKV_IN), c2), pl.BlockSpec((2, D_RWKV), c2),
                  pl.BlockSpec((2, DECAY_LORA, D_RWKV), c3), pl.BlockSpec((2, D_RWKV), c2),
                  pl.BlockSpec((2, AAA_LORA, D_RWKV), c3), pl.BlockSpec((GATE_LORA, D_RWKV), c2),
                  pl.BlockSpec((1, D_RWKV), c2), pl.BlockSpec((1, D_RWKV), c2), pl.BlockSpec((1, D_RWKV), c2)],
        out_specs=[pl.BlockSpec((2, TM, D_RWKV), dirrow), pl.BlockSpec((2, TM, D_RWKV), dirrow),
                   pl.BlockSpec((2, nct, N_PAIR, HEAD_DIM, PAIR), ops),
                   pl.BlockSpec((2, nct, N_PAIR, HEAD_DIM, PAIR), ops),
                   pl.BlockSpec((TM, D_RWKV), row), pl.BlockSpec((TM, D_RWKV), row)],
        out_shape=[jax.ShapeDtypeStruct((2, m, D_RWKV), f32), jax.ShapeDtypeStruct((2, m, D_RWKV), f32),
                   jax.ShapeDtypeStruct((2, m // CHUNK, N_PAIR, HEAD_DIM, PAIR), f32),
                   jax.ShapeDtypeStruct((2, m // CHUNK, N_PAIR, HEAD_DIM, PAIR), f32),
                   jax.ShapeDtypeStruct((m, D_RWKV), f32), jax.ShapeDtypeStruct((m, D_RWKV), f32)],
        scratch_shapes=[pltpu.VMEM((TM, D_RWKV), f32), pltpu.VMEM((TM, D_RWKV), f32), pltpu.VMEM((TM, D_RWKV), f32),
                        pltpu.VMEM((2, TM, D_RWKV), f32), pltpu.VMEM((2, TM, D_RWKV), f32),
                        pltpu.VMEM((2, TM, D_RWKV), f32)],
        compiler_params=_cparams(1),
        name="rwkv_chunks",
    )(fr, fr, fr, mu, w0, w2, a0, a2, g2, kk.reshape(1, D_RWKV), ka.reshape(1, D_RWKV), rk.reshape(1, D_RWKV))


def _rwkv_state_kernel(mf_ref, mb_ref, nf_ref, nb_ref, sf_ref, sb_ref, st_ref, *, nbatch):
    @pl.when(pl.program_id(0) == 0)
    def _():
        st_ref[...] = jnp.zeros(st_ref.shape, f32)

    group = 8
    probs = [(d, b, p) for b in range(nbatch) for d in range(2) for p in range(N_PAIR)]
    for g0 in range(0, len(probs), group):
        grp = probs[g0:g0 + group]
        st = [st_ref[d, b, p] for d, b, p in grp]
        for (d, b, p), s in zip(grp, st):
            (sf_ref if d == 0 else sb_ref)[b, 0, p] = s
        mt = [(mf_ref if d == 0 else mb_ref)[0, b, 0, p] for d, b, p in grp]
        s1 = [s.astype(bf16) for s in st]
        s2 = [(s - a.astype(f32)).astype(bf16) for s, a in zip(st, s1)]
        m1 = [x.astype(bf16) for x in mt]
        m2 = [(x - a.astype(f32)).astype(bf16) for x, a in zip(mt, m1)]
        p12 = [_dot(a1, _pair_blockdiag(jnp.concatenate([b1, b2], axis=1))) for a1, b1, b2 in zip(s1, m1, m2)]
        p21 = [_dot(a2, _pair_blockdiag(b1)) for a2, b1 in zip(s2, m1)]
        for (d, b, p), x12, x21 in zip(grp, p12, p21):
            st_ref[d, b, p] = (x12[:, :PAIR] + x12[:, PAIR:] + x21) + (nf_ref if d == 0 else nb_ref)[0, b, 0, p]


def _rwkv_state(mm, nn, nbatch, nca, ncc):
    shp = (2, nbatch, nca, N_PAIR, HEAD_DIM, PAIR)
    blk = (1, nbatch, 1, N_PAIR, HEAD_DIM, PAIR)
    c_bwd = lambda s: jnp.where(s < ncc, ncc - 1 - s, nca - 1 - (s - ncc))
    fwd = pl.BlockSpec(blk, lambda s: (0, 0, s, 0, 0, 0))
    bwd = pl.BlockSpec(blk, lambda s: (1, 0, c_bwd(s), 0, 0, 0))
    oshape = jax.ShapeDtypeStruct(shp[1:], f32)
    return pl.pallas_call(
        functools.partial(_rwkv_state_kernel, nbatch=nbatch),
        grid=(nca,),
        in_specs=[fwd, bwd, fwd, bwd],
        out_specs=[pl.BlockSpec(blk[1:], lambda s: (0, s, 0, 0, 0)),
                   pl.BlockSpec(blk[1:], lambda s: (0, c_bwd(s), 0, 0, 0))],
        out_shape=[oshape, oshape],
        scratch_shapes=[pltpu.VMEM((2, nbatch, N_PAIR, HEAD_DIM, PAIR), f32)],
        compiler_params=_cparams(1),
        name="rwkv_state",
    )(mm.reshape(shp), mm.reshape(shp), nn.reshape(shp), nn.reshape(shp))


def _rwkv_out_kernel(rp_ref, y0_ref, sf_ref, sb_ref, bv_ref, gg_ref, lng_ref, lnb_ref, o_ref):
    rows = []
    for c in range(TM // CHUNK):
        rs = slice(c * CHUNK, (c + 1) * CHUNK)
        cols = []
        for p in range(N_PAIR):
            ps = slice(p * PAIR, (p + 1) * PAIR)
            lhs = jnp.concatenate([rp_ref[0, rs, ps], rp_ref[1, rs, ps]], axis=1).astype(bf16)
            rhs = jnp.concatenate([_pair_blockdiag(sf_ref[c, p].astype(bf16)),
                                   _pair_blockdiag(sb_ref[c, p].astype(bf16))], axis=1)
            cols.append(y0_ref[0, rs, ps] + y0_ref[1, rs, ps] + _dot_nt(lhs, rhs))
        rows.append(jnp.concatenate(cols, axis=1))
    y = jnp.concatenate(rows, axis=0)
    ones_bd = _head_ones(D_RWKV, HEAD_DIM)
    mu = _dot_exact_rhs(y, ones_bd) * (1.0 / HEAD_DIM)
    yc = y - mu
    var = _dot_exact_rhs(yc * yc, ones_bd) * (1.0 / HEAD_DIM)
    yn = yc * lax.rsqrt(var + GN_EPS) * lng_ref[...] + lnb_ref[...]
    o_ref[...] = (yn + bv_ref[...]) * gg_ref[...]


def _rwkv_out(rp, y0, sf, sb, bv, gg, ln_g, ln_b):
    m = bv.shape[0]
    nct = TM // CHUNK
    c2 = lambda i: (0, 0)
    row = lambda i: (i, 0)
    dirrow = lambda i: (0, i, 0)
    st = lambda i: (i, 0, 0, 0)
    sshape = (m // CHUNK, N_PAIR, HEAD_DIM, PAIR)
    return pl.pallas_call(
        _rwkv_out_kernel,
        grid=(m // TM,),
        in_specs=[pl.BlockSpec((2, TM, D_RWKV), dirrow), pl.BlockSpec((2, TM, D_RWKV), dirrow),
                  pl.BlockSpec((nct, N_PAIR, HEAD_DIM, PAIR), st),
                  pl.BlockSpec((nct, N_PAIR, HEAD_DIM, PAIR), st),
                  pl.BlockSpec((TM, D_RWKV), row), pl.BlockSpec((TM, D_RWKV), row),
                  pl.BlockSpec((1, D_RWKV), c2), pl.BlockSpec((1, D_RWKV), c2)],
        out_specs=pl.BlockSpec((TM, D_RWKV), row),
        out_shape=jax.ShapeDtypeStruct((m, D_RWKV), f32),
        compiler_params=_cparams(1),
        name="rwkv_out",
    )(rp, y0, sf.reshape(sshape), sb.reshape(sshape), bv, gg, ln_g.reshape(1, D_RWKV), ln_b.reshape(1, D_RWKV))


def _mixout_kernel(t_ref, a_ref, b_ref, c_ref, w_ref, m_ref, o_ref):
    out = _dot(a_ref[...].astype(bf16), w_ref[0:D_RWKV, :])
    out = out + _dot(b_ref[...].astype(bf16), w_ref[D_RWKV:D_RWKV + D_CONV, :])
    out = out + _dot(c_ref[...].astype(bf16), w_ref[D_RWKV + D_CONV:, :])
    o_ref[...] = t_ref[...] + m_ref[0, 5:6, :] * out


def _mixout(t, oa, ob, oc, w, mods, grid, row_map, mod_map, c_map, out_rows):
    d = t.shape[1]
    n_axes = len(grid)
    if n_axes == 1:
        out_map = lambda i: (i, 0)
    else:
        nj = grid[1]
        out_map = lambda b, j: (b * nj + j, 0)
    rmap = lambda *a: (row_map(*a), 0)
    return pl.pallas_call(
        _mixout_kernel,
        grid=grid,
        in_specs=[pl.BlockSpec((TM, d), rmap), pl.BlockSpec((TM, D_RWKV), rmap), pl.BlockSpec((TM, D_CONV), rmap),
                  pl.BlockSpec((TM, D_DIFF), lambda *a: (c_map(*a), 0)),
                  pl.BlockSpec((D_RWKV + D_CONV + D_DIFF, d), lambda *a: (0, 0)),
                  pl.BlockSpec((1, N_MOD, d), lambda *a: (mod_map(*a), 0, 0))],
        out_specs=pl.BlockSpec((TM, d), out_map),
        out_shape=jax.ShapeDtypeStruct((out_rows, d), f32),
        compiler_params=_cparams(n_axes),
        name="mixout",
    )(t, oa, ob, oc, w, mods)


def _rope_tables(n_ctx, n_lat):
    n_rows = n_lat // GRID_W
    row = jnp.repeat(jnp.arange(n_rows, dtype=jnp.int32), GRID_W)
    col = jnp.tile(jnp.arange(GRID_W, dtype=jnp.int32), n_rows)
    inv = 1.0 / (ROPE_THETA ** (jnp.arange(ROPE_FREQS, dtype=f32) * 2.0 / AXIS_DIM))
    ang = jnp.stack([row, col], axis=-1).astype(f32)[..., None] * inv
    cos, sin = jnp.cos(ang), jnp.sin(ang)
    cos64 = jnp.concatenate([cos[:, 0], cos[:, 0], cos[:, 1], cos[:, 1]], axis=-1)
    sin64 = jnp.concatenate([-sin[:, 0], sin[:, 0], -sin[:, 1], sin[:, 1]], axis=-1)
    reps = D_QK // DIFF_QK_DIM
    cos_t = jnp.concatenate([jnp.ones((n_ctx, D_QK), f32), jnp.tile(cos64, (1, reps))], axis=0)
    sin_t = jnp.concatenate([jnp.zeros((n_ctx, D_QK), f32), jnp.tile(sin64, (1, reps))], axis=0)
    return cos_t, sin_t


def kernel(x, c, ctx, c_ctx, ada_w, ada_b, norm_g, ffn_w_in, ffn_w_out, mix_w_in, mix_w_out, rwkv_mu, rwkv_w0, rwkv_w2,
           rwkv_a0, rwkv_a2, rwkv_g2, rwkv_kk, rwkv_ka, rwkv_rk, rwkv_ln_g, rwkv_ln_b, conv_dw_w, conv_dw_b, conv_ln_g,
           conv_ln_b, diff_lam, diff_norm_g, final_g):
    nb, n_lat, d = x.shape
    n_ctx = ctx.shape[1]
    depth = ada_w.shape[0]
    seq = n_ctx + n_lat
    assert n_ctx % TM == 0 and n_lat % TM == 0 and n_lat % GRID_W == 0 and seq % n_ctx == 0
    assert nb + 1 <= COND_ROWS
    nbc, nbx, nbs = n_ctx // TM, n_lat // TM, seq // TM
    m = nb * seq

    t = jnp.concatenate([ctx, x], axis=1).reshape(m, d)
    cond = jnp.zeros((COND_ROWS, d), f32).at[:nb].set(c).at[nb].set(c_ctx)
    mods = _adaln(cond, ada_w, ada_b).reshape(depth, COND_ROWS, N_MOD, d)
    cos_t, sin_t = _rope_tables(n_ctx, n_lat)
    w_in16 = ffn_w_in.astype(bf16)
    w_out16 = ffn_w_out.astype(bf16)
    mix_in16 = mix_w_in.astype(bf16)
    mix_out16 = mix_w_out.astype(bf16)

    mod_all = lambda i: jnp.where(i % nbs < nbc, nb, i // nbs)
    row_all = lambda i: i
    row_lat = lambda b, j: b * nbs + nbc + j
    mod_lat = lambda b, j: b

    out = None
    for l in range(depth):
        last = l == depth - 1
        ml = mods[l]
        t = _ffn(t, ml, norm_g[l, 0], w_in16[l, 0], w_out16[l, 0], 0, seq, n_ctx, nb)
        fr, hc, q, k, v = _mixin(t, ml, norm_g[l, 1], mix_in16[l], cos_t, sin_t, seq, n_ctx, nb)
        rp, y0, mm, nn, bv, gg = _rwkv_chunks(fr, rwkv_mu[l], rwkv_w0[l], rwkv_w2[l], rwkv_a0[l], rwkv_a2[l], rwkv_g2[l],
                                              rwkv_kk[l], rwkv_ka[l], rwkv_rk[l].reshape(D_RWKV), nbc, nbs)
        sf, sb = _rwkv_state(mm, nn, nb, seq // CHUNK, n_ctx // CHUNK)
        oa = _rwkv_out(rp, y0, sf, sb, bv, gg, rwkv_ln_g[l], rwkv_ln_b[l])
        ob = _conv(hc, conv_dw_w[l], conv_dw_b[l], conv_ln_g[l], conv_ln_b[l], nbc, nbs)
        lam_init = 0.8 - 0.6 * math.exp(-0.3 * l)
        if last:
            oc = _attn(q, k, v, diff_lam[l], diff_norm_g[l], lam_init, nb, n_ctx, n_lat, False)
            t = _mixout(t, oa, ob, oc, mix_out16[l], ml, (nb, nbx), row_lat, mod_lat, lambda b, j: b * nbx + j,
                        nb * n_lat)
            out = _ffn(t, ml, norm_g[l, 2], w_in16[l, 1], w_out16[l, 1], 6, n_lat, 0, nb, final_g=final_g)
        else:
            oc = _attn(q, k, v, diff_lam[l], diff_norm_g[l], lam_init, nb, n_ctx, n_lat, True)
            t = _mixout(t, oa, ob, oc, mix_out16[l], ml, (nbs * nb,), row_all, mod_all, row_all, m)
            t = _ffn(t, ml, norm_g[l, 2], w_in16[l, 1], w_out16[l, 1], 6, seq, n_ctx, nb)
    return out.reshape(nb, n_lat, d)
```

```python
import functools
import math

import jax
import jax.numpy as jnp
from jax import lax
from jax.experimental import pallas as pl
from jax.experimental.pallas import tpu as pltpu

f32 = jnp.float32
bf16 = jnp.bfloat16

N_MOD = 9
EPS = 1e-6
GRID_W = 64
RWKV_HEADS = 4
HEAD_DIM = 64
D_RWKV = RWKV_HEADS * HEAD_DIM
PAIR = 2 * HEAD_DIM
N_PAIR = RWKV_HEADS // 2
DECAY_LORA = 64
AAA_LORA = 64
GATE_LORA = 128
GN_EPS = 64e-5
NORM_EPS = 1e-12
D_CONV = 256
CONV_WIDTH = 31
CONV_HALF = CONV_WIDTH // 2
DIFF_HEADS = 4
DIFF_QK_DIM = 64
DIFF_V_DIM = 2 * DIFF_QK_DIM
D_DIFF = DIFF_HEADS * DIFF_V_DIM
ROPE_THETA = 10000.0
LOG2_E = 1.4426950408889634
AXIS_DIM = DIFF_QK_DIM // 2
ROPE_FREQS = AXIS_DIM // 2
RWKV_IN = 3 * D_RWKV + 2 * DECAY_LORA + 2 * AAA_LORA + GATE_LORA
CONV_IN = 2 * D_CONV
D_QK = DIFF_HEADS * 2 * DIFF_QK_DIM
DIFF_IN = 2 * D_QK + D_DIFF
P_IN = RWKV_IN + CONV_IN + DIFF_IN

TM = 256
CHUNK = 64
CHUNKS_PER_BODY = 2
HALO = 16
COND_ROWS = 16
VMEM_LIMIT = 56 * 1024 * 1024


def _dot(a, b, prec=None):
    return jnp.dot(a, b, preferred_element_type=f32, precision=prec)


def _dot_nt(a, b, prec=None):
    return lax.dot_general(a, b, (((1,), (1,)), ((), ())), preferred_element_type=f32, precision=prec)


def _dot_tn(a, b, prec=None):
    return lax.dot_general(a, b, (((0,), (0,)), ((), ())), preferred_element_type=f32, precision=prec)


def _split3(a):
    a1 = a.astype(bf16)
    r1 = a - a1.astype(f32)
    a2 = r1.astype(bf16)
    a3 = (r1 - a2.astype(f32)).astype(bf16)
    return a1, a2, a3


def _dot_exact_rhs(a, b16):
    a1, a2, a3 = _split3(a)
    return _dot(a1, b16) + _dot(a2, b16) + _dot(a3, b16)


def _dot_exact_lhs(a16, b):
    b1, b2, b3 = _split3(b)
    return _dot(a16, b1) + _dot(a16, b2) + _dot(a16, b3)


def _pair_blockdiag(x):
    first = (lax.broadcasted_iota(jnp.int32, x.shape, 1) % PAIR) < HEAD_DIM
    zero = jnp.zeros_like(x)
    return jnp.concatenate([jnp.where(first, x, zero), jnp.where(first, zero, x)], axis=0)


def _sigmoid(x):
    return 1.0 / (1.0 + jnp.exp(-x))


def _cparams(n_axes):
    return pltpu.CompilerParams(dimension_semantics=("arbitrary",) * n_axes, vmem_limit_bytes=VMEM_LIMIT)


def _head_ones(n, width):
    r = lax.broadcasted_iota(jnp.int32, (n, n), 0) // width
    c = lax.broadcasted_iota(jnp.int32, (n, n), 1) // width
    return jnp.where(r == c, 1.0, 0.0).astype(bf16)


def _adaln_kernel(c_ref, w_ref, b_ref, o_ref):
    cond = c_ref[...]
    cond = cond * _sigmoid(cond)
    o_ref[0] = _dot(cond.astype(bf16), w_ref[0].astype(bf16)) + b_ref[0]


def _adaln(cond, ada_w, ada_b):
    depth, d, nd = ada_w.shape
    tn = nd // 4
    return pl.pallas_call(
        _adaln_kernel,
        grid=(depth, nd // tn),
        in_specs=[pl.BlockSpec((COND_ROWS, d), lambda l, j: (0, 0)),
                  pl.BlockSpec((1, d, tn), lambda l, j: (l, 0, j)),
                  pl.BlockSpec((1, 1, tn), lambda l, j: (l, 0, j))],
        out_specs=pl.BlockSpec((1, COND_ROWS, tn), lambda l, j: (l, 0, j)),
        out_shape=jax.ShapeDtypeStruct((depth, COND_ROWS, nd), f32),
        compiler_params=_cparams(2),
        name="adaln",
    )(cond, ada_w, ada_b.reshape(depth, 1, nd))


def _modulated_norm(x, g, shift, scale):
    y = x * lax.rsqrt(jnp.mean(x * x, axis=-1, keepdims=True) + EPS) * g
    return y * (1.0 + scale) + shift


def _dense_tile(rows_per_seq):
    for tm in (1024, 768, 512, 256):
        if rows_per_seq % tm == 0:
            return tm
    raise ValueError(rows_per_seq)


def _mod_row(mb_ref, mc_ref, idx, tm, n_ctx, blocks_per_seq):
    if n_ctx == 0:
        return mb_ref[0, idx:idx + 1, :]
    row = (pl.program_id(0) % blocks_per_seq) * tm + lax.broadcasted_iota(jnp.int32, (tm, 1), 0)
    return jnp.where(row < n_ctx, mc_ref[0, idx:idx + 1, :], mb_ref[0, idx:idx + 1, :])


def _ffn_kernel(x_ref, mb_ref, mc_ref, g_ref, wi_ref, wo_ref, *rest, s0, ff, fc, final, tm, n_ctx, bps):
    o_ref = rest[-1]
    x = x_ref[...]
    mod = lambda idx: _mod_row(mb_ref, mc_ref, idx, tm, n_ctx, bps)
    h = _modulated_norm(x, g_ref[...], mod(s0), mod(s0 + 1)).astype(bf16)
    acc = jnp.zeros(x.shape, f32)
    for j in range(ff // fc):
        gt = _dot(h, wi_ref[:, j * fc:(j + 1) * fc])
        up = _dot(h, wi_ref[:, ff + j * fc:ff + (j + 1) * fc])
        act = (gt * _sigmoid(gt) * up).astype(bf16)
        acc = acc + _dot(act, wo_ref[j * fc:(j + 1) * fc, :])
    out = x + 0.5 * mod(s0 + 2) * acc
    if final:
        fg_ref = rest[0]
        out = out * lax.rsqrt(jnp.mean(out * out, axis=-1, keepdims=True) + EPS) * fg_ref[...]
    o_ref[...] = out


def _ffn(t, mods, g, w_in, w_out, s0, rows_per_seq, n_ctx, ctx_row, final_g=None):
    m, d = t.shape
    ff = w_out.shape[0]
    fc = 256 if ff % 256 == 0 else ff
    tm = _dense_tile(rows_per_seq)
    bps = rows_per_seq // tm
    const = lambda i: (0, 0)
    in_specs = [pl.BlockSpec((tm, d), lambda i: (i, 0)),
                pl.BlockSpec((1, N_MOD, d), lambda i: (i // bps, 0, 0)),
                pl.BlockSpec((1, N_MOD, d), lambda i: (ctx_row, 0, 0)),
                pl.BlockSpec((1, d), const),
                pl.BlockSpec((d, 2 * ff), const, pipeline_mode=pl.Buffered(1)),
                pl.BlockSpec((ff, d), const, pipeline_mode=pl.Buffered(1))]
    args = [t, mods, mods, g.reshape(1, d), w_in, w_out]
    if final_g is not None:
        in_specs.append(pl.BlockSpec((1, d), const))
        args.append(final_g.reshape(1, d))
    return pl.pallas_call(
        functools.partial(_ffn_kernel, s0=s0, ff=ff, fc=fc, final=final_g is not None, tm=tm, n_ctx=n_ctx, bps=bps),
        grid=(m // tm,), in_specs=in_specs,
        out_specs=pl.BlockSpec((tm, d), lambda i: (i, 0)),
        out_shape=jax.ShapeDtypeStruct((m, d), f32),
        compiler_params=_cparams(1),
        name="ffn",
    )(*args)


def _mixin_kernel(x_ref, mb_ref, mc_ref, g_ref, w_ref, cos_ref, sin_ref, fr_ref, hc_ref, q_ref, k_ref, v_ref, *, tm,
                  n_ctx, bps):
    x = x_ref[...]
    mod = lambda idx: _mod_row(mb_ref, mc_ref, idx, tm, n_ctx, bps)
    h = _modulated_norm(x, g_ref[...], mod(3), mod(4)).astype(bf16)
    fr_ref[...] = _dot(h, w_ref[:, 0:RWKV_IN])
    o = RWKV_IN
    val = _dot(h, w_ref[:, o:o + D_CONV])
    gate = _dot(h, w_ref[:, o + D_CONV:o + CONV_IN])
    hc_ref[...] = val * _sigmoid(gate)
    o = RWKV_IN + CONV_IN
    cos = cos_ref[...]
    sin = sin_ref[...]
    lane = lax.broadcasted_iota(jnp.int32, cos.shape, 1)
    low_half = ((lane // ROPE_FREQS) % 2) == 0

    def rope(t):
        partner = jnp.where(low_half, pltpu.roll(t, D_QK - ROPE_FREQS, 1), pltpu.roll(t, ROPE_FREQS, 1))
        return t * cos + partner * sin

    q = _dot(h, w_ref[:, o:o + D_QK])
    q_ref[...] = (rope(q) * (DIFF_QK_DIM ** -0.5 * LOG2_E)).astype(bf16)
    k = _dot(h, w_ref[:, o + D_QK:o + 2 * D_QK])
    k_ref[...] = rope(k).astype(bf16)
    v_ref[...] = _dot(h, w_ref[:, o + 2 * D_QK:o + 2 * D_QK + D_DIFF]).astype(bf16)


def _mixin(t, mods, g, w, cos, sin, seq, n_ctx, ctx_row):
    m, d = t.shape
    tm = _dense_tile(seq)
    bps = seq // tm
    const = lambda i: (0, 0)
    row = lambda i: (i, 0)
    return pl.pallas_call(
        functools.partial(_mixin_kernel, tm=tm, n_ctx=n_ctx, bps=bps),
        grid=(m // tm,),
        in_specs=[pl.BlockSpec((tm, d), row),
                  pl.BlockSpec((1, N_MOD, d), lambda i: (i // bps, 0, 0)),
                  pl.BlockSpec((1, N_MOD, d), lambda i: (ctx_row, 0, 0)),
                  pl.BlockSpec((1, d), const),
                  pl.BlockSpec((d, P_IN), const, pipeline_mode=pl.Buffered(1)),
                  pl.BlockSpec((tm, D_QK), lambda i: (i % bps, 0)),
                  pl.BlockSpec((tm, D_QK), lambda i: (i % bps, 0))],
        out_specs=[pl.BlockSpec((tm, RWKV_IN), row), pl.BlockSpec((tm, D_CONV), row),
                   pl.BlockSpec((tm, D_QK), row), pl.BlockSpec((tm, D_QK), row), pl.BlockSpec((tm, D_DIFF), row)],
        out_shape=[jax.ShapeDtypeStruct((m, RWKV_IN), f32), jax.ShapeDtypeStruct((m, D_CONV), f32),
                   jax.ShapeDtypeStruct((m, D_QK), bf16), jax.ShapeDtypeStruct((m, D_QK), bf16),
                   jax.ShapeDtypeStruct((m, D_DIFF), bf16)],
        compiler_params=_cparams(1),
        name="mixin",
    )(t, mods, mods, g.reshape(1, d), w, cos, sin)


def _attn_kernel(q_ref, k_ref, v_ref, lam_ref, g_ref, o_ref, vaug_ref, *, lam_init, n_ctx, n_lat, tq, unroll,
                 with_ctx):
    lv = lam_ref[...]
    lam = (jnp.exp(jnp.sum(lv[0:1] * lv[1:2], keepdims=True)) - jnp.exp(jnp.sum(lv[2:3] * lv[3:4], keepdims=True))
           + lam_init)

    @pl.when(jnp.logical_and(pl.program_id(0) == 0, pl.program_id(1) == 0))
    def _():
        vaug_ref[:, DIFF_V_DIM:] = jnp.ones((vaug_ref.shape[0], DIFF_V_DIM), bf16)

    vaug_ref[:, :DIFF_V_DIM] = v_ref[...]

    def attend(q, k, v):
        parts = []
        for m in range(2):
            sl = slice(m * DIFF_QK_DIM, (m + 1) * DIFF_QK_DIM)
            s = _dot_nt(q[:, sl], k[:, sl])
            e = jnp.exp2(s - jnp.max(s, axis=-1, keepdims=True))
            pv = _dot(e.astype(bf16), v)
            parts.append(pv[:, :DIFF_V_DIM] * (1.0 / pv[:, DIFF_V_DIM:]))
        o = parts[0] - lam * parts[1]
        o = o * lax.rsqrt(jnp.mean(o * o, axis=-1, keepdims=True) + 1e-5) * g_ref[...]
        return o * (1.0 - lam_init)

    out_off = 0
    if with_ctx:
        o_ref[0:n_ctx, :] = attend(q_ref[0:n_ctx, :], k_ref[0:n_ctx, :], vaug_ref[0:n_ctx, :])
        out_off = n_ctx
    align = math.gcd(n_ctx, tq)

    def body(i, carry):
        for u in range(unroll):
            t = i * unroll + u
            q_rows = pl.ds(pl.multiple_of(n_ctx + t * tq, align), tq)
            o_rows = pl.ds(pl.multiple_of(out_off + t * tq, align), tq)
            o_ref[o_rows, :] = attend(q_ref[q_rows, :], k_ref[...], vaug_ref[...])
        return carry

    lax.fori_loop(0, n_lat // (tq * unroll), body, 0)


def _attn(q, k, v, lam_vecs, norm_g, lam_init, nbatch, n_ctx, n_lat, with_ctx):
    seq = n_ctx + n_lat
    tq = TM
    unroll = 4 if n_lat % (4 * tq) == 0 else 1
    out_seq = seq if with_ctx else n_lat
    blk = lambda b, h: (b, h)
    return pl.pallas_call(
        functools.partial(_attn_kernel, lam_init=lam_init, n_ctx=n_ctx, n_lat=n_lat, tq=tq, unroll=unroll,
                          with_ctx=with_ctx),
        grid=(nbatch, DIFF_HEADS),
        in_specs=[pl.BlockSpec((seq, DIFF_V_DIM), blk), pl.BlockSpec((seq, DIFF_V_DIM), blk),
                  pl.BlockSpec((seq, DIFF_V_DIM), blk),
                  pl.BlockSpec((4, DIFF_QK_DIM), lambda b, h: (0, 0)),
                  pl.BlockSpec((1, DIFF_V_DIM), lambda b, h: (0, 0))],
        out_specs=pl.BlockSpec((out_seq, DIFF_V_DIM), blk),
        out_shape=jax.ShapeDtypeStruct((nbatch * out_seq, D_DIFF), f32),
        scratch_shapes=[pltpu.VMEM((seq, 2 * DIFF_V_DIM), bf16)],
        compiler_params=_cparams(2),
        name="diffattn",
    )(q, k, v, lam_vecs, norm_g.reshape(1, DIFF_V_DIM))


def _segment_flags(i, nbc, nbs):
    j = i % nbs
    first = jnp.logical_or(j == 0, j == nbc)
    last = jnp.logical_or(j == nbc - 1, j == nbs - 1)
    return first, last


def _halo_specs(width, n_rows):
    per = TM // HALO
    nblk = n_rows // HALO
    prev = pl.BlockSpec((HALO, width), lambda i: (jnp.maximum(i * per - 1, 0), 0))
    nxt = pl.BlockSpec((HALO, width), lambda i: (jnp.minimum((i + 1) * per, nblk - 1), 0))
    return prev, nxt


def _conv_kernel(h_ref, hp_ref, hn_ref, w_ref, b_ref, g_ref, bb_ref, o_ref, buf, shf, *, nbc, nbs):
    first, last = _segment_flags(pl.program_id(0), nbc, nbs)
    buf[0:HALO, :] = jnp.where(first, 0.0, hp_ref[...])
    buf[HALO:HALO + TM, :] = h_ref[...]
    buf[HALO + TM:HALO + TM + HALO, :] = jnp.where(last, 0.0, hn_ref[...])
    sub = 64
    sublanes = 8
    first_tap = HALO - CONV_HALF
    for r0 in range(0, TM, sub):
        acc = jnp.zeros((sub, D_CONV), f32) + b_ref[...]
        for r in range(sublanes):
            taps = [kk for kk in range(CONV_WIDTH) if (first_tap + kk) % sublanes == r]
            if not taps:
                continue
            span = sub + sublanes * max((first_tap + kk) // sublanes for kk in taps)
            shf[0:span, :] = buf[r0 + r:r0 + r + span, :]
            for kk in taps:
                a = sublanes * ((first_tap + kk) // sublanes)
                acc = acc + shf[a:a + sub, :] * w_ref[kk:kk + 1, :]
        mu = jnp.mean(acc, axis=-1, keepdims=True)
        xc = acc - mu
        var = jnp.mean(xc * xc, axis=-1, keepdims=True)
        y = xc * lax.rsqrt(var + 1e-5) * g_ref[...] + bb_ref[...]
        o_ref[r0:r0 + sub, :] = y * _sigmoid(y)


def _conv(hc, dw_w, dw_b, ln_g, ln_b, nbc, nbs):
    m = hc.shape[0]
    prev, nxt = _halo_specs(D_CONV, m)
    const = lambda i: (0, 0)
    return pl.pallas_call(
        functools.partial(_conv_kernel, nbc=nbc, nbs=nbs),
        grid=(m // TM,),
        in_specs=[pl.BlockSpec((TM, D_CONV), lambda i: (i, 0)), prev, nxt,
                  pl.BlockSpec((CONV_WIDTH, D_CONV), const), pl.BlockSpec((1, D_CONV), const),
                  pl.BlockSpec((1, D_CONV), const), pl.BlockSpec((1, D_CONV), const)],
        out_specs=pl.BlockSpec((TM, D_CONV), lambda i: (i, 0)),
        out_shape=jax.ShapeDtypeStruct((m, D_CONV), f32),
        scratch_shapes=[pltpu.VMEM((TM + 2 * HALO, D_CONV), f32), pltpu.VMEM((64 + 2 * HALO, D_CONV), f32)],
        compiler_params=_cparams(1),
        name="convmod",
    )(hc, hc, hc, dw_w, dw_b.reshape(1, D_CONV), ln_g.reshape(1, D_CONV), ln_b.reshape(1, D_CONV))


def _rwkv_chunk_kernel(f_ref, fp_ref, fn_ref, mu_ref, w0_ref, w2_ref, a0_ref, a2_ref, g2_ref, kk_ref, ka_ref, rk_ref,
                       rp_ref, y0_ref, mm_ref, nn_ref, bv_ref, gg_ref,
                       s_r, s_v, s_a, s_lw, s_k, s_b, *, nbc, nbs):
    first, last = _segment_flags(pl.program_id(0), nbc, nbs)
    f = f_ref[...]
    prow = jnp.where(first, 0.0, fp_ref[HALO - 1:HALO, :])
    nrow = jnp.where(last, 0.0, fn_ref[0:1, :])
    rows = lax.broadcasted_iota(jnp.int32, f.shape, 0)
    prev = jnp.where(rows == 0, prow, pltpu.roll(f, 1, 0))
    nxt = jnp.where(rows == TM - 1, nrow, pltpu.roll(f, TM - 1, 0))
    fs = f + mu_ref[0:1, :] * (prev - f) + mu_ref[1:2, :] * (nxt - f)

    r = fs[:, 0:D_RWKV]
    k = fs[:, D_RWKV:2 * D_RWKV]
    v = fs[:, 2 * D_RWKV:3 * D_RWKV]
    o_w = 3 * D_RWKV
    o_a = o_w + 2 * DECAY_LORA
    o_g = o_a + 2 * AAA_LORA
    ones_bd = _head_ones(D_RWKV, HEAD_DIM)
    gg_ref[...] = _dot(_sigmoid(fs[:, o_g:o_g + GATE_LORA]).astype(bf16), g2_ref[...].astype(bf16))
    kk = k * kk_ref[...]
    kk = kk * lax.rsqrt(_dot_exact_rhs(kk * kk, ones_bd) + NORM_EPS)
    ksum = None
    for d in range(2):
        wd = fs[:, o_w + d * DECAY_LORA:o_w + (d + 1) * DECAY_LORA]
        ad = fs[:, o_a + d * AAA_LORA:o_a + (d + 1) * AAA_LORA]
        w_raw = w0_ref[d:d + 1, :] + _dot(jnp.tanh(wd).astype(bf16), w2_ref[d].astype(bf16))
        z = -w_raw
        softplus = jnp.maximum(z, 0.0) + jnp.log(1.0 + jnp.exp(-jnp.abs(z)))
        s_lw[d] = -jnp.exp(-softplus - 0.5)
        a = _sigmoid(a0_ref[d:d + 1, :] + _dot(ad.astype(bf16), a2_ref[d].astype(bf16)))
        kd = k * (1.0 + (a - 1.0) * ka_ref[...])
        ksum = kd if d == 0 else ksum + kd
        s_k[d] = kd
        s_b[d] = kk * a
    s_r[...] = r
    s_v[...] = v
    s_a[...] = -kk
    bv_ref[...] = _dot_exact_rhs(r * ksum * rk_ref[...], ones_bd) * v

    ri = lax.broadcasted_iota(jnp.int32, (CHUNK, CHUNK), 0)
    ci = lax.broadcasted_iota(jnp.int32, (CHUNK, CHUNK), 1)
    tri = tuple(jnp.where(msk, 1.0, 0.0).astype(bf16) for msk in (ci <= ri, ci >= ri))
    rp_i = lax.broadcasted_iota(jnp.int32, (CHUNK, PAIR), 0)
    cp_i = lax.broadcasted_iota(jnp.int32, (CHUNK, PAIR), 1) % CHUNK
    eye = jnp.where(rp_i == cp_i, 1.0, 0.0).astype(f32)
    strict = (cp_i < rp_i, cp_i > rp_i)
    incl = (cp_i <= rp_i, cp_i >= rp_i)
    low = lax.broadcasted_iota(jnp.int32, (CHUNK, PAIR), 1) < HEAD_DIM

    def chunk_body(ci, carry):
        cs = [ci * CHUNKS_PER_BODY + u for u in range(CHUNKS_PER_BODY)]
        rss = [pl.ds(pl.multiple_of(c * CHUNK, CHUNK), CHUNK) for c in cs]
        v16_c, at16, rt, bt16, kt16, wc = [], {}, {}, {}, {}, {}
        for u, rs in enumerate(rss):
            r_c = s_r[rs, :]
            a_c = s_a[rs, :]
            v16_c.append(s_v[rs, :].astype(bf16))
            for d in range(2):
                lw = s_lw[d, rs, :]
                cum = _dot_exact_lhs(tri[d], lw)
                er = jnp.exp(cum)
                ei = jnp.exp(-cum)
                at16[u, d] = (a_c * jnp.exp(cum - lw)).astype(bf16)
                rt[u, d] = r_c * er
                bt16[u, d] = (s_b[d, rs, :] * ei).astype(bf16)
                kt16[u, d] = (s_k[d, rs, :] * ei).astype(bf16)
                wc[u, d] = er[CHUNK - 1:CHUNK, :] if d == 0 else er[0:1, :]

        insts = [(u, d, p) for u in range(CHUNKS_PER_BODY) for d in range(2) for p in range(N_PAIR)]
        psl = lambda p: slice(p * PAIR, (p + 1) * PAIR)
        gms = [_dot_nt(jnp.concatenate([at16[u, d][:, psl(p)], rt[u, d][:, psl(p)].astype(bf16)], axis=0),
                       jnp.concatenate([_pair_blockdiag(bt16[u, d][:, psl(p)]),
                                        _pair_blockdiag(kt16[u, d][:, psl(p)])], axis=0)) for u, d, p in insts]
        lab = [jnp.where(strict[d], g[:CHUNK, :PAIR], 0.0) for (u, d, p), g in zip(insts, gms)]
        lakrk16 = [jnp.concatenate([jnp.where(strict[d], g[:CHUNK, PAIR:], 0.0),
                                    jnp.where(incl[d], g[CHUNK:, PAIR:], 0.0)], axis=0).astype(bf16)
                   for (u, d, p), g in zip(insts, gms)]
        lrb16 = [jnp.where(incl[d], g[CHUNK:, :PAIR], 0.0).astype(bf16) for (u, d, p), g in zip(insts, gms)]
        uv = [_dot(lk, _pair_blockdiag(v16_c[u][:, psl(p)])) for (u, d, p), lk in zip(insts, lakrk16)]
        vtk = [_dot_tn(v16_c[u][:, psl(p)], kt16[u, d][:, psl(p)]) for u, d, p in insts]
        def split(t):
            hi = t.astype(bf16)
            return hi, (t - hi.astype(f32)).astype(bf16)

        def mul_x3(a, b):
            a_hi, a_lo = split(a)
            b_hi, b_lo = split(b)
            w = b.shape[1]
            full = _dot(a_hi, _pair_blockdiag(jnp.concatenate([b_hi, b_lo], axis=1)))
            return full[:, :w] + full[:, w:] + _dot(a_lo, _pair_blockdiag(b_hi))

        n_stage = int(math.log2(CHUNK))
        x = [eye + l for l in lab]
        q = [mul_x3(l, l) for l in lab]
        for stage in range(1, n_stage):
            if stage < n_stage - 1:
                prod = [mul_x3(jnp.concatenate([xx, qq], axis=0), qq) for xx, qq in zip(x, q)]
                x = [xx + pr[:CHUNK] for xx, pr in zip(x, prod)]
                q = [pr[CHUNK:] for pr in prod]
            else:
                x = [xx + mul_x3(xx, qq) for xx, qq in zip(x, q)]
        z0 = [jnp.concatenate([at16[u, d][:, psl(p)].astype(f32), lv[:CHUNK]], axis=1)
              for (u, d, p), lv in zip(insts, uv)]
        au16 = [mul_x3(xx, zz).astype(bf16) for xx, zz in zip(x, z0)]
        ry = [_dot(lr, _pair_blockdiag(au)) for lr, au in zip(lrb16, au16)]
        mn = [_dot_tn(au, bt16[u, d][:, psl(p)]) for (u, d, p), au in zip(insts, au16)]
        for i, (u, d, p) in enumerate(insts):
            ps = psl(p)
            rp_ref[d, rss[u], ps] = rt[u, d][:, ps] + ry[i][:, :PAIR]
            y0_ref[d, rss[u], ps] = ry[i][:, PAIR:] + uv[i][CHUNK:]
            wc_p = wc[u, d][:, ps]
            m_pair = jnp.where(low, mn[i][0:CHUNK], mn[i][CHUNK:2 * CHUNK])
            n_pair = jnp.where(low, mn[i][2 * CHUNK:3 * CHUNK], mn[i][3 * CHUNK:])
            k_pair = jnp.where(low, vtk[i][:CHUNK], vtk[i][CHUNK:])
            mm_ref[d, cs[u], p] = (eye + m_pair) * wc_p
            nn_ref[d, cs[u], p] = (n_pair + k_pair) * wc_p
        return carry

    lax.fori_loop(0, TM // (CHUNK * CHUNKS_PER_BODY), chunk_body, 0)


def _rwkv_chunks(fr, mu, w0, w2, a0, a2, g2, kk, ka, rk, nbc, nbs):
    m = fr.shape[0]
    nct = TM // CHUNK
    prev, nxt = _halo_specs(RWKV_IN, m)
    c2 = lambda i: (0, 0)
    c3 = lambda i: (0, 0, 0)
    row = lambda i: (i, 0)
    dirrow = lambda i: (0, i, 0)
    ops = lambda i: (0, i, 0, 0, 0)
    return pl.pallas_call(
        functools.partial(_rwkv_chunk_kernel, nbc=nbc, nbs=nbs),
        grid=(m // TM,),
        in_specs=[pl.BlockSpec((TM, RWKV_IN), row), prev, nxt,
                  pl.BlockSpec((2, RWKV_IN), c2), pl.BlockSpec((2, D_RWKV), c2),
                  pl.BlockSpec((2, DECAY_LORA, D_RWKV), c3), pl.BlockSpec((2, D_RWKV), c2),
                  pl.BlockSpec((2, AAA_LORA, D_RWKV), c3), pl.BlockSpec((GATE_LORA, D_RWKV), c2),
                  pl.BlockSpec((1, D_RWKV), c2), pl.BlockSpec((1, D_RWKV), c2), pl.BlockSpec((1, D_RWKV), c2)],
        out_specs=[pl.BlockSpec((2, TM, D_RWKV), dirrow), pl.BlockSpec((2, TM, D_RWKV), dirrow),
                   pl.BlockSpec((2, nct, N_PAIR, HEAD_DIM, PAIR), ops),
                   pl.BlockSpec((2, nct, N_PAIR, HEAD_DIM, PAIR), ops),
                   pl.BlockSpec((TM, D_RWKV), row), pl.BlockSpec((TM, D_RWKV), row)],
        out_shape=[jax.ShapeDtypeStruct((2, m, D_RWKV), f32), jax.ShapeDtypeStruct((2, m, D_RWKV), f32),
                   jax.ShapeDtypeStruct((2, m // CHUNK, N_PAIR, HEAD_DIM, PAIR), f32),
                   jax.ShapeDtypeStruct((2, m // CHUNK, N_PAIR, HEAD_DIM, PAIR), f32),
                   jax.ShapeDtypeStruct((m, D_RWKV), f32), jax.ShapeDtypeStruct((m, D_RWKV), f32)],
        scratch_shapes=[pltpu.VMEM((TM, D_RWKV), f32), pltpu.VMEM((TM, D_RWKV), f32), pltpu.VMEM((TM, D_RWKV), f32),
                        pltpu.VMEM((2, TM, D_RWKV), f32), pltpu.VMEM((2, TM, D_RWKV), f32),
                        pltpu.VMEM((2, TM, D_RWKV), f32)],
        compiler_params=_cparams(1),
        name="rwkv_chunks",
    )(fr, fr, fr, mu, w0, w2, a0, a2, g2, kk.reshape(1, D_RWKV), ka.reshape(1, D_RWKV), rk.reshape(1, D_RWKV))


def _rwkv_state_kernel(mf_ref, mb_ref, nf_ref, nb_ref, sf_ref, sb_ref, st_ref, *, nbatch):
    @pl.when(pl.program_id(0) == 0)
    def _():
        st_ref[...] = jnp.zeros(st_ref.shape, f32)

    group = 8
    probs = [(d, b, p) for b in range(nbatch) for d in range(2) for p in range(N_PAIR)]
    for g0 in range(0, len(probs), group):
        grp = probs[g0:g0 + group]
        st = [st_ref[d, b, p] for d, b, p in grp]
        for (d, b, p), s in zip(grp, st):
            (sf_ref if d == 0 else sb_ref)[b, 0, p] = s
        mt = [(mf_ref if d == 0 else mb_ref)[0, b, 0, p] for d, b, p in grp]
        s1 = [s.astype(bf16) for s in st]
        s2 = [(s - a.astype(f32)).astype(bf16) for s, a in zip(st, s1)]
        m1 = [x.astype(bf16) for x in mt]
        m2 = [(x - a.astype(f32)).astype(bf16) for x, a in zip(mt, m1)]
        p12 = [_dot(a1, _pair_blockdiag(jnp.concatenate([b1, b2], axis=1))) for a1, b1, b2 in zip(s1, m1, m2)]
        p21 = [_dot(a2, _pair_blockdiag(b1)) for a2, b1 in zip(s2, m1)]
        for (d, b, p), x12, x21 in zip(grp, p12, p21):
            st_ref[d, b, p] = (x12[:, :PAIR] + x12[:, PAIR:] + x21) + (nf_ref if d == 0 else nb_ref)[0, b, 0, p]


def _rwkv_state(mm, nn, nbatch, nca, ncc):
    shp = (2, nbatch, nca, N_PAIR, HEAD_DIM, PAIR)
    blk = (1, nbatch, 1, N_PAIR, HEAD_DIM, PAIR)
    c_bwd = lambda s: jnp.where(s < ncc, ncc - 1 - s, nca - 1 - (s - ncc))
    fwd = pl.BlockSpec(blk, lambda s: (0, 0, s, 0, 0, 0))
    bwd = pl.BlockSpec(blk, lambda s: (1, 0, c_bwd(s), 0, 0, 0))
    oshape = jax.ShapeDtypeStruct(shp[1:], f32)
    return pl.pallas_call(
        functools.partial(_rwkv_state_kernel, nbatch=nbatch),
        grid=(nca,),
        in_specs=[fwd, bwd, fwd, bwd],
        out_specs=[pl.BlockSpec(blk[1:], lambda s: (0, s, 0, 0, 0)),
                   pl.BlockSpec(blk[1:], lambda s: (0, c_bwd(s), 0, 0, 0))],
        out_shape=[oshape, oshape],
        scratch_shapes=[pltpu.VMEM((2, nbatch, N_PAIR, HEAD_DIM, PAIR), f32)],
        compiler_params=_cparams(1),
        name="rwkv_state",
    )(mm.reshape(shp), mm.reshape(shp), nn.reshape(shp), nn.reshape(shp))


def _rwkv_out_kernel(rp_ref, y0_ref, sf_ref, sb_ref, bv_ref, gg_ref, lng_ref, lnb_ref, o_ref):
    rows = []
    for c in range(TM // CHUNK):
        rs = slice(c * CHUNK, (c + 1) * CHUNK)
        cols = []
        for p in range(N_PAIR):
            ps = slice(p * PAIR, (p + 1) * PAIR)
            lhs = jnp.concatenate([rp_ref[0, rs, ps], rp_ref[1, rs, ps]], axis=1).astype(bf16)
            rhs = jnp.concatenate([_pair_blockdiag(sf_ref[c, p].astype(bf16)),
                                   _pair_blockdiag(sb_ref[c, p].astype(bf16))], axis=1)
            cols.append(y0_ref[0, rs, ps] + y0_ref[1, rs, ps] + _dot_nt(lhs, rhs))
        rows.append(jnp.concatenate(cols, axis=1))
    y = jnp.concatenate(rows, axis=0)
    ones_bd = _head_ones(D_RWKV, HEAD_DIM)
    mu = _dot_exact_rhs(y, ones_bd) * (1.0 / HEAD_DIM)
    yc = y - mu
    var = _dot_exact_rhs(yc * yc, ones_bd) * (1.0 / HEAD_DIM)
    yn = yc * lax.rsqrt(var + GN_EPS) * lng_ref[...] + lnb_ref[...]
    o_ref[...] = (yn + bv_ref[...]) * gg_ref[...]


def _rwkv_out(rp, y0, sf, sb, bv, gg, ln_g, ln_b):
    m = bv.shape[0]
    nct = TM // CHUNK
    c2 = lambda i: (0, 0)
    row = lambda i: (i, 0)
    dirrow = lambda i: (0, i, 0)
    st = lambda i: (i, 0, 0, 0)
    sshape = (m // CHUNK, N_PAIR, HEAD_DIM, PAIR)
    return pl.pallas_call(
        _rwkv_out_kernel,
        grid=(m // TM,),
        in_specs=[pl.BlockSpec((2, TM, D_RWKV), dirrow), pl.BlockSpec((2, TM, D_RWKV), dirrow),
                  pl.BlockSpec((nct, N_PAIR, HEAD_DIM, PAIR), st),
                  pl.BlockSpec((nct, N_PAIR, HEAD_DIM, PAIR), st),
                  pl.BlockSpec((TM, D_RWKV), row), pl.BlockSpec((TM, D_RWKV), row),
                  pl.BlockSpec((1, D_RWKV), c2), pl.BlockSpec((1, D_RWKV), c2)],
        out_specs=pl.BlockSpec((TM, D_RWKV), row),
        out_shape=jax.ShapeDtypeStruct((m, D_RWKV), f32),
        compiler_params=_cparams(1),
        name="rwkv_out",
    )(rp, y0, sf.reshape(sshape), sb.reshape(sshape), bv, gg, ln_g.reshape(1, D_RWKV), ln_b.reshape(1, D_RWKV))


def _mixout_kernel(t_ref, a_ref, b_ref, c_ref, w_ref, m_ref, o_ref):
    out = _dot(a_ref[...].astype(bf16), w_ref[0:D_RWKV, :])
    out = out + _dot(b_ref[...].astype(bf16), w_ref[D_RWKV:D_RWKV + D_CONV, :])
    out = out + _dot(c_ref[...].astype(bf16), w_ref[D_RWKV + D_CONV:, :])
    o_ref[...] = t_ref[...] + m_ref[0, 5:6, :] * out


def _mixout(t, oa, ob, oc, w, mods, grid, row_map, mod_map, c_map, out_rows):
    d = t.shape[1]
    n_axes = len(grid)
    if n_axes == 1:
        out_map = lambda i: (i, 0)
    else:
        nj = grid[1]
        out_map = lambda b, j: (b * nj + j, 0)
    rmap = lambda *a: (row_map(*a), 0)
    return pl.pallas_call(
        _mixout_kernel,
        grid=grid,
        in_specs=[pl.BlockSpec((TM, d), rmap), pl.BlockSpec((TM, D_RWKV), rmap), pl.BlockSpec((TM, D_CONV), rmap),
                  pl.BlockSpec((TM, D_DIFF), lambda *a: (c_map(*a), 0)),
                  pl.BlockSpec((D_RWKV + D_CONV + D_DIFF, d), lambda *a: (0, 0)),
                  pl.BlockSpec((1, N_MOD, d), lambda *a: (mod_map(*a), 0, 0))],
        out_specs=pl.BlockSpec((TM, d), out_map),
        out_shape=jax.ShapeDtypeStruct((out_rows, d), f32),
        compiler_params=_cparams(n_axes),
        name="mixout",
    )(t, oa, ob, oc, w, mods)


def _rope_tables(n_ctx, n_lat):
    n_rows = n_lat // GRID_W
    row = jnp.repeat(jnp.arange(n_rows, dtype=jnp.int32), GRID_W)
    col = jnp.tile(jnp.arange(GRID_W, dtype=jnp.int32), n_rows)
    inv = 1.0 / (ROPE_THETA ** (jnp.arange(ROPE_FREQS, dtype=f32) * 2.0 / AXIS_DIM))
    ang = jnp.stack([row, col], axis=-1).astype(f32)[..., None] * inv
    cos, sin = jnp.cos(ang), jnp.sin(ang)
    cos64 = jnp.concatenate([cos[:, 0], cos[:, 0], cos[:, 1], cos[:, 1]], axis=-1)
    sin64 = jnp.concatenate([-sin[:, 0], sin[:, 0], -sin[:, 1], sin[:, 1]], axis=-1)
    reps = D_QK // DIFF_QK_DIM
    cos_t = jnp.concatenate([jnp.ones((n_ctx, D_QK), f32), jnp.tile(cos64, (1, reps))], axis=0)
    sin_t = jnp.concatenate([jnp.zeros((n_ctx, D_QK), f32), jnp.tile(sin64, (1, reps))], axis=0)
    return cos_t, sin_t


def kernel(x, c, ctx, c_ctx, ada_w, ada_b, norm_g, ffn_w_in, ffn_w_out, mix_w_in, mix_w_out, rwkv_mu, rwkv_w0, rwkv_w2,
           rwkv_a0, rwkv_a2, rwkv_g2, rwkv_kk, rwkv_ka, rwkv_rk, rwkv_ln_g, rwkv_ln_b, conv_dw_w, conv_dw_b, conv_ln_g,
           conv_ln_b, diff_lam, diff_norm_g, final_g):
    nb, n_lat, d = x.shape
    n_ctx = ctx.shape[1]
    depth = ada_w.shape[0]
    seq = n_ctx + n_lat
    assert n_ctx % TM == 0 and n_lat % TM == 0 and n_lat % GRID_W == 0 and seq % n_ctx == 0
    assert nb + 1 <= COND_ROWS
    nbc, nbx, nbs = n_ctx // TM, n_lat // TM, seq // TM
    m = nb * seq

    t = jnp.concatenate([ctx, x], axis=1).reshape(m, d)
    cond = jnp.zeros((COND_ROWS, d), f32).at[:nb].set(c).at[nb].set(c_ctx)
    mods = _adaln(cond, ada_w, ada_b).reshape(depth, COND_ROWS, N_MOD, d)
    cos_t, sin_t = _rope_tables(n_ctx, n_lat)
    w_in16 = ffn_w_in.astype(bf16)
    w_out16 = ffn_w_out.astype(bf16)
    mix_in16 = mix_w_in.astype(bf16)
    mix_out16 = mix_w_out.astype(bf16)

    mod_all = lambda i: jnp.where(i % nbs < nbc, nb, i // nbs)
    row_all = lambda i: i
    row_lat = lambda b, j: b * nbs + nbc + j
    mod_lat = lambda b, j: b

    out = None
    for l in range(depth):
        last = l == depth - 1
        ml = mods[l]
        t = _ffn(t, ml, norm_g[l, 0], w_in16[l, 0], w_out16[l, 0], 0, seq, n_ctx, nb)
        fr, hc, q, k, v = _mixin(t, ml, norm_g[l, 1], mix_in16[l], cos_t, sin_t, seq, n_ctx, nb)
        rp, y0, mm, nn, bv, gg = _rwkv_chunks(fr, rwkv_mu[l], rwkv_w0[l], rwkv_w2[l], rwkv_a0[l], rwkv_a2[l], rwkv_g2[l],
                                              rwkv_kk[l], rwkv_ka[l], rwkv_rk[l].reshape(D_RWKV), nbc, nbs)
        sf, sb = _rwkv_state(mm, nn, nb, seq // CHUNK, n_ctx // CHUNK)
        oa = _rwkv_out(rp, y0, sf, sb, bv, gg, rwkv_ln_g[l], rwkv_ln_b[l])
        ob = _conv(hc, conv_dw_w[l], conv_dw_b[l], conv_ln_g[l], conv_ln_b[l], nbc, nbs)
        lam_init = 0.8 - 0.6 * math.exp(-0.3 * l)
        if last:
            oc = _attn(q, k, v, diff_lam[l], diff_norm_g[l], lam_init, nb, n_ctx, n_lat, False)
            t = _mixout(t, oa, ob, oc, mix_out16[l], ml, (nb, nbx), row_lat, mod_lat, lambda b, j: b * nbx + j,
                        nb * n_lat)
            out = _ffn(t, ml, norm_g[l, 2], w_in16[l, 1], w_out16[l, 1], 6, n_lat, 0, nb, final_g=final_g)
        else:
            oc = _attn(q, k, v, diff_lam[l], diff_norm_g[l], lam_init, nb, n_ctx, n_lat, True)
            t = _mixout(t, oa, ob, oc, mix_out16[l], ml, (nbs * nb,), row_all, mod_all, row_all, m)
            t = _ffn(t, ml, norm_g[l, 2], w_in16[l, 1], w_out16[l, 1], 6, seq, n_ctx, nb)
    return out.reshape(nb, n_lat, d)
```

```python
import functools
import math

import jax
import jax.numpy as jnp
from jax import lax
from jax.experimental import pallas as pl
from jax.experimental.pallas import tpu as pltpu

f32 = jnp.float32
bf16 = jnp.bfloat16

N_MOD = 9
EPS = 1e-6
GRID_W = 64
RWKV_HEADS = 4
HEAD_DIM = 64
D_RWKV = RWKV_HEADS * HEAD_DIM
PAIR = 2 * HEAD_DIM
N_PAIR = RWKV_HEADS // 2
DECAY_LORA = 64
AAA_LORA = 64
GATE_LORA = 128
GN_EPS = 64e-5
NORM_EPS = 1e-12
D_CONV = 256
CONV_WIDTH = 31
CONV_HALF = CONV_WIDTH // 2
DIFF_HEADS = 4
DIFF_QK_DIM = 64
DIFF_V_DIM = 2 * DIFF_QK_DIM
D_DIFF = DIFF_HEADS * DIFF_V_DIM
ROPE_THETA = 10000.0
LOG2_E = 1.4426950408889634
AXIS_DIM = DIFF_QK_DIM // 2
ROPE_FREQS = AXIS_DIM // 2
RWKV_IN = 3 * D_RWKV + 2 * DECAY_LORA + 2 * AAA_LORA + GATE_LORA
CONV_IN = 2 * D_CONV
D_QK = DIFF_HEADS * 2 * DIFF_QK_DIM
DIFF_IN = 2 * D_QK + D_DIFF
P_IN = RWKV_IN + CONV_IN + DIFF_IN

TM = 256
CHUNK = 64
CHUNKS_PER_BODY = 2
HALO = 16
COND_ROWS = 16
VMEM_LIMIT = 56 * 1024 * 1024


def _dot(a, b, prec=None):
    return jnp.dot(a, b, preferred_element_type=f32, precision=prec)


def _dot_nt(a, b, prec=None):
    return lax.dot_general(a, b, (((1,), (1,)), ((), ())), preferred_element_type=f32, precision=prec)


def _dot_tn(a, b, prec=None):
    return lax.dot_general(a, b, (((0,), (0,)), ((), ())), preferred_element_type=f32, precision=prec)


def _split3(a):
    a1 = a.astype(bf16)
    r1 = a - a1.astype(f32)
    a2 = r1.astype(bf16)
    a3 = (r1 - a2.astype(f32)).astype(bf16)
    return a1, a2, a3


def _dot_exact_rhs(a, b16):
    a1, a2, a3 = _split3(a)
    return _dot(a1, b16) + _dot(a2, b16) + _dot(a3, b16)


def _dot_exact_lhs(a16, b):
    b1, b2, b3 = _split3(b)
    return _dot(a16, b1) + _dot(a16, b2) + _dot(a16, b3)


def _pair_blockdiag(x):
    first = (lax.broadcasted_iota(jnp.int32, x.shape, 1) % PAIR) < HEAD_DIM
    zero = jnp.zeros_like(x)
    return jnp.concatenate([jnp.where(first, x, zero), jnp.where(first, zero, x)], axis=0)


def _sigmoid(x):
    return 1.0 / (1.0 + jnp.exp(-x))


def _cparams(n_axes):
    return pltpu.CompilerParams(dimension_semantics=("arbitrary",) * n_axes, vmem_limit_bytes=VMEM_LIMIT)


def _head_ones(n, width):
    r = lax.broadcasted_iota(jnp.int32, (n, n), 0) // width
    c = lax.broadcasted_iota(jnp.int32, (n, n), 1) // width
    return jnp.where(r == c, 1.0, 0.0).astype(bf16)


def _adaln_kernel(c_ref, w_ref, b_ref, o_ref):
    cond = c_ref[...]
    cond = cond * _sigmoid(cond)
    o_ref[0] = _dot(cond.astype(bf16), w_ref[0].astype(bf16)) + b_ref[0]


def _adaln(cond, ada_w, ada_b):
    depth, d, nd = ada_w.shape
    tn = nd // 4
    return pl.pallas_call(
        _adaln_kernel,
        grid=(depth, nd // tn),
        in_specs=[pl.BlockSpec((COND_ROWS, d), lambda l, j: (0, 0)),
                  pl.BlockSpec((1, d, tn), lambda l, j: (l, 0, j)),
                  pl.BlockSpec((1, 1, tn), lambda l, j: (l, 0, j))],
        out_specs=pl.BlockSpec((1, COND_ROWS, tn), lambda l, j: (l, 0, j)),
        out_shape=jax.ShapeDtypeStruct((depth, COND_ROWS, nd), f32),
        compiler_params=_cparams(2),
        name="adaln",
    )(cond, ada_w, ada_b.reshape(depth, 1, nd))


def _modulated_norm(x, g, shift, scale):
    y = x * lax.rsqrt(jnp.mean(x * x, axis=-1, keepdims=True) + EPS) * g
    return y * (1.0 + scale) + shift


def _dense_tile(rows_per_seq):
    for tm in (1024, 768, 512, 256):
        if rows_per_seq % tm == 0:
            return tm
    raise ValueError(rows_per_seq)


def _mod_row(mb_ref, mc_ref, idx, tm, n_ctx, blocks_per_seq):
    if n_ctx == 0:
        return mb_ref[0, idx:idx + 1, :]
    row = (pl.program_id(0) % blocks_per_seq) * tm + lax.broadcasted_iota(jnp.int32, (tm, 1), 0)
    return jnp.where(row < n_ctx, mc_ref[0, idx:idx + 1, :], mb_ref[0, idx:idx + 1, :])


def _ffn_kernel(x_ref, mb_ref, mc_ref, g_ref, wi_ref, wo_ref, *rest, s0, ff, fc, final, tm, n_ctx, bps):
    o_ref = rest[-1]
    x = x_ref[...]
    mod = lambda idx: _mod_row(mb_ref, mc_ref, idx, tm, n_ctx, bps)
    h = _modulated_norm(x, g_ref[...], mod(s0), mod(s0 + 1)).astype(bf16)
    acc = jnp.zeros(x.shape, f32)
    for j in range(ff // fc):
        gt = _dot(h, wi_ref[:, j * fc:(j + 1) * fc])
        up = _dot(h, wi_ref[:, ff + j * fc:ff + (j + 1) * fc])
        act = (gt * _sigmoid(gt) * up).astype(bf16)
        acc = acc + _dot(act, wo_ref[j * fc:(j + 1) * fc, :])
    out = x + 0.5 * mod(s0 + 2) * acc
    if final:
        fg_ref = rest[0]
        out = out * lax.rsqrt(jnp.mean(out * out, axis=-1, keepdims=True) + EPS) * fg_ref[...]
    o_ref[...] = out


def _ffn(t, mods, g, w_in, w_out, layer, slot, s0, rows_per_seq, n_ctx, ctx_row, final_g=None):
    m, d = t.shape
    ff = w_out.shape[2]
    fc = 256 if ff % 256 == 0 else ff
    tm = _dense_tile(rows_per_seq)
    bps = rows_per_seq // tm
    const = lambda i: (0, 0)
    pick = lambda i: (layer, slot, 0, 0)
    in_specs = [pl.BlockSpec((tm, d), lambda i: (i, 0)),
                pl.BlockSpec((1, N_MOD, d), lambda i: (i // bps, 0, 0)),
                pl.BlockSpec((1, N_MOD, d), lambda i: (ctx_row, 0, 0)),
                pl.BlockSpec((1, d), const),
                pl.BlockSpec((None, None, d, 2 * ff), pick, pipeline_mode=pl.Buffered(1)),
                pl.BlockSpec((None, None, ff, d), pick, pipeline_mode=pl.Buffered(1))]
    args = [t, mods, mods, g.reshape(1, d), w_in, w_out]
    if final_g is not None:
        in_specs.append(pl.BlockSpec((1, d), const))
        args.append(final_g.reshape(1, d))
    return pl.pallas_call(
        functools.partial(_ffn_kernel, s0=s0, ff=ff, fc=fc, final=final_g is not None, tm=tm, n_ctx=n_ctx, bps=bps),
        grid=(m // tm,), in_specs=in_specs,
        out_specs=pl.BlockSpec((tm, d), lambda i: (i, 0)),
        out_shape=jax.ShapeDtypeStruct((m, d), f32),
        compiler_params=_cparams(1),
        name="ffn",
    )(*args)


def _mixin_kernel(x_ref, mb_ref, mc_ref, g_ref, w_ref, cos_ref, sin_ref, fr_ref, hc_ref, q_ref, k_ref, v_ref, *, tm,
                  n_ctx, bps):
    x = x_ref[...]
    mod = lambda idx: _mod_row(mb_ref, mc_ref, idx, tm, n_ctx, bps)
    h = _modulated_norm(x, g_ref[...], mod(3), mod(4)).astype(bf16)
    fr_ref[...] = _dot(h, w_ref[:, 0:RWKV_IN])
    o = RWKV_IN
    val = _dot(h, w_ref[:, o:o + D_CONV])
    gate = _dot(h, w_ref[:, o + D_CONV:o + CONV_IN])
    hc_ref[...] = val * _sigmoid(gate)
    o = RWKV_IN + CONV_IN
    cos = cos_ref[...]
    sin = sin_ref[...]
    lane = lax.broadcasted_iota(jnp.int32, cos.shape, 1)
    low_half = ((lane // ROPE_FREQS) % 2) == 0

    def rope(t):
        partner = jnp.where(low_half, pltpu.roll(t, D_QK - ROPE_FREQS, 1), pltpu.roll(t, ROPE_FREQS, 1))
        return t * cos + partner * sin

    q = _dot(h, w_ref[:, o:o + D_QK])
    q_ref[...] = (rope(q) * (DIFF_QK_DIM ** -0.5 * LOG2_E)).astype(bf16)
    k = _dot(h, w_ref[:, o + D_QK:o + 2 * D_QK])
    k_ref[...] = rope(k).astype(bf16)
    v_ref[...] = _dot(h, w_ref[:, o + 2 * D_QK:o + 2 * D_QK + D_DIFF]).astype(bf16)


def _mixin(t, mods, g, w, layer, cos, sin, seq, n_ctx, ctx_row):
    m, d = t.shape
    tm = _dense_tile(seq)
    bps = seq // tm
    const = lambda i: (0, 0)
    row = lambda i: (i, 0)
    return pl.pallas_call(
        functools.partial(_mixin_kernel, tm=tm, n_ctx=n_ctx, bps=bps),
        grid=(m // tm,),
        in_specs=[pl.BlockSpec((tm, d), row),
                  pl.BlockSpec((1, N_MOD, d), lambda i: (i // bps, 0, 0)),
                  pl.BlockSpec((1, N_MOD, d), lambda i: (ctx_row, 0, 0)),
                  pl.BlockSpec((1, d), const),
                  pl.BlockSpec((None, d, P_IN), lambda i: (layer, 0, 0), pipeline_mode=pl.Buffered(1)),
                  pl.BlockSpec((tm, D_QK), lambda i: (i % bps, 0)),
                  pl.BlockSpec((tm, D_QK), lambda i: (i % bps, 0))],
        out_specs=[pl.BlockSpec((tm, RWKV_IN), row), pl.BlockSpec((tm, D_CONV), row),
                   pl.BlockSpec((tm, D_QK), row), pl.BlockSpec((tm, D_QK), row), pl.BlockSpec((tm, D_DIFF), row)],
        out_shape=[jax.ShapeDtypeStruct((m, RWKV_IN), f32), jax.ShapeDtypeStruct((m, D_CONV), f32),
                   jax.ShapeDtypeStruct((m, D_QK), bf16), jax.ShapeDtypeStruct((m, D_QK), bf16),
                   jax.ShapeDtypeStruct((m, D_DIFF), bf16)],
        compiler_params=_cparams(1),
        name="mixin",
    )(t, mods, mods, g.reshape(1, d), w, cos, sin)


def _attn_kernel(q_ref, k_ref, v_ref, lam_ref, g_ref, o_ref, vaug_ref, *, lam_init, n_ctx, n_lat, tq, unroll,
                 with_ctx):
    lv = lam_ref[...]
    lam = (jnp.exp(jnp.sum(lv[0:1] * lv[1:2], keepdims=True)) - jnp.exp(jnp.sum(lv[2:3] * lv[3:4], keepdims=True))
           + lam_init)

    @pl.when(jnp.logical_and(pl.program_id(0) == 0, pl.program_id(1) == 0))
    def _():
        vaug_ref[:, DIFF_V_DIM:] = jnp.ones((vaug_ref.shape[0], DIFF_V_DIM), bf16)

    vaug_ref[:, :DIFF_V_DIM] = v_ref[...]

    def attend(q, k, v):
        parts = []
        for m in range(2):
            sl = slice(m * DIFF_QK_DIM, (m + 1) * DIFF_QK_DIM)
            s = _dot_nt(q[:, sl], k[:, sl])
            e = jnp.exp2(s - jnp.max(s, axis=-1, keepdims=True))
            pv = _dot(e.astype(bf16), v)
            parts.append(pv[:, :DIFF_V_DIM] * (1.0 / pv[:, DIFF_V_DIM:]))
        o = parts[0] - lam * parts[1]
        o = o * lax.rsqrt(jnp.mean(o * o, axis=-1, keepdims=True) + 1e-5) * g_ref[...]
        return o * (1.0 - lam_init)

    out_off = 0
    if with_ctx:
        o_ref[0:n_ctx, :] = attend(q_ref[0:n_ctx, :], k_ref[0:n_ctx, :], vaug_ref[0:n_ctx, :])
        out_off = n_ctx
    align = math.gcd(n_ctx, tq)

    def body(i, carry):
        for u in range(unroll):
            t = i * unroll + u
            q_rows = pl.ds(pl.multiple_of(n_ctx + t * tq, align), tq)
            o_rows = pl.ds(pl.multiple_of(out_off + t * tq, align), tq)
            o_ref[o_rows, :] = attend(q_ref[q_rows, :], k_ref[...], vaug_ref[...])
        return carry

    lax.fori_loop(0, n_lat // (tq * unroll), body, 0)


def _attn(q, k, v, lam_vecs, norm_g, lam_init, nbatch, n_ctx, n_lat, with_ctx):
    seq = n_ctx + n_lat
    tq = TM
    unroll = 4 if n_lat % (4 * tq) == 0 else 1
    out_seq = seq if with_ctx else n_lat
    blk = lambda b, h: (b, h)
    return pl.pallas_call(
        functools.partial(_attn_kernel, lam_init=lam_init, n_ctx=n_ctx, n_lat=n_lat, tq=tq, unroll=unroll,
                          with_ctx=with_ctx),
        grid=(nbatch, DIFF_HEADS),
        in_specs=[pl.BlockSpec((seq, DIFF_V_DIM), blk), pl.BlockSpec((seq, DIFF_V_DIM), blk),
                  pl.BlockSpec((seq, DIFF_V_DIM), blk),
                  pl.BlockSpec((4, DIFF_QK_DIM), lambda b, h: (0, 0)),
                  pl.BlockSpec((1, DIFF_V_DIM), lambda b, h: (0, 0))],
        out_specs=pl.BlockSpec((out_seq, DIFF_V_DIM), blk),
        out_shape=jax.ShapeDtypeStruct((nbatch * out_seq, D_DIFF), f32),
        scratch_shapes=[pltpu.VMEM((seq, 2 * DIFF_V_DIM), bf16)],
        compiler_params=_cparams(2),
        name="diffattn",
    )(q, k, v, lam_vecs, norm_g.reshape(1, DIFF_V_DIM))


def _segment_flags(i, nbc, nbs):
    j = i % nbs
    first = jnp.logical_or(j == 0, j == nbc)
    last = jnp.logical_or(j == nbc - 1, j == nbs - 1)
    return first, last


def _halo_specs(width, n_rows):
    per = TM // HALO
    nblk = n_rows // HALO
    prev = pl.BlockSpec((HALO, width), lambda i: (jnp.maximum(i * per - 1, 0), 0))
    nxt = pl.BlockSpec((HALO, width), lambda i: (jnp.minimum((i + 1) * per, nblk - 1), 0))
    return prev, nxt


def _conv_kernel(h_ref, hp_ref, hn_ref, w_ref, b_ref, g_ref, bb_ref, o_ref, buf, shf, *, nbc, nbs):
    first, last = _segment_flags(pl.program_id(0), nbc, nbs)
    buf[0:HALO, :] = jnp.where(first, 0.0, hp_ref[...])
    buf[HALO:HALO + TM, :] = h_ref[...]
    buf[HALO + TM:HALO + TM + HALO, :] = jnp.where(last, 0.0, hn_ref[...])
    sub = 64
    sublanes = 8
    first_tap = HALO - CONV_HALF
    for r0 in range(0, TM, sub):
        acc = jnp.zeros((sub, D_CONV), f32) + b_ref[...]
        for r in range(sublanes):
            taps = [kk for kk in range(CONV_WIDTH) if (first_tap + kk) % sublanes == r]
            if not taps:
                continue
            span = sub + sublanes * max((first_tap + kk) // sublanes for kk in taps)
            shf[0:span, :] = buf[r0 + r:r0 + r + span, :]
            for kk in taps:
                a = sublanes * ((first_tap + kk) // sublanes)
                acc = acc + shf[a:a + sub, :] * w_ref[kk:kk + 1, :]
        mu = jnp.mean(acc, axis=-1, keepdims=True)
        xc = acc - mu
        var = jnp.mean(xc * xc, axis=-1, keepdims=True)
        y = xc * lax.rsqrt(var + 1e-5) * g_ref[...] + bb_ref[...]
        o_ref[r0:r0 + sub, :] = y * _sigmoid(y)


def _conv(hc, dw_w, dw_b, ln_g, ln_b, nbc, nbs):
    m = hc.shape[0]
    prev, nxt = _halo_specs(D_CONV, m)
    const = lambda i: (0, 0)
    return pl.pallas_call(
        functools.partial(_conv_kernel, nbc=nbc, nbs=nbs),
        grid=(m // TM,),
        in_specs=[pl.BlockSpec((TM, D_CONV), lambda i: (i, 0)), prev, nxt,
                  pl.BlockSpec((CONV_WIDTH, D_CONV), const), pl.BlockSpec((1, D_CONV), const),
                  pl.BlockSpec((1, D_CONV), const), pl.BlockSpec((1, D_CONV), const)],
        out_specs=pl.BlockSpec((TM, D_CONV), lambda i: (i, 0)),
        out_shape=jax.ShapeDtypeStruct((m, D_CONV), f32),
        scratch_shapes=[pltpu.VMEM((TM + 2 * HALO, D_CONV), f32), pltpu.VMEM((64 + 2 * HALO, D_CONV), f32)],
        compiler_params=_cparams(1),
        name="convmod",
    )(hc, hc, hc, dw_w, dw_b.reshape(1, D_CONV), ln_g.reshape(1, D_CONV), ln_b.reshape(1, D_CONV))


def _rwkv_chunk_kernel(f_ref, fp_ref, fn_ref, mu_ref, w0_ref, w2_ref, a0_ref, a2_ref, g2_ref, kk_ref, ka_ref, rk_ref,
                       rp_ref, y0_ref, mm_ref, nn_ref, bv_ref, gg_ref,
                       s_r, s_v, s_a, s_lw, s_k, s_b, *, nbc, nbs):
    first, last = _segment_flags(pl.program_id(0), nbc, nbs)
    f = f_ref[...]
    prow = jnp.where(first, 0.0, fp_ref[HALO - 1:HALO, :])
    nrow = jnp.where(last, 0.0, fn_ref[0:1, :])
    rows = lax.broadcasted_iota(jnp.int32, f.shape, 0)
    prev = jnp.where(rows == 0, prow, pltpu.roll(f, 1, 0))
    nxt = jnp.where(rows == TM - 1, nrow, pltpu.roll(f, TM - 1, 0))
    fs = f + mu_ref[0:1, :] * (prev - f) + mu_ref[1:2, :] * (nxt - f)

    r = fs[:, 0:D_RWKV]
    k = fs[:, D_RWKV:2 * D_RWKV]
    v = fs[:, 2 * D_RWKV:3 * D_RWKV]
    o_w = 3 * D_RWKV
    o_a = o_w + 2 * DECAY_LORA
    o_g = o_a + 2 * AAA_LORA
    ones_bd = _head_ones(D_RWKV, HEAD_DIM)
    gg_ref[...] = _dot(_sigmoid(fs[:, o_g:o_g + GATE_LORA]).astype(bf16), g2_ref[...].astype(bf16))
    kk = k * kk_ref[...]
    kk = kk * lax.rsqrt(_dot_exact_rhs(kk * kk, ones_bd) + NORM_EPS)
    ksum = None
    for d in range(2):
        wd = fs[:, o_w + d * DECAY_LORA:o_w + (d + 1) * DECAY_LORA]
        ad = fs[:, o_a + d * AAA_LORA:o_a + (d + 1) * AAA_LORA]
        w_raw = w0_ref[d:d + 1, :] + _dot(jnp.tanh(wd).astype(bf16), w2_ref[d].astype(bf16))
        z = -w_raw
        softplus = jnp.maximum(z, 0.0) + jnp.log(1.0 + jnp.exp(-jnp.abs(z)))
        s_lw[d] = -jnp.exp(-softplus - 0.5)
        a = _sigmoid(a0_ref[d:d + 1, :] + _dot(ad.astype(bf16), a2_ref[d].astype(bf16)))
        kd = k * (1.0 + (a - 1.0) * ka_ref[...])
        ksum = kd if d == 0 else ksum + kd
        s_k[d] = kd
        s_b[d] = kk * a
    s_r[...] = r
    s_v[...] = v
    s_a[...] = -kk
    bv_ref[...] = _dot_exact_rhs(r * ksum * rk_ref[...], ones_bd) * v

    ri = lax.broadcasted_iota(jnp.int32, (CHUNK, CHUNK), 0)
    ci = lax.broadcasted_iota(jnp.int32, (CHUNK, CHUNK), 1)
    tri = tuple(jnp.where(msk, 1.0, 0.0).astype(bf16) for msk in (ci <= ri, ci >= ri))
    rp_i = lax.broadcasted_iota(jnp.int32, (CHUNK, PAIR), 0)
    cp_i = lax.broadcasted_iota(jnp.int32, (CHUNK, PAIR), 1) % CHUNK
    eye = jnp.where(rp_i == cp_i, 1.0, 0.0).astype(f32)
    strict = (cp_i < rp_i, cp_i > rp_i)
    incl = (cp_i <= rp_i, cp_i >= rp_i)
    low = lax.broadcasted_iota(jnp.int32, (CHUNK, PAIR), 1) < HEAD_DIM

    def chunk_body(ci, carry):
        cs = [ci * CHUNKS_PER_BODY + u for u in range(CHUNKS_PER_BODY)]
        rss = [pl.ds(pl.multiple_of(c * CHUNK, CHUNK), CHUNK) for c in cs]
        v16_c, at16, rt, bt16, kt16, wc = [], {}, {}, {}, {}, {}
        for u, rs in enumerate(rss):
            r_c = s_r[rs, :]
            a_c = s_a[rs, :]
            v16_c.append(s_v[rs, :].astype(bf16))
            for d in range(2):
                lw = s_lw[d, rs, :]
                cum = _dot_exact_lhs(tri[d], lw)
                er = jnp.exp(cum)
                ei = jnp.exp(-cum)
                at16[u, d] = (a_c * jnp.exp(cum - lw)).astype(bf16)
                rt[u, d] = r_c * er
                bt16[u, d] = (s_b[d, rs, :] * ei).astype(bf16)
                kt16[u, d] = (s_k[d, rs, :] * ei).astype(bf16)
                wc[u, d] = er[CHUNK - 1:CHUNK, :] if d == 0 else er[0:1, :]

        insts = [(u, d, p) for u in range(CHUNKS_PER_BODY) for d in range(2) for p in range(N_PAIR)]
        psl = lambda p: slice(p * PAIR, (p + 1) * PAIR)
        gms = [_dot_nt(jnp.concatenate([at16[u, d][:, psl(p)], rt[u, d][:, psl(p)].astype(bf16)], axis=0),
                       jnp.concatenate([_pair_blockdiag(bt16[u, d][:, psl(p)]),
                                        _pair_blockdiag(kt16[u, d][:, psl(p)])], axis=0)) for u, d, p in insts]
        lab = [jnp.where(strict[d], g[:CHUNK, :PAIR], 0.0) for (u, d, p), g in zip(insts, gms)]
        lakrk16 = [jnp.concatenate([jnp.where(strict[d], g[:CHUNK, PAIR:], 0.0),
                                    jnp.where(incl[d], g[CHUNK:, PAIR:], 0.0)], axis=0).astype(bf16)
                   for (u, d, p), g in zip(insts, gms)]
        lrb16 = [jnp.where(incl[d], g[CHUNK:, :PAIR], 0.0).astype(bf16) for (u, d, p), g in zip(insts, gms)]
        uv = [_dot(lk, _pair_blockdiag(v16_c[u][:, psl(p)])) for (u, d, p), lk in zip(insts, lakrk16)]
        vtk = [_dot_tn(v16_c[u][:, psl(p)], kt16[u, d][:, psl(p)]) for u, d, p in insts]
        def split(t):
            hi = t.astype(bf16)
            return hi, (t - hi.astype(f32)).astype(bf16)

        def mul_x3(a, b):
            a_hi, a_lo = split(a)
            b_hi, b_lo = split(b)
            w = b.shape[1]
            full = _dot(a_hi, _pair_blockdiag(jnp.concatenate([b_hi, b_lo], axis=1)))
            return full[:, :w] + full[:, w:] + _dot(a_lo, _pair_blockdiag(b_hi))

        n_stage = int(math.log2(CHUNK))
        x = [eye + l for l in lab]
        q = [mul_x3(l, l) for l in lab]
        for stage in range(1, n_stage):
            if stage < n_stage - 1:
                prod = [mul_x3(jnp.concatenate([xx, qq], axis=0), qq) for xx, qq in zip(x, q)]
                x = [xx + pr[:CHUNK] for xx, pr in zip(x, prod)]
                q = [pr[CHUNK:] for pr in prod]
            else:
                x = [xx + mul_x3(xx, qq) for xx, qq in zip(x, q)]
        z0 = [jnp.concatenate([at16[u, d][:, psl(p)].astype(f32), lv[:CHUNK]], axis=1)
              for (u, d, p), lv in zip(insts, uv)]
        au16 = [mul_x3(xx, zz).astype(bf16) for xx, zz in zip(x, z0)]
        ry = [_dot(lr, _pair_blockdiag(au)) for lr, au in zip(lrb16, au16)]
        mn = [_dot_tn(au, bt16[u, d][:, psl(p)]) for (u, d, p), au in zip(insts, au16)]
        for i, (u, d, p) in enumerate(insts):
            ps = psl(p)
            rp_ref[d, rss[u], ps] = rt[u, d][:, ps] + ry[i][:, :PAIR]
            y0_ref[d, rss[u], ps] = ry[i][:, PAIR:] + uv[i][CHUNK:]
            wc_p = wc[u, d][:, ps]
            m_pair = jnp.where(low, mn[i][0:CHUNK], mn[i][CHUNK:2 * CHUNK])
            n_pair = jnp.where(low, mn[i][2 * CHUNK:3 * CHUNK], mn[i][3 * CHUNK:])
            k_pair = jnp.where(low, vtk[i][:CHUNK], vtk[i][CHUNK:])
            mm_ref[d, cs[u], p] = (eye + m_pair) * wc_p
            nn_ref[d, cs[u], p] = (n_pair + k_pair) * wc_p
        return carry

    lax.fori_loop(0, TM // (CHUNK * CHUNKS_PER_BODY), chunk_body, 0)


def _rwkv_chunks(fr, mu, w0, w2, a0, a2, g2, kk, ka, rk, nbc, nbs):
    m = fr.shape[0]
    nct = TM // CHUNK
    prev, nxt = _halo_specs(RWKV_IN, m)
    c2 = lambda i: (0, 0)
    c3 = lambda i: (0, 0, 0)
    row = lambda i: (i, 0)
    dirrow = lambda i: (0, i, 0)
    ops = lambda i: (0, i, 0, 0, 0)
    return pl.pallas_call(
        functools.partial(_rwkv_chunk_kernel, nbc=nbc, nbs=nbs),
        grid=(m // TM,),
        in_specs=[pl.BlockSpec((TM, RWKV_IN), row), prev, nxt,
                  pl.BlockSpec((2, RWKV_IN), c2), pl.BlockSpec((2, D_RWKV), c2),
                  pl.BlockSpec((2, DECAY_LORA, D_RWKV), c3), pl.BlockSpec((2, D_RWKV), c2),
                  pl.BlockSpec((2, AAA_LORA, D_RWKV), c3), pl.BlockSpec((GATE_LORA, D_RWKV), c2),
                  pl.BlockSpec((1, D_RWKV), c2), pl.BlockSpec((1, D_RWKV), c2), pl.BlockSpec((1, D_RWKV), c2)],
        out_specs=[pl.BlockSpec((2, TM, D_RWKV), dirrow), pl.BlockSpec((2, TM, D_RWKV), dirrow),
                   pl.BlockSpec((2, nct, N_PAIR, HEAD_DIM, PAIR), ops),
                   pl.BlockSpec((2, nct, N_PAIR, HEAD_DIM, PAIR), ops),
                   pl.BlockSpec((TM, D_RWKV), row), pl.BlockSpec((TM, D_RWKV), row)],
        out_shape=[jax.ShapeDtypeStruct((2, m, D_RWKV), f32), jax.ShapeDtypeStruct((2, m, D_RWKV), f32),
                   jax.ShapeDtypeStruct((2, m // CHUNK, N_PAIR, HEAD_DIM, PAIR), f32),
                   jax.ShapeDtypeStruct((2, m // CHUNK, N_PAIR, HEAD_DIM, PAIR), f32),
                   jax.ShapeDtypeStruct((m, D_RWKV), f32), jax.ShapeDtypeStruct((m, D_RWKV), f32)],
        scratch_shapes=[pltpu.VMEM((TM, D_RWKV), f32), pltpu.VMEM((TM, D_RWKV), f32), pltpu.VMEM((TM, D_RWKV), f32),
                        pltpu.VMEM((2, TM, D_RWKV), f32), pltpu.VMEM((2, TM, D_RWKV), f32),
                        pltpu.VMEM((2, TM, D_RWKV), f32)],
        compiler_params=_cparams(1),
        name="rwkv_chunks",
    )(fr, fr, fr, mu, w0, w2, a0, a2, g2, kk.reshape(1, D_RWKV), ka.reshape(1, D_RWKV), rk.reshape(1, D_RWKV))


def _rwkv_state_kernel(mf_ref, mb_ref, nf_ref, nb_ref, sf_ref, sb_ref, st_ref, *, nbatch):
    @pl.when(pl.program_id(0) == 0)
    def _():
        st_ref[...] = jnp.zeros(st_ref.shape, f32)

    group = 8
    probs = [(d, b, p) for b in range(nbatch) for d in range(2) for p in range(N_PAIR)]
    for g0 in range(0, len(probs), group):
        grp = probs[g0:g0 + group]
        st = [st_ref[d, b, p] for d, b, p in grp]
        for (d, b, p), s in zip(grp, st):
            (sf_ref if d == 0 else sb_ref)[b, 0, p] = s
        mt = [(mf_ref if d == 0 else mb_ref)[0, b, 0, p] for d, b, p in grp]
        s1 = [s.astype(bf16) for s in st]
        s2 = [(s - a.astype(f32)).astype(bf16) for s, a in zip(st, s1)]
        m1 = [x.astype(bf16) for x in mt]
        m2 = [(x - a.astype(f32)).astype(bf16) for x, a in zip(mt, m1)]
        p12 = [_dot(a1, _pair_blockdiag(jnp.concatenate([b1, b2], axis=1))) for a1, b1, b2 in zip(s1, m1, m2)]
        p21 = [_dot(a2, _pair_blockdiag(b1)) for a2, b1 in zip(s2, m1)]
        for (d, b, p), x12, x21 in zip(grp, p12, p21):
            st_ref[d, b, p] = (x12[:, :PAIR] + x12[:, PAIR:] + x21) + (nf_ref if d == 0 else nb_ref)[0, b, 0, p]


def _rwkv_state(mm, nn, nbatch, nca, ncc):
    shp = (2, nbatch, nca, N_PAIR, HEAD_DIM, PAIR)
    blk = (1, nbatch, 1, N_PAIR, HEAD_DIM, PAIR)
    c_bwd = lambda s: jnp.where(s < ncc, ncc - 1 - s, nca - 1 - (s - ncc))
    fwd = pl.BlockSpec(blk, lambda s: (0, 0, s, 0, 0, 0))
    bwd = pl.BlockSpec(blk, lambda s: (1, 0, c_bwd(s), 0, 0, 0))
    oshape = jax.ShapeDtypeStruct(shp[1:], f32)
    return pl.pallas_call(
        functools.partial(_rwkv_state_kernel, nbatch=nbatch),
        grid=(nca,),
        in_specs=[fwd, bwd, fwd, bwd],
        out_specs=[pl.BlockSpec(blk[1:], lambda s: (0, s, 0, 0, 0)),
                   pl.BlockSpec(blk[1:], lambda s: (0, c_bwd(s), 0, 0, 0))],
        out_shape=[oshape, oshape],
        scratch_shapes=[pltpu.VMEM((2, nbatch, N_PAIR, HEAD_DIM, PAIR), f32)],
        compiler_params=_cparams(1),
        name="rwkv_state",
    )(mm.reshape(shp), mm.reshape(shp), nn.reshape(shp), nn.reshape(shp))


def _rwkv_readout(rp_ref, y0_ref, sf_ref, sb_ref, bv_ref, gg_ref, lng_ref, lnb_ref):
    rows = []
    for c in range(TM // CHUNK):
        rs = slice(c * CHUNK, (c + 1) * CHUNK)
        cols = []
        for p in range(N_PAIR):
            ps = slice(p * PAIR, (p + 1) * PAIR)
            lhs = jnp.concatenate([rp_ref[0, rs, ps], rp_ref[1, rs, ps]], axis=1).astype(bf16)
            rhs = jnp.concatenate([_pair_blockdiag(sf_ref[c, p].astype(bf16)),
                                   _pair_blockdiag(sb_ref[c, p].astype(bf16))], axis=1)
            cols.append(y0_ref[0, rs, ps] + y0_ref[1, rs, ps] + _dot_nt(lhs, rhs))
        rows.append(jnp.concatenate(cols, axis=1))
    y = jnp.concatenate(rows, axis=0)
    ones_bd = _head_ones(D_RWKV, HEAD_DIM)
    mu = _dot_exact_rhs(y, ones_bd) * (1.0 / HEAD_DIM)
    yc = y - mu
    var = _dot_exact_rhs(yc * yc, ones_bd) * (1.0 / HEAD_DIM)
    yn = yc * lax.rsqrt(var + GN_EPS) * lng_ref[...] + lnb_ref[...]
    return (yn + bv_ref[...]) * gg_ref[...]


def _mixout_kernel(t_ref, rp_ref, y0_ref, sf_ref, sb_ref, bv_ref, gg_ref, lng_ref, lnb_ref, b_ref, c_ref, w_ref, m_ref,
                   o_ref):
    a = _rwkv_readout(rp_ref, y0_ref, sf_ref, sb_ref, bv_ref, gg_ref, lng_ref, lnb_ref)
    out = _dot(a.astype(bf16), w_ref[0:D_RWKV, :])
    out = out + _dot(b_ref[...].astype(bf16), w_ref[D_RWKV:D_RWKV + D_CONV, :])
    out = out + _dot(c_ref[...].astype(bf16), w_ref[D_RWKV + D_CONV:, :])
    o_ref[...] = t_ref[...] + m_ref[0, 5:6, :] * out


def _mixout(t, rwkv_parts, ln_g, ln_b, ob, oc, w, layer, mods, grid, row_map, mod_map, c_map, out_rows):
    rp, y0, sf, sb, bv, gg = rwkv_parts
    d = t.shape[1]
    nct = TM // CHUNK
    sshape = (bv.shape[0] // CHUNK, N_PAIR, HEAD_DIM, PAIR)
    n_axes = len(grid)
    if n_axes == 1:
        out_map = lambda i: (i, 0)
    else:
        nj = grid[1]
        out_map = lambda b, j: (b * nj + j, 0)
    rmap = lambda *a: (row_map(*a), 0)
    dirrow = lambda *a: (0, row_map(*a), 0)
    st = lambda *a: (row_map(*a), 0, 0, 0)
    c2 = lambda *a: (0, 0)
    return pl.pallas_call(
        _mixout_kernel,
        grid=grid,
        in_specs=[pl.BlockSpec((TM, d), rmap),
                  pl.BlockSpec((2, TM, D_RWKV), dirrow), pl.BlockSpec((2, TM, D_RWKV), dirrow),
                  pl.BlockSpec((nct, N_PAIR, HEAD_DIM, PAIR), st), pl.BlockSpec((nct, N_PAIR, HEAD_DIM, PAIR), st),
                  pl.BlockSpec((TM, D_RWKV), rmap), pl.BlockSpec((TM, D_RWKV), rmap),
                  pl.BlockSpec((1, D_RWKV), c2), pl.BlockSpec((1, D_RWKV), c2),
                  pl.BlockSpec((TM, D_CONV), rmap),
                  pl.BlockSpec((TM, D_DIFF), lambda *a: (c_map(*a), 0)),
                  pl.BlockSpec((None, D_RWKV + D_CONV + D_DIFF, d), lambda *a: (layer, 0, 0)),
                  pl.BlockSpec((1, N_MOD, d), lambda *a: (mod_map(*a), 0, 0))],
        out_specs=pl.BlockSpec((TM, d), out_map),
        out_shape=jax.ShapeDtypeStruct((out_rows, d), f32),
        compiler_params=_cparams(n_axes),
        name="mixout",
    )(t, rp, y0, sf.reshape(sshape), sb.reshape(sshape), bv, gg, ln_g.reshape(1, D_RWKV), ln_b.reshape(1, D_RWKV),
      ob, oc, w, mods)


def _rope_tables(n_ctx, n_lat):
    n_rows = n_lat // GRID_W
    row = jnp.repeat(jnp.arange(n_rows, dtype=jnp.int32), GRID_W)
    col = jnp.tile(jnp.arange(GRID_W, dtype=jnp.int32), n_rows)
    inv = 1.0 / (ROPE_THETA ** (jnp.arange(ROPE_FREQS, dtype=f32) * 2.0 / AXIS_DIM))
    ang = jnp.stack([row, col], axis=-1).astype(f32)[..., None] * inv
    cos, sin = jnp.cos(ang), jnp.sin(ang)
    cos64 = jnp.concatenate([cos[:, 0], cos[:, 0], cos[:, 1], cos[:, 1]], axis=-1)
    sin64 = jnp.concatenate([-sin[:, 0], sin[:, 0], -sin[:, 1], sin[:, 1]], axis=-1)
    reps = D_QK // DIFF_QK_DIM
    cos_t = jnp.concatenate([jnp.ones((n_ctx, D_QK), f32), jnp.tile(cos64, (1, reps))], axis=0)
    sin_t = jnp.concatenate([jnp.zeros((n_ctx, D_QK), f32), jnp.tile(sin64, (1, reps))], axis=0)
    return cos_t, sin_t


def kernel(x, c, ctx, c_ctx, ada_w, ada_b, norm_g, ffn_w_in, ffn_w_out, mix_w_in, mix_w_out, rwkv_mu, rwkv_w0, rwkv_w2,
           rwkv_a0, rwkv_a2, rwkv_g2, rwkv_kk, rwkv_ka, rwkv_rk, rwkv_ln_g, rwkv_ln_b, conv_dw_w, conv_dw_b, conv_ln_g,
           conv_ln_b, diff_lam, diff_norm_g, final_g):
    nb, n_lat, d = x.shape
    n_ctx = ctx.shape[1]
    depth = ada_w.shape[0]
    seq = n_ctx + n_lat
    assert n_ctx % TM == 0 and n_lat % TM == 0 and n_lat % GRID_W == 0 and seq % n_ctx == 0
    assert nb + 1 <= COND_ROWS
    nbc, nbx, nbs = n_ctx // TM, n_lat // TM, seq // TM
    m = nb * seq

    t = jnp.concatenate([ctx, x], axis=1).reshape(m, d)
    cond = jnp.zeros((COND_ROWS, d), f32).at[:nb].set(c).at[nb].set(c_ctx)
    mods = _adaln(cond, ada_w, ada_b).reshape(depth, COND_ROWS, N_MOD, d)
    cos_t, sin_t = _rope_tables(n_ctx, n_lat)
    w_in16 = ffn_w_in.astype(bf16)
    w_out16 = ffn_w_out.astype(bf16)
    mix_in16 = mix_w_in.astype(bf16)
    mix_out16 = mix_w_out.astype(bf16)

    mod_all = lambda i: jnp.where(i % nbs < nbc, nb, i // nbs)
    row_all = lambda i: i
    row_lat = lambda b, j: b * nbs + nbc + j
    mod_lat = lambda b, j: b

    out = None
    for l in range(depth):
        last = l == depth - 1
        ml = mods[l]
        t = _ffn(t, ml, norm_g[l, 0], w_in16, w_out16, l, 0, 0, seq, n_ctx, nb)
        fr, hc, q, k, v = _mixin(t, ml, norm_g[l, 1], mix_in16, l, cos_t, sin_t, seq, n_ctx, nb)
        rp, y0, mm, nn, bv, gg = _rwkv_chunks(fr, rwkv_mu[l], rwkv_w0[l], rwkv_w2[l], rwkv_a0[l], rwkv_a2[l], rwkv_g2[l],
                                              rwkv_kk[l], rwkv_ka[l], rwkv_rk[l].reshape(D_RWKV), nbc, nbs)
        sf, sb = _rwkv_state(mm, nn, nb, seq // CHUNK, n_ctx // CHUNK)
        rwkv_parts = (rp, y0, sf, sb, bv, gg)
        ob = _conv(hc, conv_dw_w[l], conv_dw_b[l], conv_ln_g[l], conv_ln_b[l], nbc, nbs)
        lam_init = 0.8 - 0.6 * math.exp(-0.3 * l)
        if last:
            oc = _attn(q, k, v, diff_lam[l], diff_norm_g[l], lam_init, nb, n_ctx, n_lat, False)
            t = _mixout(t, rwkv_parts, rwkv_ln_g[l], rwkv_ln_b[l], ob, oc, mix_out16, l, ml, (nb, nbx), row_lat,
                        mod_lat, lambda b, j: b * nbx + j, nb * n_lat)
            out = _ffn(t, ml, norm_g[l, 2], w_in16, w_out16, l, 1, 6, n_lat, 0, nb, final_g=final_g)
        else:
            oc = _attn(q, k, v, diff_lam[l], diff_norm_g[l], lam_init, nb, n_ctx, n_lat, True)
            t = _mixout(t, rwkv_parts, rwkv_ln_g[l], rwkv_ln_b[l], ob, oc, mix_out16, l, ml, (nbs * nb,), row_all,
                        mod_all, row_all, m)
            t = _ffn(t, ml, norm_g[l, 2], w_in16, w_out16, l, 1, 6, seq, n_ctx, nb)
    return out.reshape(nb, n_lat, d)
```

```python
import functools
import math

import jax
import jax.numpy as jnp
from jax import lax
from jax.experimental import pallas as pl
from jax.experimental.pallas import tpu as pltpu

f32 = jnp.float32
bf16 = jnp.bfloat16

N_MOD = 9
EPS = 1e-6
GRID_W = 64
RWKV_HEADS = 4
HEAD_DIM = 64
D_RWKV = RWKV_HEADS * HEAD_DIM
PAIR = 2 * HEAD_DIM
N_PAIR = RWKV_HEADS // 2
DECAY_LORA = 64
AAA_LORA = 64
GATE_LORA = 128
GN_EPS = 64e-5
NORM_EPS = 1e-12
D_CONV = 256
CONV_WIDTH = 31
CONV_HALF = CONV_WIDTH // 2
DIFF_HEADS = 4
DIFF_QK_DIM = 64
DIFF_V_DIM = 2 * DIFF_QK_DIM
D_DIFF = DIFF_HEADS * DIFF_V_DIM
ROPE_THETA = 10000.0
LOG2_E = 1.4426950408889634
AXIS_DIM = DIFF_QK_DIM // 2
ROPE_FREQS = AXIS_DIM // 2
RWKV_IN = 3 * D_RWKV + 2 * DECAY_LORA + 2 * AAA_LORA + GATE_LORA
CONV_IN = 2 * D_CONV
D_QK = DIFF_HEADS * 2 * DIFF_QK_DIM
DIFF_IN = 2 * D_QK + D_DIFF
P_IN = RWKV_IN + CONV_IN + DIFF_IN

TM = 256
CHUNK = 64
CHUNKS_PER_BODY = 4
HALO = 16
COND_ROWS = 16
VMEM_LIMIT = 56 * 1024 * 1024


def _dot(a, b, prec=None):
    return jnp.dot(a, b, preferred_element_type=f32, precision=prec)


def _dot_nt(a, b, prec=None):
    return lax.dot_general(a, b, (((1,), (1,)), ((), ())), preferred_element_type=f32, precision=prec)


def _dot_tn(a, b, prec=None):
    return lax.dot_general(a, b, (((0,), (0,)), ((), ())), preferred_element_type=f32, precision=prec)


def _split3(a):
    a1 = a.astype(bf16)
    r1 = a - a1.astype(f32)
    a2 = r1.astype(bf16)
    a3 = (r1 - a2.astype(f32)).astype(bf16)
    return a1, a2, a3


def _dot_exact_rhs(a, b16):
    a1, a2, a3 = _split3(a)
    return _dot(a1, b16) + _dot(a2, b16) + _dot(a3, b16)


def _dot_exact_lhs(a16, b):
    b1, b2, b3 = _split3(b)
    return _dot(a16, b1) + _dot(a16, b2) + _dot(a16, b3)


def _pair_blockdiag(x):
    first = (lax.broadcasted_iota(jnp.int32, x.shape, 1) % PAIR) < HEAD_DIM
    zero = jnp.zeros_like(x)
    return jnp.concatenate([jnp.where(first, x, zero), jnp.where(first, zero, x)], axis=0)


def _sigmoid(x):
    return 1.0 / (1.0 + jnp.exp(-x))


def _cparams(n_axes):
    return pltpu.CompilerParams(dimension_semantics=("arbitrary",) * n_axes, vmem_limit_bytes=VMEM_LIMIT)


def _head_ones(n, width):
    r = lax.broadcasted_iota(jnp.int32, (n, n), 0) // width
    c = lax.broadcasted_iota(jnp.int32, (n, n), 1) // width
    return jnp.where(r == c, 1.0, 0.0).astype(bf16)


def _adaln_kernel(c_ref, w_ref, b_ref, o_ref):
    cond = c_ref[...]
    cond = cond * _sigmoid(cond)
    o_ref[0] = _dot(cond.astype(bf16), w_ref[0].astype(bf16)) + b_ref[0]


def _adaln(cond, ada_w, ada_b):
    depth, d, nd = ada_w.shape
    tn = nd // 4
    return pl.pallas_call(
        _adaln_kernel,
        grid=(depth, nd // tn),
        in_specs=[pl.BlockSpec((COND_ROWS, d), lambda l, j: (0, 0)),
                  pl.BlockSpec((1, d, tn), lambda l, j: (l, 0, j)),
                  pl.BlockSpec((1, 1, tn), lambda l, j: (l, 0, j))],
        out_specs=pl.BlockSpec((1, COND_ROWS, tn), lambda l, j: (l, 0, j)),
        out_shape=jax.ShapeDtypeStruct((depth, COND_ROWS, nd), f32),
        compiler_params=_cparams(2),
        name="adaln",
    )(cond, ada_w, ada_b.reshape(depth, 1, nd))


def _modulated_norm(x, g, shift, scale):
    y = x * lax.rsqrt(jnp.mean(x * x, axis=-1, keepdims=True) + EPS) * g
    return y * (1.0 + scale) + shift


def _dense_tile(rows_per_seq):
    for tm in (1024, 768, 512, 256):
        if rows_per_seq % tm == 0:
            return tm
    raise ValueError(rows_per_seq)


def _mod_row(mb_ref, mc_ref, idx, tm, n_ctx, blocks_per_seq):
    if n_ctx == 0:
        return mb_ref[0, idx:idx + 1, :]
    row = (pl.program_id(0) % blocks_per_seq) * tm + lax.broadcasted_iota(jnp.int32, (tm, 1), 0)
    return jnp.where(row < n_ctx, mc_ref[0, idx:idx + 1, :], mb_ref[0, idx:idx + 1, :])


def _ffn_kernel(x_ref, mb_ref, mc_ref, g_ref, wi_ref, wo_ref, *rest, s0, ff, fc, final, tm, n_ctx, bps):
    o_ref = rest[-1]
    x = x_ref[...]
    mod = lambda idx: _mod_row(mb_ref, mc_ref, idx, tm, n_ctx, bps)
    h = _modulated_norm(x, g_ref[...], mod(s0), mod(s0 + 1)).astype(bf16)
    acc = jnp.zeros(x.shape, f32)
    for j in range(ff // fc):
        gt = _dot(h, wi_ref[:, j * fc:(j + 1) * fc])
        up = _dot(h, wi_ref[:, ff + j * fc:ff + (j + 1) * fc])
        act = (gt * _sigmoid(gt) * up).astype(bf16)
        acc = acc + _dot(act, wo_ref[j * fc:(j + 1) * fc, :])
    out = x + 0.5 * mod(s0 + 2) * acc
    if final:
        fg_ref = rest[0]
        out = out * lax.rsqrt(jnp.mean(out * out, axis=-1, keepdims=True) + EPS) * fg_ref[...]
    o_ref[...] = out


def _ffn(t, mods, g, w_in, w_out, layer, slot, s0, rows_per_seq, n_ctx, ctx_row, final_g=None):
    m, d = t.shape
    ff = w_out.shape[2]
    fc = 256 if ff % 256 == 0 else ff
    tm = _dense_tile(rows_per_seq)
    bps = rows_per_seq // tm
    const = lambda i: (0, 0)
    pick = lambda i: (layer, slot, 0, 0)
    in_specs = [pl.BlockSpec((tm, d), lambda i: (i, 0)),
                pl.BlockSpec((1, N_MOD, d), lambda i: (i // bps, 0, 0)),
                pl.BlockSpec((1, N_MOD, d), lambda i: (ctx_row, 0, 0)),
                pl.BlockSpec((1, d), const),
                pl.BlockSpec((None, None, d, 2 * ff), pick, pipeline_mode=pl.Buffered(1)),
                pl.BlockSpec((None, None, ff, d), pick, pipeline_mode=pl.Buffered(1))]
    args = [t, mods, mods, g.reshape(1, d), w_in, w_out]
    if final_g is not None:
        in_specs.append(pl.BlockSpec((1, d), const))
        args.append(final_g.reshape(1, d))
    return pl.pallas_call(
        functools.partial(_ffn_kernel, s0=s0, ff=ff, fc=fc, final=final_g is not None, tm=tm, n_ctx=n_ctx, bps=bps),
        grid=(m // tm,), in_specs=in_specs,
        out_specs=pl.BlockSpec((tm, d), lambda i: (i, 0)),
        out_shape=jax.ShapeDtypeStruct((m, d), f32),
        compiler_params=_cparams(1),
        name="ffn",
    )(*args)


def _mixin_kernel(x_ref, mb_ref, mc_ref, g_ref, w_ref, cos_ref, sin_ref, fr_ref, hc_ref, q_ref, k_ref, v_ref, *, tm,
                  n_ctx, bps):
    x = x_ref[...]
    mod = lambda idx: _mod_row(mb_ref, mc_ref, idx, tm, n_ctx, bps)
    h = _modulated_norm(x, g_ref[...], mod(3), mod(4)).astype(bf16)
    fr_ref[...] = _dot(h, w_ref[:, 0:RWKV_IN])
    o = RWKV_IN
    val = _dot(h, w_ref[:, o:o + D_CONV])
    gate = _dot(h, w_ref[:, o + D_CONV:o + CONV_IN])
    hc_ref[...] = val * _sigmoid(gate)
    o = RWKV_IN + CONV_IN
    cos = cos_ref[...]
    sin = sin_ref[...]
    lane = lax.broadcasted_iota(jnp.int32, cos.shape, 1)
    low_half = ((lane // ROPE_FREQS) % 2) == 0

    def rope(t):
        partner = jnp.where(low_half, pltpu.roll(t, D_QK - ROPE_FREQS, 1), pltpu.roll(t, ROPE_FREQS, 1))
        return t * cos + partner * sin

    q = _dot(h, w_ref[:, o:o + D_QK])
    q_ref[...] = (rope(q) * (DIFF_QK_DIM ** -0.5 * LOG2_E)).astype(bf16)
    k = _dot(h, w_ref[:, o + D_QK:o + 2 * D_QK])
    k_ref[...] = rope(k).astype(bf16)
    v_ref[...] = _dot(h, w_ref[:, o + 2 * D_QK:o + 2 * D_QK + D_DIFF]).astype(bf16)


def _mixin(t, mods, g, w, layer, cos, sin, seq, n_ctx, ctx_row):
    m, d = t.shape
    tm = _dense_tile(seq)
    bps = seq // tm
    const = lambda i: (0, 0)
    row = lambda i: (i, 0)
    return pl.pallas_call(
        functools.partial(_mixin_kernel, tm=tm, n_ctx=n_ctx, bps=bps),
        grid=(m // tm,),
        in_specs=[pl.BlockSpec((tm, d), row),
                  pl.BlockSpec((1, N_MOD, d), lambda i: (i // bps, 0, 0)),
                  pl.BlockSpec((1, N_MOD, d), lambda i: (ctx_row, 0, 0)),
                  pl.BlockSpec((1, d), const),
                  pl.BlockSpec((None, d, P_IN), lambda i: (layer, 0, 0), pipeline_mode=pl.Buffered(1)),
                  pl.BlockSpec((tm, D_QK), lambda i: (i % bps, 0)),
                  pl.BlockSpec((tm, D_QK), lambda i: (i % bps, 0))],
        out_specs=[pl.BlockSpec((tm, RWKV_IN), row), pl.BlockSpec((tm, D_CONV), row),
                   pl.BlockSpec((tm, D_QK), row), pl.BlockSpec((tm, D_QK), row), pl.BlockSpec((tm, D_DIFF), row)],
        out_shape=[jax.ShapeDtypeStruct((m, RWKV_IN), f32), jax.ShapeDtypeStruct((m, D_CONV), f32),
                   jax.ShapeDtypeStruct((m, D_QK), bf16), jax.ShapeDtypeStruct((m, D_QK), bf16),
                   jax.ShapeDtypeStruct((m, D_DIFF), bf16)],
        compiler_params=_cparams(1),
        name="mixin",
    )(t, mods, mods, g.reshape(1, d), w, cos, sin)


def _attn_kernel(q_ref, k_ref, v_ref, lam_ref, g_ref, o_ref, vaug_ref, *, lam_init, n_ctx, n_lat, tq, unroll,
                 with_ctx):
    lv = lam_ref[...]
    lam = (jnp.exp(jnp.sum(lv[0:1] * lv[1:2], keepdims=True)) - jnp.exp(jnp.sum(lv[2:3] * lv[3:4], keepdims=True))
           + lam_init)

    @pl.when(jnp.logical_and(pl.program_id(0) == 0, pl.program_id(1) == 0))
    def _():
        vaug_ref[:, DIFF_V_DIM:] = jnp.ones((vaug_ref.shape[0], DIFF_V_DIM), bf16)

    vaug_ref[:, :DIFF_V_DIM] = v_ref[...]

    def attend(q, k, v):
        parts = []
        for m in range(2):
            sl = slice(m * DIFF_QK_DIM, (m + 1) * DIFF_QK_DIM)
            s = _dot_nt(q[:, sl], k[:, sl])
            e = jnp.exp2(s - jnp.max(s, axis=-1, keepdims=True))
            pv = _dot(e.astype(bf16), v)
            parts.append(pv[:, :DIFF_V_DIM] * (1.0 / pv[:, DIFF_V_DIM:]))
        o = parts[0] - lam * parts[1]
        o = o * lax.rsqrt(jnp.mean(o * o, axis=-1, keepdims=True) + 1e-5) * g_ref[...]
        return o * (1.0 - lam_init)

    out_off = 0
    if with_ctx:
        o_ref[0:n_ctx, :] = attend(q_ref[0:n_ctx, :], k_ref[0:n_ctx, :], vaug_ref[0:n_ctx, :])
        out_off = n_ctx
    align = math.gcd(n_ctx, tq)

    def body(i, carry):
        for u in range(unroll):
            t = i * unroll + u
            q_rows = pl.ds(pl.multiple_of(n_ctx + t * tq, align), tq)
            o_rows = pl.ds(pl.multiple_of(out_off + t * tq, align), tq)
            o_ref[o_rows, :] = attend(q_ref[q_rows, :], k_ref[...], vaug_ref[...])
        return carry

    lax.fori_loop(0, n_lat // (tq * unroll), body, 0)


def _attn(q, k, v, lam_vecs, norm_g, lam_init, nbatch, n_ctx, n_lat, with_ctx):
    seq = n_ctx + n_lat
    tq = TM
    unroll = 8 if n_lat % (8 * tq) == 0 else 1
    out_seq = seq if with_ctx else n_lat
    blk = lambda b, h: (b, h)
    return pl.pallas_call(
        functools.partial(_attn_kernel, lam_init=lam_init, n_ctx=n_ctx, n_lat=n_lat, tq=tq, unroll=unroll,
                          with_ctx=with_ctx),
        grid=(nbatch, DIFF_HEADS),
        in_specs=[pl.BlockSpec((seq, DIFF_V_DIM), blk), pl.BlockSpec((seq, DIFF_V_DIM), blk),
                  pl.BlockSpec((seq, DIFF_V_DIM), blk),
                  pl.BlockSpec((4, DIFF_QK_DIM), lambda b, h: (0, 0)),
                  pl.BlockSpec((1, DIFF_V_DIM), lambda b, h: (0, 0))],
        out_specs=pl.BlockSpec((out_seq, DIFF_V_DIM), blk),
        out_shape=jax.ShapeDtypeStruct((nbatch * out_seq, D_DIFF), f32),
        scratch_shapes=[pltpu.VMEM((seq, 2 * DIFF_V_DIM), bf16)],
        compiler_params=_cparams(2),
        name="diffattn",
    )(q, k, v, lam_vecs, norm_g.reshape(1, DIFF_V_DIM))


def _segment_flags(i, nbc, nbs):
    j = i % nbs
    first = jnp.logical_or(j == 0, j == nbc)
    last = jnp.logical_or(j == nbc - 1, j == nbs - 1)
    return first, last


def _halo_specs(width, n_rows):
    per = TM // HALO
    nblk = n_rows // HALO
    prev = pl.BlockSpec((HALO, width), lambda i: (jnp.maximum(i * per - 1, 0), 0))
    nxt = pl.BlockSpec((HALO, width), lambda i: (jnp.minimum((i + 1) * per, nblk - 1), 0))
    return prev, nxt


def _conv_kernel(h_ref, hp_ref, hn_ref, w_ref, b_ref, g_ref, bb_ref, o_ref, buf, shf, *, nbc, nbs):
    first, last = _segment_flags(pl.program_id(0), nbc, nbs)
    buf[0:HALO, :] = jnp.where(first, 0.0, hp_ref[...])
    buf[HALO:HALO + TM, :] = h_ref[...]
    buf[HALO + TM:HALO + TM + HALO, :] = jnp.where(last, 0.0, hn_ref[...])
    sub = 64
    sublanes = 8
    first_tap = HALO - CONV_HALF
    for r0 in range(0, TM, sub):
        acc = jnp.zeros((sub, D_CONV), f32) + b_ref[...]
        for r in range(sublanes):
            taps = [kk for kk in range(CONV_WIDTH) if (first_tap + kk) % sublanes == r]
            if not taps:
                continue
            span = sub + sublanes * max((first_tap + kk) // sublanes for kk in taps)
            shf[0:span, :] = buf[r0 + r:r0 + r + span, :]
            for kk in taps:
                a = sublanes * ((first_tap + kk) // sublanes)
                acc = acc + shf[a:a + sub, :] * w_ref[kk:kk + 1, :]
        mu = jnp.mean(acc, axis=-1, keepdims=True)
        xc = acc - mu
        var = jnp.mean(xc * xc, axis=-1, keepdims=True)
        y = xc * lax.rsqrt(var + 1e-5) * g_ref[...] + bb_ref[...]
        o_ref[r0:r0 + sub, :] = y * _sigmoid(y)


def _conv(hc, dw_w, dw_b, ln_g, ln_b, nbc, nbs):
    m = hc.shape[0]
    prev, nxt = _halo_specs(D_CONV, m)
    const = lambda i: (0, 0)
    return pl.pallas_call(
        functools.partial(_conv_kernel, nbc=nbc, nbs=nbs),
        grid=(m // TM,),
        in_specs=[pl.BlockSpec((TM, D_CONV), lambda i: (i, 0)), prev, nxt,
                  pl.BlockSpec((CONV_WIDTH, D_CONV), const), pl.BlockSpec((1, D_CONV), const),
                  pl.BlockSpec((1, D_CONV), const), pl.BlockSpec((1, D_CONV), const)],
        out_specs=pl.BlockSpec((TM, D_CONV), lambda i: (i, 0)),
        out_shape=jax.ShapeDtypeStruct((m, D_CONV), f32),
        scratch_shapes=[pltpu.VMEM((TM + 2 * HALO, D_CONV), f32), pltpu.VMEM((64 + 2 * HALO, D_CONV), f32)],
        compiler_params=_cparams(1),
        name="convmod",
    )(hc, hc, hc, dw_w, dw_b.reshape(1, D_CONV), ln_g.reshape(1, D_CONV), ln_b.reshape(1, D_CONV))


def _rwkv_chunk_kernel(f_ref, fp_ref, fn_ref, mu_ref, w0_ref, w2_ref, a0_ref, a2_ref, g2_ref, kk_ref, ka_ref, rk_ref,
                       rp_ref, y0_ref, mm_ref, nn_ref, bv_ref, gg_ref,
                       s_r, s_v, s_a, s_lw, s_k, s_b, *, nbc, nbs):
    first, last = _segment_flags(pl.program_id(0), nbc, nbs)
    f = f_ref[...]
    prow = jnp.where(first, 0.0, fp_ref[HALO - 1:HALO, :])
    nrow = jnp.where(last, 0.0, fn_ref[0:1, :])
    rows = lax.broadcasted_iota(jnp.int32, f.shape, 0)
    prev = jnp.where(rows == 0, prow, pltpu.roll(f, 1, 0))
    nxt = jnp.where(rows == TM - 1, nrow, pltpu.roll(f, TM - 1, 0))
    fs = f + mu_ref[0:1, :] * (prev - f) + mu_ref[1:2, :] * (nxt - f)

    r = fs[:, 0:D_RWKV]
    k = fs[:, D_RWKV:2 * D_RWKV]
    v = fs[:, 2 * D_RWKV:3 * D_RWKV]
    o_w = 3 * D_RWKV
    o_a = o_w + 2 * DECAY_LORA
    o_g = o_a + 2 * AAA_LORA
    ones_bd = _head_ones(D_RWKV, HEAD_DIM)
    gg_ref[...] = _dot(_sigmoid(fs[:, o_g:o_g + GATE_LORA]).astype(bf16), g2_ref[...].astype(bf16))
    kk = k * kk_ref[...]
    kk = kk * lax.rsqrt(_dot_exact_rhs(kk * kk, ones_bd) + NORM_EPS)
    ksum = None
    for d in range(2):
        wd = fs[:, o_w + d * DECAY_LORA:o_w + (d + 1) * DECAY_LORA]
        ad = fs[:, o_a + d * AAA_LORA:o_a + (d + 1) * AAA_LORA]
        w_raw = w0_ref[d:d + 1, :] + _dot(jnp.tanh(wd).astype(bf16), w2_ref[d].astype(bf16))
        z = -w_raw
        softplus = jnp.maximum(z, 0.0) + jnp.log(1.0 + jnp.exp(-jnp.abs(z)))
        s_lw[d] = -jnp.exp(-softplus - 0.5)
        a = _sigmoid(a0_ref[d:d + 1, :] + _dot(ad.astype(bf16), a2_ref[d].astype(bf16)))
        kd = k * (1.0 + (a - 1.0) * ka_ref[...])
        ksum = kd if d == 0 else ksum + kd
        s_k[d] = kd
        s_b[d] = kk * a
    s_r[...] = r
    s_v[...] = v
    s_a[...] = -kk
    bv_ref[...] = _dot_exact_rhs(r * ksum * rk_ref[...], ones_bd) * v

    ri = lax.broadcasted_iota(jnp.int32, (CHUNK, CHUNK), 0)
    ci = lax.broadcasted_iota(jnp.int32, (CHUNK, CHUNK), 1)
    tri = tuple(jnp.where(msk, 1.0, 0.0).astype(bf16) for msk in (ci <= ri, ci >= ri))
    rp_i = lax.broadcasted_iota(jnp.int32, (CHUNK, PAIR), 0)
    cp_i = lax.broadcasted_iota(jnp.int32, (CHUNK, PAIR), 1) % CHUNK
    eye = jnp.where(rp_i == cp_i, 1.0, 0.0).astype(f32)
    strict = (cp_i < rp_i, cp_i > rp_i)
    incl = (cp_i <= rp_i, cp_i >= rp_i)
    low = lax.broadcasted_iota(jnp.int32, (CHUNK, PAIR), 1) < HEAD_DIM

    def chunk_body(ci, carry):
        cs = [ci * CHUNKS_PER_BODY + u for u in range(CHUNKS_PER_BODY)]
        rss = [pl.ds(pl.multiple_of(c * CHUNK, CHUNK), CHUNK) for c in cs]
        v16_c, at16, rt, bt16, kt16, wc = [], {}, {}, {}, {}, {}
        for u, rs in enumerate(rss):
            r_c = s_r[rs, :]
            a_c = s_a[rs, :]
            v16_c.append(s_v[rs, :].astype(bf16))
            for d in range(2):
                lw = s_lw[d, rs, :]
                cum = _dot_exact_lhs(tri[d], lw)
                er = jnp.exp(cum)
                ei = jnp.exp(-cum)
                at16[u, d] = (a_c * jnp.exp(cum - lw)).astype(bf16)
                rt[u, d] = r_c * er
                bt16[u, d] = (s_b[d, rs, :] * ei).astype(bf16)
                kt16[u, d] = (s_k[d, rs, :] * ei).astype(bf16)
                wc[u, d] = er[CHUNK - 1:CHUNK, :] if d == 0 else er[0:1, :]

        insts = [(u, d, p) for u in range(CHUNKS_PER_BODY) for d in range(2) for p in range(N_PAIR)]
        psl = lambda p: slice(p * PAIR, (p + 1) * PAIR)
        gms = [_dot_nt(jnp.concatenate([at16[u, d][:, psl(p)], rt[u, d][:, psl(p)].astype(bf16)], axis=0),
                       jnp.concatenate([_pair_blockdiag(bt16[u, d][:, psl(p)]),
                                        _pair_blockdiag(kt16[u, d][:, psl(p)])], axis=0)) for u, d, p in insts]
        lab = [jnp.where(strict[d], g[:CHUNK, :PAIR], 0.0) for (u, d, p), g in zip(insts, gms)]
        lakrk16 = [jnp.concatenate([jnp.where(strict[d], g[:CHUNK, PAIR:], 0.0),
                                    jnp.where(incl[d], g[CHUNK:, PAIR:], 0.0)], axis=0).astype(bf16)
                   for (u, d, p), g in zip(insts, gms)]
        lrb16 = [jnp.where(incl[d], g[CHUNK:, :PAIR], 0.0).astype(bf16) for (u, d, p), g in zip(insts, gms)]
        uv = [_dot(lk, _pair_blockdiag(v16_c[u][:, psl(p)])) for (u, d, p), lk in zip(insts, lakrk16)]
        vtk = [_dot_tn(v16_c[u][:, psl(p)], kt16[u, d][:, psl(p)]) for u, d, p in insts]
        def split(t):
            hi = t.astype(bf16)
            return hi, (t - hi.astype(f32)).astype(bf16)

        def mul_x3(a, b):
            a_hi, a_lo = split(a)
            b_hi, b_lo = split(b)
            w = b.shape[1]
            full = _dot(a_hi, _pair_blockdiag(jnp.concatenate([b_hi, b_lo], axis=1)))
            return full[:, :w] + full[:, w:] + _dot(a_lo, _pair_blockdiag(b_hi))

        n_stage = int(math.log2(CHUNK))
        x = [eye + l for l in lab]
        q = [mul_x3(l, l) for l in lab]
        for stage in range(1, n_stage):
            if stage < n_stage - 1:
                prod = [mul_x3(jnp.concatenate([xx, qq], axis=0), qq) for xx, qq in zip(x, q)]
                x = [xx + pr[:CHUNK] for xx, pr in zip(x, prod)]
                q = [pr[CHUNK:] for pr in prod]
            else:
                x = [xx + mul_x3(xx, qq) for xx, qq in zip(x, q)]
        z0 = [jnp.concatenate([at16[u, d][:, psl(p)].astype(f32), lv[:CHUNK]], axis=1)
              for (u, d, p), lv in zip(insts, uv)]
        au16 = [mul_x3(xx, zz).astype(bf16) for xx, zz in zip(x, z0)]
        ry = [_dot(lr, _pair_blockdiag(au)) for lr, au in zip(lrb16, au16)]
        mn = [_dot_tn(au, bt16[u, d][:, psl(p)]) for (u, d, p), au in zip(insts, au16)]
        for i, (u, d, p) in enumerate(insts):
            ps = psl(p)
            rp_ref[d, rss[u], ps] = rt[u, d][:, ps] + ry[i][:, :PAIR]
            y0_ref[d, rss[u], ps] = ry[i][:, PAIR:] + uv[i][CHUNK:]
            wc_p = wc[u, d][:, ps]
            m_pair = jnp.where(low, mn[i][0:CHUNK], mn[i][CHUNK:2 * CHUNK])
            n_pair = jnp.where(low, mn[i][2 * CHUNK:3 * CHUNK], mn[i][3 * CHUNK:])
            k_pair = jnp.where(low, vtk[i][:CHUNK], vtk[i][CHUNK:])
            mm_ref[d, cs[u], p] = (eye + m_pair) * wc_p
            nn_ref[d, cs[u], p] = (n_pair + k_pair) * wc_p
        return carry

    lax.fori_loop(0, TM // (CHUNK * CHUNKS_PER_BODY), chunk_body, 0)


def _rwkv_chunks(fr, mu, w0, w2, a0, a2, g2, kk, ka, rk, nbc, nbs):
    m = fr.shape[0]
    nct = TM // CHUNK
    prev, nxt = _halo_specs(RWKV_IN, m)
    c2 = lambda i: (0, 0)
    c3 = lambda i: (0, 0, 0)
    row = lambda i: (i, 0)
    dirrow = lambda i: (0, i, 0)
    ops = lambda i: (0, i, 0, 0, 0)
    return pl.pallas_call(
        functools.partial(_rwkv_chunk_kernel, nbc=nbc, nbs=nbs),
        grid=(m // TM,),
        in_specs=[pl.BlockSpec((TM, RWKV_IN), row), prev, nxt,
                  pl.BlockSpec((2, RWKV_IN), c2), pl.BlockSpec((2, D_RWKV), c2),
                  pl.BlockSpec((2, DECAY_LORA, D_RWKV), c3), pl.BlockSpec((2, D_RWKV), c2),
                  pl.BlockSpec((2, AAA_LORA, D_RWKV), c3), pl.BlockSpec((GATE_LORA, D_RWKV), c2),
                  pl.BlockSpec((1, D_RWKV), c2), pl.BlockSpec((1, D_RWKV), c2), pl.BlockSpec((1, D_RWKV), c2)],
        out_specs=[pl.BlockSpec((2, TM, D_RWKV), dirrow), pl.BlockSpec((2, TM, D_RWKV), dirrow),
                   pl.BlockSpec((2, nct, N_PAIR, HEAD_DIM, PAIR), ops),
                   pl.BlockSpec((2, nct, N_PAIR, HEAD_DIM, PAIR), ops),
                   pl.BlockSpec((TM, D_RWKV), row), pl.BlockSpec((TM, D_RWKV), row)],
        out_shape=[jax.ShapeDtypeStruct((2, m, D_RWKV), f32), jax.ShapeDtypeStruct((2, m, D_RWKV), f32),
                   jax.ShapeDtypeStruct((2, m // CHUNK, N_PAIR, HEAD_DIM, PAIR), f32),
                   jax.ShapeDtypeStruct((2, m // CHUNK, N_PAIR, HEAD_DIM, PAIR), f32),
                   jax.ShapeDtypeStruct((m, D_RWKV), f32), jax.ShapeDtypeStruct((m, D_RWKV), f32)],
        scratch_shapes=[pltpu.VMEM((TM, D_RWKV), f32), pltpu.VMEM((TM, D_RWKV), f32), pltpu.VMEM((TM, D_RWKV), f32),
                        pltpu.VMEM((2, TM, D_RWKV), f32), pltpu.VMEM((2, TM, D_RWKV), f32),
                        pltpu.VMEM((2, TM, D_RWKV), f32)],
        compiler_params=_cparams(1),
        name="rwkv_chunks",
    )(fr, fr, fr, mu, w0, w2, a0, a2, g2, kk.reshape(1, D_RWKV), ka.reshape(1, D_RWKV), rk.reshape(1, D_RWKV))


def _rwkv_state_kernel(mf_ref, mb_ref, nf_ref, nb_ref, sf_ref, sb_ref, st_ref, *, nbatch):
    @pl.when(pl.program_id(0) == 0)
    def _():
        st_ref[...] = jnp.zeros(st_ref.shape, f32)

    group = 8
    probs = [(d, b, p) for b in range(nbatch) for d in range(2) for p in range(N_PAIR)]
    for g0 in range(0, len(probs), group):
        grp = probs[g0:g0 + group]
        st = [st_ref[d, b, p] for d, b, p in grp]
        for (d, b, p), s in zip(grp, st):
            (sf_ref if d == 0 else sb_ref)[b, 0, p] = s
        mt = [(mf_ref if d == 0 else mb_ref)[0, b, 0, p] for d, b, p in grp]
        s1 = [s.astype(bf16) for s in st]
        s2 = [(s - a.astype(f32)).astype(bf16) for s, a in zip(st, s1)]
        m1 = [x.astype(bf16) for x in mt]
        m2 = [(x - a.astype(f32)).astype(bf16) for x, a in zip(mt, m1)]
        p12 = [_dot(a1, _pair_blockdiag(jnp.concatenate([b1, b2], axis=1))) for a1, b1, b2 in zip(s1, m1, m2)]
        p21 = [_dot(a2, _pair_blockdiag(b1)) for a2, b1 in zip(s2, m1)]
        for (d, b, p), x12, x21 in zip(grp, p12, p21):
            st_ref[d, b, p] = (x12[:, :PAIR] + x12[:, PAIR:] + x21) + (nf_ref if d == 0 else nb_ref)[0, b, 0, p]


def _rwkv_state(mm, nn, nbatch, nca, ncc):
    shp = (2, nbatch, nca, N_PAIR, HEAD_DIM, PAIR)
    blk = (1, nbatch, 1, N_PAIR, HEAD_DIM, PAIR)
    c_bwd = lambda s: jnp.where(s < ncc, ncc - 1 - s, nca - 1 - (s - ncc))
    fwd = pl.BlockSpec(blk, lambda s: (0, 0, s, 0, 0, 0))
    bwd = pl.BlockSpec(blk, lambda s: (1, 0, c_bwd(s), 0, 0, 0))
    oshape = jax.ShapeDtypeStruct(shp[1:], f32)
    return pl.pallas_call(
        functools.partial(_rwkv_state_kernel, nbatch=nbatch),
        grid=(nca,),
        in_specs=[fwd, bwd, fwd, bwd],
        out_specs=[pl.BlockSpec(blk[1:], lambda s: (0, s, 0, 0, 0)),
                   pl.BlockSpec(blk[1:], lambda s: (0, c_bwd(s), 0, 0, 0))],
        out_shape=[oshape, oshape],
        scratch_shapes=[pltpu.VMEM((2, nbatch, N_PAIR, HEAD_DIM, PAIR), f32)],
        compiler_params=_cparams(1),
        name="rwkv_state",
    )(mm.reshape(shp), mm.reshape(shp), nn.reshape(shp), nn.reshape(shp))


def _rwkv_readout(rp_ref, y0_ref, sf_ref, sb_ref, bv_ref, gg_ref, lng_ref, lnb_ref):
    rows = []
    for c in range(TM // CHUNK):
        rs = slice(c * CHUNK, (c + 1) * CHUNK)
        cols = []
        for p in range(N_PAIR):
            ps = slice(p * PAIR, (p + 1) * PAIR)
            lhs = jnp.concatenate([rp_ref[0, rs, ps], rp_ref[1, rs, ps]], axis=1).astype(bf16)
            rhs = jnp.concatenate([_pair_blockdiag(sf_ref[c, p].astype(bf16)),
                                   _pair_blockdiag(sb_ref[c, p].astype(bf16))], axis=1)
            cols.append(y0_ref[0, rs, ps] + y0_ref[1, rs, ps] + _dot_nt(lhs, rhs))
        rows.append(jnp.concatenate(cols, axis=1))
    y = jnp.concatenate(rows, axis=0)
    ones_bd = _head_ones(D_RWKV, HEAD_DIM)
    mu = _dot_exact_rhs(y, ones_bd) * (1.0 / HEAD_DIM)
    yc = y - mu
    var = _dot_exact_rhs(yc * yc, ones_bd) * (1.0 / HEAD_DIM)
    yn = yc * lax.rsqrt(var + GN_EPS) * lng_ref[...] + lnb_ref[...]
    return (yn + bv_ref[...]) * gg_ref[...]


def _mixout_kernel(t_ref, rp_ref, y0_ref, sf_ref, sb_ref, bv_ref, gg_ref, lng_ref, lnb_ref, b_ref, c_ref, w_ref, m_ref,
                   o_ref):
    a = _rwkv_readout(rp_ref, y0_ref, sf_ref, sb_ref, bv_ref, gg_ref, lng_ref, lnb_ref)
    out = _dot(a.astype(bf16), w_ref[0:D_RWKV, :])
    out = out + _dot(b_ref[...].astype(bf16), w_ref[D_RWKV:D_RWKV + D_CONV, :])
    out = out + _dot(c_ref[...].astype(bf16), w_ref[D_RWKV + D_CONV:, :])
    o_ref[...] = t_ref[...] + m_ref[0, 5:6, :] * out


def _mixout(t, rwkv_parts, ln_g, ln_b, ob, oc, w, layer, mods, grid, row_map, mod_map, c_map, out_rows):
    rp, y0, sf, sb, bv, gg = rwkv_parts
    d = t.shape[1]
    nct = TM // CHUNK
    sshape = (bv.shape[0] // CHUNK, N_PAIR, HEAD_DIM, PAIR)
    n_axes = len(grid)
    if n_axes == 1:
        out_map = lambda i: (i, 0)
    else:
        nj = grid[1]
        out_map = lambda b, j: (b * nj + j, 0)
    rmap = lambda *a: (row_map(*a), 0)
    dirrow = lambda *a: (0, row_map(*a), 0)
    st = lambda *a: (row_map(*a), 0, 0, 0)
    c2 = lambda *a: (0, 0)
    return pl.pallas_call(
        _mixout_kernel,
        grid=grid,
        in_specs=[pl.BlockSpec((TM, d), rmap),
                  pl.BlockSpec((2, TM, D_RWKV), dirrow), pl.BlockSpec((2, TM, D_RWKV), dirrow),
                  pl.BlockSpec((nct, N_PAIR, HEAD_DIM, PAIR), st), pl.BlockSpec((nct, N_PAIR, HEAD_DIM, PAIR), st),
                  pl.BlockSpec((TM, D_RWKV), rmap), pl.BlockSpec((TM, D_RWKV), rmap),
                  pl.BlockSpec((1, D_RWKV), c2), pl.BlockSpec((1, D_RWKV), c2),
                  pl.BlockSpec((TM, D_CONV), rmap),
                  pl.BlockSpec((TM, D_DIFF), lambda *a: (c_map(*a), 0)),
                  pl.BlockSpec((None, D_RWKV + D_CONV + D_DIFF, d), lambda *a: (layer, 0, 0)),
                  pl.BlockSpec((1, N_MOD, d), lambda *a: (mod_map(*a), 0, 0))],
        out_specs=pl.BlockSpec((TM, d), out_map),
        out_shape=jax.ShapeDtypeStruct((out_rows, d), f32),
        compiler_params=_cparams(n_axes),
        name="mixout",
    )(t, rp, y0, sf.reshape(sshape), sb.reshape(sshape), bv, gg, ln_g.reshape(1, D_RWKV), ln_b.reshape(1, D_RWKV),
      ob, oc, w, mods)


def _rope_tables(n_ctx, n_lat):
    n_rows = n_lat // GRID_W
    row = jnp.repeat(jnp.arange(n_rows, dtype=jnp.int32), GRID_W)
    col = jnp.tile(jnp.arange(GRID_W, dtype=jnp.int32), n_rows)
    inv = 1.0 / (ROPE_THETA ** (jnp.arange(ROPE_FREQS, dtype=f32) * 2.0 / AXIS_DIM))
    ang = jnp.stack([row, col], axis=-1).astype(f32)[..., None] * inv
    cos, sin = jnp.cos(ang), jnp.sin(ang)
    cos64 = jnp.concatenate([cos[:, 0], cos[:, 0], cos[:, 1], cos[:, 1]], axis=-1)
    sin64 = jnp.concatenate([-sin[:, 0], sin[:, 0], -sin[:, 1], sin[:, 1]], axis=-1)
    reps = D_QK // DIFF_QK_DIM
    cos_t = jnp.concatenate([jnp.ones((n_ctx, D_QK), f32), jnp.tile(cos64, (1, reps))], axis=0)
    sin_t = jnp.concatenate([jnp.zeros((n_ctx, D_QK), f32), jnp.tile(sin64, (1, reps))], axis=0)
    return cos_t, sin_t


def kernel(x, c, ctx, c_ctx, ada_w, ada_b, norm_g, ffn_w_in, ffn_w_out, mix_w_in, mix_w_out, rwkv_mu, rwkv_w0, rwkv_w2,
           rwkv_a0, rwkv_a2, rwkv_g2, rwkv_kk, rwkv_ka, rwkv_rk, rwkv_ln_g, rwkv_ln_b, conv_dw_w, conv_dw_b, conv_ln_g,
           conv_ln_b, diff_lam, diff_norm_g, final_g):
    nb, n_lat, d = x.shape
    n_ctx = ctx.shape[1]
    depth = ada_w.shape[0]
    seq = n_ctx + n_lat
    assert n_ctx % TM == 0 and n_lat % TM == 0 and n_lat % GRID_W == 0 and seq % n_ctx == 0
    assert nb + 1 <= COND_ROWS
    nbc, nbx, nbs = n_ctx // TM, n_lat // TM, seq // TM
    m = nb * seq

    t = jnp.concatenate([ctx, x], axis=1).reshape(m, d)
    cond = jnp.zeros((COND_ROWS, d), f32).at[:nb].set(c).at[nb].set(c_ctx)
    mods = _adaln(cond, ada_w, ada_b).reshape(depth, COND_ROWS, N_MOD, d)
    cos_t, sin_t = _rope_tables(n_ctx, n_lat)
    w_in16 = ffn_w_in.astype(bf16)
    w_out16 = ffn_w_out.astype(bf16)
    mix_in16 = mix_w_in.astype(bf16)
    mix_out16 = mix_w_out.astype(bf16)

    mod_all = lambda i: jnp.where(i % nbs < nbc, nb, i // nbs)
    row_all = lambda i: i
    row_lat = lambda b, j: b * nbs + nbc + j
    mod_lat = lambda b, j: b

    out = None
    for l in range(depth):
        last = l == depth - 1
        ml = mods[l]
        t = _ffn(t, ml, norm_g[l, 0], w_in16, w_out16, l, 0, 0, seq, n_ctx, nb)
        fr, hc, q, k, v = _mixin(t, ml, norm_g[l, 1], mix_in16, l, cos_t, sin_t, seq, n_ctx, nb)
        rp, y0, mm, nn, bv, gg = _rwkv_chunks(fr, rwkv_mu[l], rwkv_w0[l], rwkv_w2[l], rwkv_a0[l], rwkv_a2[l], rwkv_g2[l],
                                              rwkv_kk[l], rwkv_ka[l], rwkv_rk[l].reshape(D_RWKV), nbc, nbs)
        sf, sb = _rwkv_state(mm, nn, nb, seq // CHUNK, n_ctx // CHUNK)
        rwkv_parts = (rp, y0, sf, sb, bv, gg)
        ob = _conv(hc, conv_dw_w[l], conv_dw_b[l], conv_ln_g[l], conv_ln_b[l], nbc, nbs)
        lam_init = 0.8 - 0.6 * math.exp(-0.3 * l)
        if last:
            oc = _attn(q, k, v, diff_lam[l], diff_norm_g[l], lam_init, nb, n_ctx, n_lat, False)
            t = _mixout(t, rwkv_parts, rwkv_ln_g[l], rwkv_ln_b[l], ob, oc, mix_out16, l, ml, (nb, nbx), row_lat,
                        mod_lat, lambda b, j: b * nbx + j, nb * n_lat)
            out = _ffn(t, ml, norm_g[l, 2], w_in16, w_out16, l, 1, 6, n_lat, 0, nb, final_g=final_g)
        else:
            oc = _attn(q, k, v, diff_lam[l], diff_norm_g[l], lam_init, nb, n_ctx, n_lat, True)
            t = _mixout(t, rwkv_parts, rwkv_ln_g[l], rwkv_ln_b[l], ob, oc, mix_out16, l, ml, (nbs * nb,), row_all,
                        mod_all, row_all, m)
            t = _ffn(t, ml, norm_g[l, 2], w_in16, w_out16, l, 1, 6, seq, n_ctx, nb)
    return out.reshape(nb, n_lat, d)
```

```python
import functools
import math

import jax
import jax.numpy as jnp
from jax import lax
from jax.experimental import pallas as pl
from jax.experimental.pallas import tpu as pltpu

f32 = jnp.float32
bf16 = jnp.bfloat16

N_MOD = 9
EPS = 1e-6
GRID_W = 64
RWKV_HEADS = 4
HEAD_DIM = 64
D_RWKV = RWKV_HEADS * HEAD_DIM
PAIR = 2 * HEAD_DIM
N_PAIR = RWKV_HEADS // 2
DECAY_LORA = 64
AAA_LORA = 64
GATE_LORA = 128
GN_EPS = 64e-5
NORM_EPS = 1e-12
D_CONV = 256
CONV_WIDTH = 31
CONV_HALF = CONV_WIDTH // 2
DIFF_HEADS = 4
DIFF_QK_DIM = 64
DIFF_V_DIM = 2 * DIFF_QK_DIM
D_DIFF = DIFF_HEADS * DIFF_V_DIM
ROPE_THETA = 10000.0
LOG2_E = 1.4426950408889634
AXIS_DIM = DIFF_QK_DIM // 2
ROPE_FREQS = AXIS_DIM // 2
RWKV_IN = 3 * D_RWKV + 2 * DECAY_LORA + 2 * AAA_LORA + GATE_LORA
CONV_IN = 2 * D_CONV
D_QK = DIFF_HEADS * 2 * DIFF_QK_DIM
DIFF_IN = 2 * D_QK + D_DIFF
P_IN = RWKV_IN + CONV_IN + DIFF_IN

TM = 256
CHUNK = 64
CHUNKS_PER_BODY = 4
HALO = 16
COND_ROWS = 16
VMEM_LIMIT = 56 * 1024 * 1024


def _dot(a, b, prec=None):
    return jnp.dot(a, b, preferred_element_type=f32, precision=prec)


def _dot_nt(a, b, prec=None):
    return lax.dot_general(a, b, (((1,), (1,)), ((), ())), preferred_element_type=f32, precision=prec)


def _dot_tn(a, b, prec=None):
    return lax.dot_general(a, b, (((0,), (0,)), ((), ())), preferred_element_type=f32, precision=prec)


def _split3(a):
    a1 = a.astype(bf16)
    r1 = a - a1.astype(f32)
    a2 = r1.astype(bf16)
    a3 = (r1 - a2.astype(f32)).astype(bf16)
    return a1, a2, a3


def _dot_exact_rhs(a, b16):
    a1, a2, a3 = _split3(a)
    return _dot(a1, b16) + _dot(a2, b16) + _dot(a3, b16)


def _dot_exact_lhs(a16, b):
    b1, b2, b3 = _split3(b)
    return _dot(a16, b1) + _dot(a16, b2) + _dot(a16, b3)


def _pair_blockdiag(x):
    first = (lax.broadcasted_iota(jnp.int32, x.shape, 1) % PAIR) < HEAD_DIM
    zero = jnp.zeros_like(x)
    return jnp.concatenate([jnp.where(first, x, zero), jnp.where(first, zero, x)], axis=0)


def _sigmoid(x):
    return 1.0 / (1.0 + jnp.exp(-x))


def _cparams(n_axes):
    return pltpu.CompilerParams(dimension_semantics=("arbitrary",) * n_axes, vmem_limit_bytes=VMEM_LIMIT)


def _head_ones(n, width):
    r = lax.broadcasted_iota(jnp.int32, (n, n), 0) // width
    c = lax.broadcasted_iota(jnp.int32, (n, n), 1) // width
    return jnp.where(r == c, 1.0, 0.0).astype(bf16)


def _adaln_kernel(c_ref, w_ref, b_ref, o_ref):
    cond = c_ref[...]
    cond = cond * _sigmoid(cond)
    o_ref[0] = _dot(cond.astype(bf16), w_ref[0].astype(bf16)) + b_ref[0]


def _adaln(cond, ada_w, ada_b):
    depth, d, nd = ada_w.shape
    tn = nd // 4
    return pl.pallas_call(
        _adaln_kernel,
        grid=(depth, nd // tn),
        in_specs=[pl.BlockSpec((COND_ROWS, d), lambda l, j: (0, 0)),
                  pl.BlockSpec((1, d, tn), lambda l, j: (l, 0, j)),
                  pl.BlockSpec((1, 1, tn), lambda l, j: (l, 0, j))],
        out_specs=pl.BlockSpec((1, COND_ROWS, tn), lambda l, j: (l, 0, j)),
        out_shape=jax.ShapeDtypeStruct((depth, COND_ROWS, nd), f32),
        compiler_params=_cparams(2),
        name="adaln",
    )(cond, ada_w, ada_b.reshape(depth, 1, nd))


def _modulated_norm(x, g, shift, scale):
    y = x * lax.rsqrt(jnp.mean(x * x, axis=-1, keepdims=True) + EPS) * g
    return y * (1.0 + scale) + shift


def _dense_tile(rows_per_seq):
    for tm in (1024, 768, 512, 256):
        if rows_per_seq % tm == 0:
            return tm
    raise ValueError(rows_per_seq)


def _mod_row(mb_ref, mc_ref, idx, tm, n_ctx, blocks_per_seq):
    if n_ctx == 0:
        return mb_ref[0, idx:idx + 1, :]
    row = (pl.program_id(0) % blocks_per_seq) * tm + lax.broadcasted_iota(jnp.int32, (tm, 1), 0)
    return jnp.where(row < n_ctx, mc_ref[0, idx:idx + 1, :], mb_ref[0, idx:idx + 1, :])


def _ffn_kernel(*refs, source, s0, ff, fc, final, tm, n_ctx, bps):
    n_src = {"rows": 1, "split": 4}[source]
    src, (mb_ref, mc_ref, g_ref, wi_ref, wo_ref), rest = refs[:n_src], refs[n_src:n_src + 5], refs[n_src + 5:]
    o_ref = rest[-1]
    mod = lambda idx: _mod_row(mb_ref, mc_ref, idx, tm, n_ctx, bps)
    if source == "rows":
        x = src[0][...]
    else:
        first = pl.program_id(0) % bps == 0
        x = jnp.concatenate([jnp.where(first, src[0][...], src[1][...]), src[2][...], src[3][...]], axis=0)
    h = _modulated_norm(x, g_ref[...], mod(s0), mod(s0 + 1)).astype(bf16)
    acc = jnp.zeros(x.shape, f32)
    for j in range(ff // fc):
        gt = _dot(h, wi_ref[:, j * fc:(j + 1) * fc])
        up = _dot(h, wi_ref[:, ff + j * fc:ff + (j + 1) * fc])
        act = (gt * _sigmoid(gt) * up).astype(bf16)
        acc = acc + _dot(act, wo_ref[j * fc:(j + 1) * fc, :])
    out = x + 0.5 * mod(s0 + 2) * acc
    if final:
        fg_ref = rest[0]
        out = out * lax.rsqrt(jnp.mean(out * out, axis=-1, keepdims=True) + EPS) * fg_ref[...]
    o_ref[...] = out


def _ffn(t, mods, g, w_in, w_out, layer, slot, s0, rows_per_seq, n_ctx, ctx_row, final_g=None):
    ff = w_out.shape[2]
    fc = 256 if ff % 256 == 0 else ff
    tm = _dense_tile(rows_per_seq)
    bps = rows_per_seq // tm
    const = lambda i: (0, 0)
    pick = lambda i: (layer, slot, 0, 0)
    row = lambda i: (i, 0)
    if isinstance(t, tuple):
        lat, ctx = t
        d = lat.shape[1]
        assert n_ctx == TM and tm == 3 * TM
        m = lat.shape[0] + ctx.shape[0]
        per = (rows_per_seq - n_ctx) // TM
        lat_blk = lambda u: pl.BlockSpec(
            (TM, d), lambda i: ((i // bps) * per + jnp.maximum(3 * (i % bps) + u - 1, 0), 0))
        source, src_specs, src_args = "split", [pl.BlockSpec((TM, d), lambda i: (i // bps, 0)),
                                                lat_blk(0), lat_blk(1), lat_blk(2)], [ctx, lat, lat, lat]
    else:
        m, d = t.shape
        source, src_specs, src_args = "rows", [pl.BlockSpec((tm, d), row)], [t]
    in_specs = src_specs + [pl.BlockSpec((1, N_MOD, d), lambda i: (i // bps, 0, 0)),
                            pl.BlockSpec((1, N_MOD, d), lambda i: (ctx_row, 0, 0)),
                            pl.BlockSpec((1, d), const),
                            pl.BlockSpec((None, None, d, 2 * ff), pick, pipeline_mode=pl.Buffered(1)),
                            pl.BlockSpec((None, None, ff, d), pick, pipeline_mode=pl.Buffered(1))]
    args = src_args + [mods, mods, g.reshape(1, d), w_in, w_out]
    if final_g is not None:
        in_specs.append(pl.BlockSpec((1, d), const))
        args.append(final_g.reshape(1, d))
    return pl.pallas_call(
        functools.partial(_ffn_kernel, source=source, s0=s0, ff=ff, fc=fc, final=final_g is not None, tm=tm,
                          n_ctx=n_ctx, bps=bps),
        grid=(m // tm,), in_specs=in_specs,
        out_specs=pl.BlockSpec((tm, d), lambda i: (i, 0)),
        out_shape=jax.ShapeDtypeStruct((m, d), f32),
        compiler_params=_cparams(1),
        name="ffn",
    )(*args)


def _mixin_kernel(x_ref, mb_ref, mc_ref, g_ref, w_ref, cos_ref, sin_ref, fr_ref, hc_ref, q_ref, k_ref, v_ref, *, tm,
                  n_ctx, bps):
    x = x_ref[...]
    mod = lambda idx: _mod_row(mb_ref, mc_ref, idx, tm, n_ctx, bps)
    h = _modulated_norm(x, g_ref[...], mod(3), mod(4)).astype(bf16)
    fr_ref[...] = _dot(h, w_ref[:, 0:RWKV_IN])
    o = RWKV_IN
    val = _dot(h, w_ref[:, o:o + D_CONV])
    gate = _dot(h, w_ref[:, o + D_CONV:o + CONV_IN])
    hc_ref[...] = val * _sigmoid(gate)
    o = RWKV_IN + CONV_IN
    cos = cos_ref[...]
    sin = sin_ref[...]
    lane = lax.broadcasted_iota(jnp.int32, cos.shape, 1)
    low_half = ((lane // ROPE_FREQS) % 2) == 0

    def rope(t):
        partner = jnp.where(low_half, pltpu.roll(t, D_QK - ROPE_FREQS, 1), pltpu.roll(t, ROPE_FREQS, 1))
        return t * cos + partner * sin

    q = _dot(h, w_ref[:, o:o + D_QK])
    q_ref[...] = (rope(q) * (DIFF_QK_DIM ** -0.5 * LOG2_E)).astype(bf16)
    k = _dot(h, w_ref[:, o + D_QK:o + 2 * D_QK])
    k_ref[...] = rope(k).astype(bf16)
    v_ref[...] = _dot(h, w_ref[:, o + 2 * D_QK:o + 2 * D_QK + D_DIFF]).astype(bf16)


def _mixin(t, mods, g, w, layer, cos, sin, seq, n_ctx, ctx_row):
    m, d = t.shape
    tm = _dense_tile(seq)
    bps = seq // tm
    const = lambda i: (0, 0)
    row = lambda i: (i, 0)
    return pl.pallas_call(
        functools.partial(_mixin_kernel, tm=tm, n_ctx=n_ctx, bps=bps),
        grid=(m // tm,),
        in_specs=[pl.BlockSpec((tm, d), row),
                  pl.BlockSpec((1, N_MOD, d), lambda i: (i // bps, 0, 0)),
                  pl.BlockSpec((1, N_MOD, d), lambda i: (ctx_row, 0, 0)),
                  pl.BlockSpec((1, d), const),
                  pl.BlockSpec((None, d, P_IN), lambda i: (layer, 0, 0), pipeline_mode=pl.Buffered(1)),
                  pl.BlockSpec((tm, D_QK), lambda i: (i % bps, 0)),
                  pl.BlockSpec((tm, D_QK), lambda i: (i % bps, 0))],
        out_specs=[pl.BlockSpec((tm, RWKV_IN), row), pl.BlockSpec((tm, D_CONV), row),
                   pl.BlockSpec((tm, D_QK), row), pl.BlockSpec((tm, D_QK), row), pl.BlockSpec((tm, D_DIFF), row)],
        out_shape=[jax.ShapeDtypeStruct((m, RWKV_IN), f32), jax.ShapeDtypeStruct((m, D_CONV), f32),
                   jax.ShapeDtypeStruct((m, D_QK), bf16), jax.ShapeDtypeStruct((m, D_QK), bf16),
                   jax.ShapeDtypeStruct((m, D_DIFF), bf16)],
        compiler_params=_cparams(1),
        name="mixin",
    )(t, mods, mods, g.reshape(1, d), w, cos, sin)


def _attn_kernel(q_ref, k_ref, v_ref, lam_ref, g_ref, o_ref, vaug_ref, *, lam_init, n_ctx, n_lat, tq, unroll,
                 with_ctx):
    lv = lam_ref[...]
    lam = (jnp.exp(jnp.sum(lv[0:1] * lv[1:2], keepdims=True)) - jnp.exp(jnp.sum(lv[2:3] * lv[3:4], keepdims=True))
           + lam_init)

    @pl.when(jnp.logical_and(pl.program_id(0) == 0, pl.program_id(1) == 0))
    def _():
        vaug_ref[:, DIFF_V_DIM:] = jnp.ones((vaug_ref.shape[0], DIFF_V_DIM), bf16)

    vaug_ref[:, :DIFF_V_DIM] = v_ref[...]

    def attend(q, k, v):
        parts = []
        for m in range(2):
            sl = slice(m * DIFF_QK_DIM, (m + 1) * DIFF_QK_DIM)
            s = _dot_nt(q[:, sl], k[:, sl])
            e = jnp.exp2(s - jnp.max(s, axis=-1, keepdims=True))
            pv = _dot(e.astype(bf16), v)
            parts.append(pv[:, :DIFF_V_DIM] * (1.0 / pv[:, DIFF_V_DIM:]))
        o = parts[0] - lam * parts[1]
        o = o * lax.rsqrt(jnp.mean(o * o, axis=-1, keepdims=True) + 1e-5) * g_ref[...]
        return o * (1.0 - lam_init)

    out_off = 0
    if with_ctx:
        o_ref[0:n_ctx, :] = attend(q_ref[0:n_ctx, :], k_ref[0:n_ctx, :], vaug_ref[0:n_ctx, :])
        out_off = n_ctx
    align = math.gcd(n_ctx, tq)

    def body(i, carry):
        for u in range(unroll):
            t = i * unroll + u
            q_rows = pl.ds(pl.multiple_of(n_ctx + t * tq, align), tq)
            o_rows = pl.ds(pl.multiple_of(out_off + t * tq, align), tq)
            o_ref[o_rows, :] = attend(q_ref[q_rows, :], k_ref[...], vaug_ref[...])
        return carry

    lax.fori_loop(0, n_lat // (tq * unroll), body, 0)


def _attn(q, k, v, lam_vecs, norm_g, lam_init, nbatch, n_ctx, n_lat, with_ctx):
    seq = n_ctx + n_lat
    tq = TM
    unroll = 8 if n_lat % (8 * tq) == 0 else 1
    out_seq = seq if with_ctx else n_lat
    blk = lambda b, h: (b, h)
    return pl.pallas_call(
        functools.partial(_attn_kernel, lam_init=lam_init, n_ctx=n_ctx, n_lat=n_lat, tq=tq, unroll=unroll,
                          with_ctx=with_ctx),
        grid=(nbatch, DIFF_HEADS),
        in_specs=[pl.BlockSpec((seq, DIFF_V_DIM), blk), pl.BlockSpec((seq, DIFF_V_DIM), blk),
                  pl.BlockSpec((seq, DIFF_V_DIM), blk),
                  pl.BlockSpec((4, DIFF_QK_DIM), lambda b, h: (0, 0)),
                  pl.BlockSpec((1, DIFF_V_DIM), lambda b, h: (0, 0))],
        out_specs=pl.BlockSpec((out_seq, DIFF_V_DIM), blk),
        out_shape=jax.ShapeDtypeStruct((nbatch * out_seq, D_DIFF), f32),
        scratch_shapes=[pltpu.VMEM((seq, 2 * DIFF_V_DIM), bf16)],
        compiler_params=_cparams(2),
        name="diffattn",
    )(q, k, v, lam_vecs, norm_g.reshape(1, DIFF_V_DIM))


def _segment_flags(i, nbc, nbs):
    j = i % nbs
    first = jnp.logical_or(j == 0, j == nbc)
    last = jnp.logical_or(j == nbc - 1, j == nbs - 1)
    return first, last


def _halo_specs(width, n_rows):
    per = TM // HALO
    nblk = n_rows // HALO
    prev = pl.BlockSpec((HALO, width), lambda i: (jnp.maximum(i * per - 1, 0), 0))
    nxt = pl.BlockSpec((HALO, width), lambda i: (jnp.minimum((i + 1) * per, nblk - 1), 0))
    return prev, nxt


def _conv_kernel(h_ref, hp_ref, hn_ref, w_ref, b_ref, g_ref, bb_ref, o_ref, buf, shf, *, nbc, nbs):
    first, last = _segment_flags(pl.program_id(0), nbc, nbs)
    buf[0:HALO, :] = jnp.where(first, 0.0, hp_ref[...])
    buf[HALO:HALO + TM, :] = h_ref[...]
    buf[HALO + TM:HALO + TM + HALO, :] = jnp.where(last, 0.0, hn_ref[...])
    sub = 64
    sublanes = 8
    first_tap = HALO - CONV_HALF
    for r0 in range(0, TM, sub):
        acc = jnp.zeros((sub, D_CONV), f32) + b_ref[...]
        for r in range(sublanes):
            taps = [kk for kk in range(CONV_WIDTH) if (first_tap + kk) % sublanes == r]
            if not taps:
                continue
            span = sub + sublanes * max((first_tap + kk) // sublanes for kk in taps)
            shf[0:span, :] = buf[r0 + r:r0 + r + span, :]
            for kk in taps:
                a = sublanes * ((first_tap + kk) // sublanes)
                acc = acc + shf[a:a + sub, :] * w_ref[kk:kk + 1, :]
        mu = jnp.mean(acc, axis=-1, keepdims=True)
        xc = acc - mu
        var = jnp.mean(xc * xc, axis=-1, keepdims=True)
        y = xc * lax.rsqrt(var + 1e-5) * g_ref[...] + bb_ref[...]
        o_ref[r0:r0 + sub, :] = y * _sigmoid(y)


def _conv(hc, dw_w, dw_b, ln_g, ln_b, nbc, nbs):
    m = hc.shape[0]
    prev, nxt = _halo_specs(D_CONV, m)
    const = lambda i: (0, 0)
    return pl.pallas_call(
        functools.partial(_conv_kernel, nbc=nbc, nbs=nbs),
        grid=(m // TM,),
        in_specs=[pl.BlockSpec((TM, D_CONV), lambda i: (i, 0)), prev, nxt,
                  pl.BlockSpec((CONV_WIDTH, D_CONV), const), pl.BlockSpec((1, D_CONV), const),
                  pl.BlockSpec((1, D_CONV), const), pl.BlockSpec((1, D_CONV), const)],
        out_specs=pl.BlockSpec((TM, D_CONV), lambda i: (i, 0)),
        out_shape=jax.ShapeDtypeStruct((m, D_CONV), f32),
        scratch_shapes=[pltpu.VMEM((TM + 2 * HALO, D_CONV), f32), pltpu.VMEM((64 + 2 * HALO, D_CONV), f32)],
        compiler_params=_cparams(1),
        name="convmod",
    )(hc, hc, hc, dw_w, dw_b.reshape(1, D_CONV), ln_g.reshape(1, D_CONV), ln_b.reshape(1, D_CONV))


def _rwkv_chunk_kernel(f_ref, fp_ref, fn_ref, mu_ref, w0_ref, w2_ref, a0_ref, a2_ref, g2_ref, kk_ref, ka_ref, rk_ref,
                       rp_ref, y0_ref, mm_ref, nn_ref, bv_ref, gg_ref,
                       s_r, s_v, s_a, s_lw, s_k, s_b, *, nbc, nbs):
    first, last = _segment_flags(pl.program_id(0), nbc, nbs)
    f = f_ref[...]
    prow = jnp.where(first, 0.0, fp_ref[HALO - 1:HALO, :])
    nrow = jnp.where(last, 0.0, fn_ref[0:1, :])
    rows = lax.broadcasted_iota(jnp.int32, f.shape, 0)
    prev = jnp.where(rows == 0, prow, pltpu.roll(f, 1, 0))
    nxt = jnp.where(rows == TM - 1, nrow, pltpu.roll(f, TM - 1, 0))
    fs = f + mu_ref[0:1, :] * (prev - f) + mu_ref[1:2, :] * (nxt - f)

    r = fs[:, 0:D_RWKV]
    k = fs[:, D_RWKV:2 * D_RWKV]
    v = fs[:, 2 * D_RWKV:3 * D_RWKV]
    o_w = 3 * D_RWKV
    o_a = o_w + 2 * DECAY_LORA
    o_g = o_a + 2 * AAA_LORA
    ones_bd = _head_ones(D_RWKV, HEAD_DIM)
    gg_ref[...] = _dot(_sigmoid(fs[:, o_g:o_g + GATE_LORA]).astype(bf16), g2_ref[...].astype(bf16))
    kk = k * kk_ref[...]
    kk = kk * lax.rsqrt(_dot_exact_rhs(kk * kk, ones_bd) + NORM_EPS)
    ksum = None
    for d in range(2):
        wd = fs[:, o_w + d * DECAY_LORA:o_w + (d + 1) * DECAY_LORA]
        ad = fs[:, o_a + d * AAA_LORA:o_a + (d + 1) * AAA_LORA]
        w_raw = w0_ref[d:d + 1, :] + _dot(jnp.tanh(wd).astype(bf16), w2_ref[d].astype(bf16))
        z = -w_raw
        softplus = jnp.maximum(z, 0.0) + jnp.log(1.0 + jnp.exp(-jnp.abs(z)))
        s_lw[d] = -jnp.exp(-softplus - 0.5)
        a = _sigmoid(a0_ref[d:d + 1, :] + _dot(ad.astype(bf16), a2_ref[d].astype(bf16)))
        kd = k * (1.0 + (a - 1.0) * ka_ref[...])
        ksum = kd if d == 0 else ksum + kd
        s_k[d] = kd
        s_b[d] = kk * a
    s_r[...] = r
    s_v[...] = v
    s_a[...] = -kk
    bv_ref[...] = _dot_exact_rhs(r * ksum * rk_ref[...], ones_bd) * v

    ri = lax.broadcasted_iota(jnp.int32, (CHUNK, CHUNK), 0)
    ci = lax.broadcasted_iota(jnp.int32, (CHUNK, CHUNK), 1)
    tri = tuple(jnp.where(msk, 1.0, 0.0).astype(bf16) for msk in (ci <= ri, ci >= ri))
    rp_i = lax.broadcasted_iota(jnp.int32, (CHUNK, PAIR), 0)
    cp_i = lax.broadcasted_iota(jnp.int32, (CHUNK, PAIR), 1) % CHUNK
    eye = jnp.where(rp_i == cp_i, 1.0, 0.0).astype(f32)
    strict = (cp_i < rp_i, cp_i > rp_i)
    incl = (cp_i <= rp_i, cp_i >= rp_i)
    low = lax.broadcasted_iota(jnp.int32, (CHUNK, PAIR), 1) < HEAD_DIM

    def chunk_body(ci, carry):
        cs = [ci * CHUNKS_PER_BODY + u for u in range(CHUNKS_PER_BODY)]
        rss = [pl.ds(pl.multiple_of(c * CHUNK, CHUNK), CHUNK) for c in cs]
        v16_c, at16, rt, bt16, kt16, wc = [], {}, {}, {}, {}, {}
        for u, rs in enumerate(rss):
            r_c = s_r[rs, :]
            a_c = s_a[rs, :]
            v16_c.append(s_v[rs, :].astype(bf16))
            for d in range(2):
                lw = s_lw[d, rs, :]
                cum = _dot_exact_lhs(tri[d], lw)
                er = jnp.exp(cum)
                ei = jnp.exp(-cum)
                at16[u, d] = (a_c * jnp.exp(cum - lw)).astype(bf16)
                rt[u, d] = r_c * er
                bt16[u, d] = (s_b[d, rs, :] * ei).astype(bf16)
                kt16[u, d] = (s_k[d, rs, :] * ei).astype(bf16)
                wc[u, d] = er[CHUNK - 1:CHUNK, :] if d == 0 else er[0:1, :]

        insts = [(u, d, p) for u in range(CHUNKS_PER_BODY) for d in range(2) for p in range(N_PAIR)]
        psl = lambda p: slice(p * PAIR, (p + 1) * PAIR)
        gms = [_dot_nt(jnp.concatenate([at16[u, d][:, psl(p)], rt[u, d][:, psl(p)].astype(bf16)], axis=0),
                       jnp.concatenate([_pair_blockdiag(bt16[u, d][:, psl(p)]),
                                        _pair_blockdiag(kt16[u, d][:, psl(p)])], axis=0)) for u, d, p in insts]
        lab = [jnp.where(strict[d], g[:CHUNK, :PAIR], 0.0) for (u, d, p), g in zip(insts, gms)]
        lakrk16 = [jnp.concatenate([jnp.where(strict[d], g[:CHUNK, PAIR:], 0.0),
                                    jnp.where(incl[d], g[CHUNK:, PAIR:], 0.0)], axis=0).astype(bf16)
                   for (u, d, p), g in zip(insts, gms)]
        lrb16 = [jnp.where(incl[d], g[CHUNK:, :PAIR], 0.0).astype(bf16) for (u, d, p), g in zip(insts, gms)]
        uv = [_dot(lk, _pair_blockdiag(v16_c[u][:, psl(p)])) for (u, d, p), lk in zip(insts, lakrk16)]
        vtk = [_dot_tn(v16_c[u][:, psl(p)], kt16[u, d][:, psl(p)]) for u, d, p in insts]
        def split(t):
            hi = t.astype(bf16)
            return hi, (t - hi.astype(f32)).astype(bf16)

        def mul_x3(a, b):
            a_hi, a_lo = split(a)
            b_hi, b_lo = split(b)
            w = b.shape[1]
            full = _dot(a_hi, _pair_blockdiag(jnp.concatenate([b_hi, b_lo], axis=1)))
            return full[:, :w] + full[:, w:] + _dot(a_lo, _pair_blockdiag(b_hi))

        n_stage = int(math.log2(CHUNK))
        x = [eye + l for l in lab]
        q = [mul_x3(l, l) for l in lab]
        for stage in range(1, n_stage):
            if stage < n_stage - 1:
                prod = [mul_x3(jnp.concatenate([xx, qq], axis=0), qq) for xx, qq in zip(x, q)]
                x = [xx + pr[:CHUNK] for xx, pr in zip(x, prod)]
                q = [pr[CHUNK:] for pr in prod]
            else:
                x = [xx + mul_x3(xx, qq) for xx, qq in zip(x, q)]
        z0 = [jnp.concatenate([at16[u, d][:, psl(p)].astype(f32), lv[:CHUNK]], axis=1)
              for (u, d, p), lv in zip(insts, uv)]
        au16 = [mul_x3(xx, zz).astype(bf16) for xx, zz in zip(x, z0)]
        ry = [_dot(lr, _pair_blockdiag(au)) for lr, au in zip(lrb16, au16)]
        mn = [_dot_tn(au, bt16[u, d][:, psl(p)]) for (u, d, p), au in zip(insts, au16)]
        for i, (u, d, p) in enumerate(insts):
            ps = psl(p)
            rp_ref[d, rss[u], ps] = rt[u, d][:, ps] + ry[i][:, :PAIR]
            y0_ref[d, rss[u], ps] = ry[i][:, PAIR:] + uv[i][CHUNK:]
            wc_p = wc[u, d][:, ps]
            m_pair = jnp.where(low, mn[i][0:CHUNK], mn[i][CHUNK:2 * CHUNK])
            n_pair = jnp.where(low, mn[i][2 * CHUNK:3 * CHUNK], mn[i][3 * CHUNK:])
            k_pair = jnp.where(low, vtk[i][:CHUNK], vtk[i][CHUNK:])
            mm_ref[d, cs[u], p] = (eye + m_pair) * wc_p
            nn_ref[d, cs[u], p] = (n_pair + k_pair) * wc_p
        return carry

    lax.fori_loop(0, TM // (CHUNK * CHUNKS_PER_BODY), chunk_body, 0)


def _rwkv_chunks(fr, mu, w0, w2, a0, a2, g2, kk, ka, rk, nbc, nbs):
    m = fr.shape[0]
    nct = TM // CHUNK
    prev, nxt = _halo_specs(RWKV_IN, m)
    c2 = lambda i: (0, 0)
    c3 = lambda i: (0, 0, 0)
    row = lambda i: (i, 0)
    dirrow = lambda i: (0, i, 0)
    ops = lambda i: (0, i, 0, 0, 0)
    return pl.pallas_call(
        functools.partial(_rwkv_chunk_kernel, nbc=nbc, nbs=nbs),
        grid=(m // TM,),
        in_specs=[pl.BlockSpec((TM, RWKV_IN), row), prev, nxt,
                  pl.BlockSpec((2, RWKV_IN), c2), pl.BlockSpec((2, D_RWKV), c2),
                  pl.BlockSpec((2, DECAY_LORA, D_RWKV), c3), pl.BlockSpec((2, D_RWKV), c2),
                  pl.BlockSpec((2, AAA_LORA, D_RWKV), c3), pl.BlockSpec((GATE_LORA, D_RWKV), c2),
                  pl.BlockSpec((1, D_RWKV), c2), pl.BlockSpec((1, D_RWKV), c2), pl.BlockSpec((1, D_RWKV), c2)],
        out_specs=[pl.BlockSpec((2, TM, D_RWKV), dirrow), pl.BlockSpec((2, TM, D_RWKV), dirrow),
                   pl.BlockSpec((2, nct, N_PAIR, HEAD_DIM, PAIR), ops),
                   pl.BlockSpec((2, nct, N_PAIR, HEAD_DIM, PAIR), ops),
                   pl.BlockSpec((TM, D_RWKV), row), pl.BlockSpec((TM, D_RWKV), row)],
        out_shape=[jax.ShapeDtypeStruct((2, m, D_RWKV), f32), jax.ShapeDtypeStruct((2, m, D_RWKV), f32),
                   jax.ShapeDtypeStruct((2, m // CHUNK, N_PAIR, HEAD_DIM, PAIR), f32),
                   jax.ShapeDtypeStruct((2, m // CHUNK, N_PAIR, HEAD_DIM, PAIR), f32),
                   jax.ShapeDtypeStruct((m, D_RWKV), f32), jax.ShapeDtypeStruct((m, D_RWKV), f32)],
        scratch_shapes=[pltpu.VMEM((TM, D_RWKV), f32), pltpu.VMEM((TM, D_RWKV), f32), pltpu.VMEM((TM, D_RWKV), f32),
                        pltpu.VMEM((2, TM, D_RWKV), f32), pltpu.VMEM((2, TM, D_RWKV), f32),
                        pltpu.VMEM((2, TM, D_RWKV), f32)],
        compiler_params=_cparams(1),
        name="rwkv_chunks",
    )(fr, fr, fr, mu, w0, w2, a0, a2, g2, kk.reshape(1, D_RWKV), ka.reshape(1, D_RWKV), rk.reshape(1, D_RWKV))


def _rwkv_state_kernel(mf_ref, mb_ref, nf_ref, nb_ref, sf_ref, sb_ref, st_ref, *, nbatch):
    @pl.when(pl.program_id(0) == 0)
    def _():
        st_ref[...] = jnp.zeros(st_ref.shape, f32)

    group = 8
    probs = [(d, b, p) for b in range(nbatch) for d in range(2) for p in range(N_PAIR)]
    for g0 in range(0, len(probs), group):
        grp = probs[g0:g0 + group]
        st = [st_ref[d, b, p] for d, b, p in grp]
        for (d, b, p), s in zip(grp, st):
            (sf_ref if d == 0 else sb_ref)[b, 0, p] = s
        mt = [(mf_ref if d == 0 else mb_ref)[0, b, 0, p] for d, b, p in grp]
        s1 = [s.astype(bf16) for s in st]
        s2 = [(s - a.astype(f32)).astype(bf16) for s, a in zip(st, s1)]
        m1 = [x.astype(bf16) for x in mt]
        m2 = [(x - a.astype(f32)).astype(bf16) for x, a in zip(mt, m1)]
        p12 = [_dot(a1, _pair_blockdiag(jnp.concatenate([b1, b2], axis=1))) for a1, b1, b2 in zip(s1, m1, m2)]
        p21 = [_dot(a2, _pair_blockdiag(b1)) for a2, b1 in zip(s2, m1)]
        for (d, b, p), x12, x21 in zip(grp, p12, p21):
            st_ref[d, b, p] = (x12[:, :PAIR] + x12[:, PAIR:] + x21) + (nf_ref if d == 0 else nb_ref)[0, b, 0, p]


def _rwkv_state(mm, nn, nbatch, nca, ncc):
    shp = (2, nbatch, nca, N_PAIR, HEAD_DIM, PAIR)
    blk = (1, nbatch, 1, N_PAIR, HEAD_DIM, PAIR)
    c_bwd = lambda s: jnp.where(s < ncc, ncc - 1 - s, nca - 1 - (s - ncc))
    fwd = pl.BlockSpec(blk, lambda s: (0, 0, s, 0, 0, 0))
    bwd = pl.BlockSpec(blk, lambda s: (1, 0, c_bwd(s), 0, 0, 0))
    oshape = jax.ShapeDtypeStruct(shp[1:], f32)
    return pl.pallas_call(
        functools.partial(_rwkv_state_kernel, nbatch=nbatch),
        grid=(nca,),
        in_specs=[fwd, bwd, fwd, bwd],
        out_specs=[pl.BlockSpec(blk[1:], lambda s: (0, s, 0, 0, 0)),
                   pl.BlockSpec(blk[1:], lambda s: (0, c_bwd(s), 0, 0, 0))],
        out_shape=[oshape, oshape],
        scratch_shapes=[pltpu.VMEM((2, nbatch, N_PAIR, HEAD_DIM, PAIR), f32)],
        compiler_params=_cparams(1),
        name="rwkv_state",
    )(mm.reshape(shp), mm.reshape(shp), nn.reshape(shp), nn.reshape(shp))


def _rwkv_readout(rp_ref, y0_ref, sf_ref, sb_ref, bv_ref, gg_ref, lng_ref, lnb_ref, n_rows):
    rows = []
    for c in range(n_rows // CHUNK):
        rs = slice(c * CHUNK, (c + 1) * CHUNK)
        cols = []
        for p in range(N_PAIR):
            ps = slice(p * PAIR, (p + 1) * PAIR)
            lhs = jnp.concatenate([rp_ref[0, rs, ps], rp_ref[1, rs, ps]], axis=1).astype(bf16)
            rhs = jnp.concatenate([_pair_blockdiag(sf_ref[c, p].astype(bf16)),
                                   _pair_blockdiag(sb_ref[c, p].astype(bf16))], axis=1)
            cols.append(y0_ref[0, rs, ps] + y0_ref[1, rs, ps] + _dot_nt(lhs, rhs))
        rows.append(jnp.concatenate(cols, axis=1))
    y = jnp.concatenate(rows, axis=0)
    ones_bd = _head_ones(D_RWKV, HEAD_DIM)
    mu = _dot_exact_rhs(y, ones_bd) * (1.0 / HEAD_DIM)
    yc = y - mu
    var = _dot_exact_rhs(yc * yc, ones_bd) * (1.0 / HEAD_DIM)
    yn = yc * lax.rsqrt(var + GN_EPS) * lng_ref[...] + lnb_ref[...]
    return (yn + bv_ref[...]) * gg_ref[...]


def _mixed_residual(t_ref, rp_ref, y0_ref, sf_ref, sb_ref, bv_ref, gg_ref, lng_ref, lnb_ref, b_ref, c_ref, w_ref, *, gate,
                    rows):
    a = _rwkv_readout(rp_ref, y0_ref, sf_ref, sb_ref, bv_ref, gg_ref, lng_ref, lnb_ref, rows)
    out = _dot(a.astype(bf16), w_ref[0:D_RWKV, :])
    out = out + _dot(b_ref[...].astype(bf16), w_ref[D_RWKV:D_RWKV + D_CONV, :])
    out = out + _dot(c_ref[...].astype(bf16), w_ref[D_RWKV + D_CONV:, :])
    return t_ref[...] + gate * out


def _mixout_kernel(*refs):
    m_ref, o_ref = refs[-2:]
    o_ref[...] = _mixed_residual(*refs[:-2], gate=m_ref[0, 5:6, :], rows=TM)


def _mixout(t, rwkv_parts, ln_g, ln_b, ob, oc, w, layer, mods, grid, row_map, mod_map, c_map, out_rows):
    rp, y0, sf, sb, bv, gg = rwkv_parts
    d = t.shape[1]
    nct = TM // CHUNK
    sshape = (bv.shape[0] // CHUNK, N_PAIR, HEAD_DIM, PAIR)
    n_axes = len(grid)
    if n_axes == 1:
        out_map = lambda i: (i, 0)
    else:
        nj = grid[1]
        out_map = lambda b, j: (b * nj + j, 0)
    rmap = lambda *a: (row_map(*a), 0)
    dirrow = lambda *a: (0, row_map(*a), 0)
    st = lambda *a: (row_map(*a), 0, 0, 0)
    c2 = lambda *a: (0, 0)
    return pl.pallas_call(
        _mixout_kernel,
        grid=grid,
        in_specs=[pl.BlockSpec((TM, d), rmap),
                  pl.BlockSpec((2, TM, D_RWKV), dirrow), pl.BlockSpec((2, TM, D_RWKV), dirrow),
                  pl.BlockSpec((nct, N_PAIR, HEAD_DIM, PAIR), st), pl.BlockSpec((nct, N_PAIR, HEAD_DIM, PAIR), st),
                  pl.BlockSpec((TM, D_RWKV), rmap), pl.BlockSpec((TM, D_RWKV), rmap),
                  pl.BlockSpec((1, D_RWKV), c2), pl.BlockSpec((1, D_RWKV), c2),
                  pl.BlockSpec((TM, D_CONV), rmap),
                  pl.BlockSpec((TM, D_DIFF), lambda *a: (c_map(*a), 0)),
                  pl.BlockSpec((None, D_RWKV + D_CONV + D_DIFF, d), lambda *a: (layer, 0, 0)),
                  pl.BlockSpec((1, N_MOD, d), lambda *a: (mod_map(*a), 0, 0))],
        out_specs=pl.BlockSpec((TM, d), out_map),
        out_shape=jax.ShapeDtypeStruct((out_rows, d), f32),
        compiler_params=_cparams(n_axes),
        name="mixout",
    )(t, rp, y0, sf.reshape(sshape), sb.reshape(sshape), bv, gg, ln_g.reshape(1, D_RWKV), ln_b.reshape(1, D_RWKV),
      ob, oc, w, mods)


def _rope_tables(n_ctx, n_lat):
    n_rows = n_lat // GRID_W
    row = jnp.repeat(jnp.arange(n_rows, dtype=jnp.int32), GRID_W)
    col = jnp.tile(jnp.arange(GRID_W, dtype=jnp.int32), n_rows)
    inv = 1.0 / (ROPE_THETA ** (jnp.arange(ROPE_FREQS, dtype=f32) * 2.0 / AXIS_DIM))
    ang = jnp.stack([row, col], axis=-1).astype(f32)[..., None] * inv
    cos, sin = jnp.cos(ang), jnp.sin(ang)
    cos64 = jnp.concatenate([cos[:, 0], cos[:, 0], cos[:, 1], cos[:, 1]], axis=-1)
    sin64 = jnp.concatenate([-sin[:, 0], sin[:, 0], -sin[:, 1], sin[:, 1]], axis=-1)
    reps = D_QK // DIFF_QK_DIM
    cos_t = jnp.concatenate([jnp.ones((n_ctx, D_QK), f32), jnp.tile(cos64, (1, reps))], axis=0)
    sin_t = jnp.concatenate([jnp.zeros((n_ctx, D_QK), f32), jnp.tile(sin64, (1, reps))], axis=0)
    return cos_t, sin_t


def kernel(x, c, ctx, c_ctx, ada_w, ada_b, norm_g, ffn_w_in, ffn_w_out, mix_w_in, mix_w_out, rwkv_mu, rwkv_w0, rwkv_w2,
           rwkv_a0, rwkv_a2, rwkv_g2, rwkv_kk, rwkv_ka, rwkv_rk, rwkv_ln_g, rwkv_ln_b, conv_dw_w, conv_dw_b, conv_ln_g,
           conv_ln_b, diff_lam, diff_norm_g, final_g):
    nb, n_lat, d = x.shape
    n_ctx = ctx.shape[1]
    depth = ada_w.shape[0]
    seq = n_ctx + n_lat
    assert n_ctx % TM == 0 and n_lat % TM == 0 and n_lat % GRID_W == 0 and seq % n_ctx == 0
    assert nb + 1 <= COND_ROWS
    nbc, nbx, nbs = n_ctx // TM, n_lat // TM, seq // TM
    m = nb * seq

    if n_ctx == TM and _dense_tile(seq) == 3 * TM:
        t = (x.reshape(nb * n_lat, d), ctx.reshape(nb * n_ctx, d))
    else:
        t = jnp.concatenate([ctx, x], axis=1).reshape(m, d)
    cond =jnp.zeros((COND_ROWS, d), f32).at[:nb].set(c).at[nb].set(c_ctx)
    mods = _adaln(cond, ada_w, ada_b).reshape(depth, COND_ROWS, N_MOD, d)
    cos_t, sin_t = _rope_tables(n_ctx, n_lat)
    w_in16 = ffn_w_in.astype(bf16)
    w_out16 = ffn_w_out.astype(bf16)
    mix_in16 = mix_w_in.astype(bf16)
    mix_out16 = mix_w_out.astype(bf16)

    mod_all = lambda i: jnp.where(i % nbs < nbc, nb, i // nbs)
    row_all = lambda i: i
    row_lat = lambda b, j: b * nbs + nbc + j
    mod_lat = lambda b, j: b

    out = None
    for l in range(depth):
        last = l == depth - 1
        ml = mods[l]
        t = _ffn(t, ml, norm_g[l, 0], w_in16, w_out16, l, 0, 0, seq, n_ctx, nb)
        fr, hc, q, k, v = _mixin(t, ml, norm_g[l, 1], mix_in16, l, cos_t, sin_t, seq, n_ctx, nb)
        rp, y0, mm, nn, bv, gg = _rwkv_chunks(fr, rwkv_mu[l], rwkv_w0[l], rwkv_w2[l], rwkv_a0[l], rwkv_a2[l], rwkv_g2[l],
                                              rwkv_kk[l], rwkv_ka[l], rwkv_rk[l].reshape(D_RWKV), nbc, nbs)
        sf, sb = _rwkv_state(mm, nn, nb, seq // CHUNK, n_ctx // CHUNK)
        rwkv_parts = (rp, y0, sf, sb, bv, gg)
        ob = _conv(hc, conv_dw_w[l], conv_dw_b[l], conv_ln_g[l], conv_ln_b[l], nbc, nbs)
        lam_init = 0.8 - 0.6 * math.exp(-0.3 * l)
        if last:
            oc = _attn(q, k, v, diff_lam[l], diff_norm_g[l], lam_init, nb, n_ctx, n_lat, False)
            t = _mixout(t, rwkv_parts, rwkv_ln_g[l], rwkv_ln_b[l], ob, oc, mix_out16, l, ml, (nb, nbx), row_lat,
                        mod_lat, lambda b, j: b * nbx + j, nb * n_lat)
            out = _ffn(t, ml, norm_g[l, 2], w_in16, w_out16, l, 1, 6, n_lat, 0, nb, final_g=final_g)
        else:
            oc = _attn(q, k, v, diff_lam[l], diff_norm_g[l], lam_init, nb, n_ctx, n_lat, True)
            t = _mixout(t, rwkv_parts, rwkv_ln_g[l], rwkv_ln_b[l], ob, oc, mix_out16, l, ml, (nbs * nb,), row_all,
                        mod_all, row_all, m)
            t = _ffn(t, ml, norm_g[l, 2], w_in16, w_out16, l, 1, 6, seq, n_ctx, nb)
    return out.reshape(nb, n_lat, d)
```

```python
import functools
import math

import jax
import jax.numpy as jnp
from jax import lax
from jax.experimental import pallas as pl
from jax.experimental.pallas import tpu as pltpu

f32 = jnp.float32
bf16 = jnp.bfloat16

N_MOD = 9
EPS = 1e-6
GRID_W = 64
RWKV_HEADS = 4
HEAD_DIM = 64
D_RWKV = RWKV_HEADS * HEAD_DIM
PAIR = 2 * HEAD_DIM
N_PAIR = RWKV_HEADS // 2
DECAY_LORA = 64
AAA_LORA = 64
GATE_LORA = 128
GN_EPS = 64e-5
NORM_EPS = 1e-12
D_CONV = 256
CONV_WIDTH = 31
CONV_HALF = CONV_WIDTH // 2
DIFF_HEADS = 4
DIFF_QK_DIM = 64
DIFF_V_DIM = 2 * DIFF_QK_DIM
D_DIFF = DIFF_HEADS * DIFF_V_DIM
ROPE_THETA = 10000.0
LOG2_E = 1.4426950408889634
AXIS_DIM = DIFF_QK_DIM // 2
ROPE_FREQS = AXIS_DIM // 2
RWKV_IN = 3 * D_RWKV + 2 * DECAY_LORA + 2 * AAA_LORA + GATE_LORA
CONV_IN = 2 * D_CONV
D_QK = DIFF_HEADS * 2 * DIFF_QK_DIM
DIFF_IN = 2 * D_QK + D_DIFF
P_IN = RWKV_IN + CONV_IN + DIFF_IN

TM = 256
CHUNK = 64
CHUNKS_PER_BODY = 4
STATE_STEPS = 2
HALO = 16
COND_ROWS = 16
VMEM_LIMIT = 56 * 1024 * 1024


def _dot(a, b, prec=None):
    return jnp.dot(a, b, preferred_element_type=f32, precision=prec)


def _dot_nt(a, b, prec=None):
    return lax.dot_general(a, b, (((1,), (1,)), ((), ())), preferred_element_type=f32, precision=prec)


def _dot_tn(a, b, prec=None):
    return lax.dot_general(a, b, (((0,), (0,)), ((), ())), preferred_element_type=f32, precision=prec)


def _split3(a):
    a1 = a.astype(bf16)
    r1 = a - a1.astype(f32)
    a2 = r1.astype(bf16)
    a3 = (r1 - a2.astype(f32)).astype(bf16)
    return a1, a2, a3


def _dot_exact_rhs(a, b16):
    a1, a2, a3 = _split3(a)
    return _dot(a1, b16) + _dot(a2, b16) + _dot(a3, b16)


def _dot_exact_lhs(a16, b):
    b1, b2, b3 = _split3(b)
    return _dot(a16, b1) + _dot(a16, b2) + _dot(a16, b3)


def _pair_blockdiag(x):
    first = (lax.broadcasted_iota(jnp.int32, x.shape, 1) % PAIR) < HEAD_DIM
    zero = jnp.zeros_like(x)
    return jnp.concatenate([jnp.where(first, x, zero), jnp.where(first, zero, x)], axis=0)


def _sigmoid(x):
    return 1.0 / (1.0 + jnp.exp(-x))


def _cparams(n_axes):
    return pltpu.CompilerParams(dimension_semantics=("arbitrary",) * n_axes, vmem_limit_bytes=VMEM_LIMIT)


def _head_ones(n, width):
    r = lax.broadcasted_iota(jnp.int32, (n, n), 0) // width
    c = lax.broadcasted_iota(jnp.int32, (n, n), 1) // width
    return jnp.where(r == c, 1.0, 0.0).astype(bf16)


def _adaln_kernel(c_ref, w_ref, b_ref, o_ref):
    cond = c_ref[...]
    cond = cond * _sigmoid(cond)
    o_ref[0] = _dot(cond.astype(bf16), w_ref[0].astype(bf16)) + b_ref[0]


def _adaln(cond, ada_w, ada_b):
    depth, d, nd = ada_w.shape
    tn = nd // 4
    return pl.pallas_call(
        _adaln_kernel,
        grid=(depth, nd // tn),
        in_specs=[pl.BlockSpec((COND_ROWS, d), lambda l, j: (0, 0)),
                  pl.BlockSpec((1, d, tn), lambda l, j: (l, 0, j)),
                  pl.BlockSpec((1, 1, tn), lambda l, j: (l, 0, j))],
        out_specs=pl.BlockSpec((1, COND_ROWS, tn), lambda l, j: (l, 0, j)),
        out_shape=jax.ShapeDtypeStruct((depth, COND_ROWS, nd), f32),
        compiler_params=_cparams(2),
        name="adaln",
    )(cond, ada_w, ada_b.reshape(depth, 1, nd))


def _modulated_norm(x, g, shift, scale):
    y = x * lax.rsqrt(jnp.mean(x * x, axis=-1, keepdims=True) + EPS) * g
    return y * (1.0 + scale) + shift


def _dense_tile(rows_per_seq):
    for tm in (1024, 768, 512, 256):
        if rows_per_seq % tm == 0:
            return tm
    raise ValueError(rows_per_seq)


def _mod_row(mb_ref, mc_ref, idx, tm, n_ctx, blocks_per_seq):
    if n_ctx == 0:
        return mb_ref[0, idx:idx + 1, :]
    row = (pl.program_id(0) % blocks_per_seq) * tm + lax.broadcasted_iota(jnp.int32, (tm, 1), 0)
    return jnp.where(row < n_ctx, mc_ref[0, idx:idx + 1, :], mb_ref[0, idx:idx + 1, :])


def _ffn_kernel(*refs, source, s0, ff, fc, final, tm, n_ctx, bps):
    n_src = {"rows": 1, "split": 4}[source]
    src, (mb_ref, mc_ref, g_ref, wi_ref, wo_ref), rest = refs[:n_src], refs[n_src:n_src + 5], refs[n_src + 5:]
    o_ref = rest[-1]
    mod = lambda idx: _mod_row(mb_ref, mc_ref, idx, tm, n_ctx, bps)
    if source == "rows":
        x = src[0][...]
    else:
        first = pl.program_id(0) % bps == 0
        x = jnp.concatenate([jnp.where(first, src[0][...], src[1][...]), src[2][...], src[3][...]], axis=0)
    h = _modulated_norm(x, g_ref[...], mod(s0), mod(s0 + 1)).astype(bf16)
    acc = jnp.zeros(x.shape, f32)
    for j in range(ff // fc):
        gt = _dot(h, wi_ref[:, j * fc:(j + 1) * fc])
        up = _dot(h, wi_ref[:, ff + j * fc:ff + (j + 1) * fc])
        act = (gt * _sigmoid(gt) * up).astype(bf16)
        acc = acc + _dot(act, wo_ref[j * fc:(j + 1) * fc, :])
    out = x + 0.5 * mod(s0 + 2) * acc
    if final:
        fg_ref = rest[0]
        out = out * lax.rsqrt(jnp.mean(out * out, axis=-1, keepdims=True) + EPS) * fg_ref[...]
    o_ref[...] = out


def _ffn(t, mods, g, w_in, w_out, layer, slot, s0, rows_per_seq, n_ctx, ctx_row, final_g=None):
    ff = w_out.shape[2]
    fc = 256 if ff % 256 == 0 else ff
    tm = _dense_tile(rows_per_seq)
    bps = rows_per_seq // tm
    const = lambda i: (0, 0)
    pick = lambda i: (layer, slot, 0, 0)
    row = lambda i: (i, 0)
    if isinstance(t, tuple):
        lat, ctx = t
        d = lat.shape[1]
        assert n_ctx == TM and tm == 3 * TM
        m = lat.shape[0] + ctx.shape[0]
        per = (rows_per_seq - n_ctx) // TM
        lat_blk = lambda u: pl.BlockSpec(
            (TM, d), lambda i: ((i // bps) * per + jnp.maximum(3 * (i % bps) + u - 1, 0), 0))
        source, src_specs, src_args = "split", [pl.BlockSpec((TM, d), lambda i: (i // bps, 0)),
                                                lat_blk(0), lat_blk(1), lat_blk(2)], [ctx, lat, lat, lat]
    else:
        m, d = t.shape
        source, src_specs, src_args = "rows", [pl.BlockSpec((tm, d), row)], [t]
    in_specs = src_specs + [pl.BlockSpec((1, N_MOD, d), lambda i: (i // bps, 0, 0)),
                            pl.BlockSpec((1, N_MOD, d), lambda i: (ctx_row, 0, 0)),
                            pl.BlockSpec((1, d), const),
                            pl.BlockSpec((None, None, d, 2 * ff), pick, pipeline_mode=pl.Buffered(1)),
                            pl.BlockSpec((None, None, ff, d), pick, pipeline_mode=pl.Buffered(1))]
    args = src_args + [mods, mods, g.reshape(1, d), w_in, w_out]
    if final_g is not None:
        in_specs.append(pl.BlockSpec((1, d), const))
        args.append(final_g.reshape(1, d))
    return pl.pallas_call(
        functools.partial(_ffn_kernel, source=source, s0=s0, ff=ff, fc=fc, final=final_g is not None, tm=tm,
                          n_ctx=n_ctx, bps=bps),
        grid=(m // tm,), in_specs=in_specs,
        out_specs=pl.BlockSpec((tm, d), lambda i: (i, 0)),
        out_shape=jax.ShapeDtypeStruct((m, d), f32),
        compiler_params=_cparams(1),
        name="ffn",
    )(*args)


def _mixin_kernel(x_ref, mb_ref, mc_ref, g_ref, w_ref, cos_ref, sin_ref, fr_ref, hc_ref, q_ref, k_ref, v_ref, *, tm,
                  n_ctx, bps):
    x = x_ref[...]
    mod = lambda idx: _mod_row(mb_ref, mc_ref, idx, tm, n_ctx, bps)
    h = _modulated_norm(x, g_ref[...], mod(3), mod(4)).astype(bf16)
    fr_ref[...] = _dot(h, w_ref[:, 0:RWKV_IN])
    o = RWKV_IN
    val = _dot(h, w_ref[:, o:o + D_CONV])
    gate = _dot(h, w_ref[:, o + D_CONV:o + CONV_IN])
    hc_ref[...] = val * _sigmoid(gate)
    o = RWKV_IN + CONV_IN
    cos = cos_ref[...]
    sin = sin_ref[...]
    lane = lax.broadcasted_iota(jnp.int32, cos.shape, 1)
    low_half = ((lane // ROPE_FREQS) % 2) == 0

    def rope(t):
        partner = jnp.where(low_half, pltpu.roll(t, D_QK - ROPE_FREQS, 1), pltpu.roll(t, ROPE_FREQS, 1))
        return t * cos + partner * sin

    q = _dot(h, w_ref[:, o:o + D_QK])
    q_ref[...] = (rope(q) * (DIFF_QK_DIM ** -0.5 * LOG2_E)).astype(bf16)
    k = _dot(h, w_ref[:, o + D_QK:o + 2 * D_QK])
    k_ref[...] = rope(k).astype(bf16)
    v_ref[...] = _dot(h, w_ref[:, o + 2 * D_QK:o + 2 * D_QK + D_DIFF]).astype(bf16)


def _mixin(t, mods, g, w, layer, cos, sin, seq, n_ctx, ctx_row):
    m, d = t.shape
    tm = _dense_tile(seq)
    bps = seq // tm
    const = lambda i: (0, 0)
    row = lambda i: (i, 0)
    return pl.pallas_call(
        functools.partial(_mixin_kernel, tm=tm, n_ctx=n_ctx, bps=bps),
        grid=(m // tm,),
        in_specs=[pl.BlockSpec((tm, d), row),
                  pl.BlockSpec((1, N_MOD, d), lambda i: (i // bps, 0, 0)),
                  pl.BlockSpec((1, N_MOD, d), lambda i: (ctx_row, 0, 0)),
                  pl.BlockSpec((1, d), const),
                  pl.BlockSpec((None, d, P_IN), lambda i: (layer, 0, 0), pipeline_mode=pl.Buffered(1)),
                  pl.BlockSpec((tm, D_QK), lambda i: (i % bps, 0)),
                  pl.BlockSpec((tm, D_QK), lambda i: (i % bps, 0))],
        out_specs=[pl.BlockSpec((tm, RWKV_IN), row), pl.BlockSpec((tm, D_CONV), row),
                   pl.BlockSpec((tm, D_QK), row), pl.BlockSpec((tm, D_QK), row), pl.BlockSpec((tm, D_DIFF), row)],
        out_shape=[jax.ShapeDtypeStruct((m, RWKV_IN), f32), jax.ShapeDtypeStruct((m, D_CONV), f32),
                   jax.ShapeDtypeStruct((m, D_QK), bf16), jax.ShapeDtypeStruct((m, D_QK), bf16),
                   jax.ShapeDtypeStruct((m, D_DIFF), bf16)],
        compiler_params=_cparams(1),
        name="mixin",
    )(t, mods, mods, g.reshape(1, d), w, cos, sin)


def _attn_kernel(q_ref, k_ref, v_ref, lam_ref, g_ref, o_ref, vaug_ref, *, lam_init, n_ctx, n_lat, tq, unroll,
                 with_ctx):
    lv = lam_ref[...]
    lam = (jnp.exp(jnp.sum(lv[0:1] * lv[1:2], keepdims=True)) - jnp.exp(jnp.sum(lv[2:3] * lv[3:4], keepdims=True))
           + lam_init)

    @pl.when(jnp.logical_and(pl.program_id(0) == 0, pl.program_id(1) == 0))
    def _():
        vaug_ref[:, DIFF_V_DIM:] = jnp.ones((vaug_ref.shape[0], DIFF_V_DIM), bf16)

    vaug_ref[:, :DIFF_V_DIM] = v_ref[...]

    def attend(q, k, v):
        parts = []
        for m in range(2):
            sl = slice(m * DIFF_QK_DIM, (m + 1) * DIFF_QK_DIM)
            s = _dot_nt(q[:, sl], k[:, sl])
            e = jnp.exp2(s - jnp.max(s, axis=-1, keepdims=True))
            pv = _dot(e.astype(bf16), v)
            parts.append(pv[:, :DIFF_V_DIM] * (1.0 / pv[:, DIFF_V_DIM:]))
        o = parts[0] - lam * parts[1]
        o = o * lax.rsqrt(jnp.mean(o * o, axis=-1, keepdims=True) + 1e-5) * g_ref[...]
        return (o * (1.0 - lam_init)).astype(bf16)

    out_off = 0
    if with_ctx:
        o_ref[0:n_ctx, :] = attend(q_ref[0:n_ctx, :], k_ref[0:n_ctx, :], vaug_ref[0:n_ctx, :])
        out_off = n_ctx
    align = math.gcd(n_ctx, tq)

    def body(i, carry):
        for u in range(unroll):
            t = i * unroll + u
            q_rows = pl.ds(pl.multiple_of(n_ctx + t * tq, align), tq)
            o_rows = pl.ds(pl.multiple_of(out_off + t * tq, align), tq)
            o_ref[o_rows, :] = attend(q_ref[q_rows, :], k_ref[...], vaug_ref[...])
        return carry

    lax.fori_loop(0, n_lat // (tq * unroll), body, 0)


def _attn(q, k, v, lam_vecs, norm_g, lam_init, nbatch, n_ctx, n_lat, with_ctx):
    seq = n_ctx + n_lat
    tq = TM
    unroll = 8 if n_lat % (8 * tq) == 0 else 1
    out_seq = seq if with_ctx else n_lat
    blk = lambda b, h: (b, h)
    return pl.pallas_call(
        functools.partial(_attn_kernel, lam_init=lam_init, n_ctx=n_ctx, n_lat=n_lat, tq=tq, unroll=unroll,
                          with_ctx=with_ctx),
        grid=(nbatch, DIFF_HEADS),
        in_specs=[pl.BlockSpec((seq, DIFF_V_DIM), blk), pl.BlockSpec((seq, DIFF_V_DIM), blk),
                  pl.BlockSpec((seq, DIFF_V_DIM), blk),
                  pl.BlockSpec((4, DIFF_QK_DIM), lambda b, h: (0, 0)),
                  pl.BlockSpec((1, DIFF_V_DIM), lambda b, h: (0, 0))],
        out_specs=pl.BlockSpec((out_seq, DIFF_V_DIM), blk),
        out_shape=jax.ShapeDtypeStruct((nbatch * out_seq, D_DIFF), bf16),
        scratch_shapes=[pltpu.VMEM((seq, 2 * DIFF_V_DIM), bf16)],
        compiler_params=_cparams(2),
        name="diffattn",
    )(q, k, v, lam_vecs, norm_g.reshape(1, DIFF_V_DIM))


def _segment_flags(i, nbc, nbs):
    j = i % nbs
    first = jnp.logical_or(j == 0, j == nbc)
    last = jnp.logical_or(j == nbc - 1, j == nbs - 1)
    return first, last


def _halo_specs(width, n_rows):
    per = TM // HALO
    nblk = n_rows // HALO
    prev = pl.BlockSpec((HALO, width), lambda i: (jnp.maximum(i * per - 1, 0), 0))
    nxt = pl.BlockSpec((HALO, width), lambda i: (jnp.minimum((i + 1) * per, nblk - 1), 0))
    return prev, nxt


def _conv_kernel(h_ref, hp_ref, hn_ref, w_ref, b_ref, g_ref, bb_ref, o_ref, buf, shf, *, nbc, nbs):
    first, last = _segment_flags(pl.program_id(0), nbc, nbs)
    buf[0:HALO, :] = jnp.where(first, 0.0, hp_ref[...])
    buf[HALO:HALO + TM, :] = h_ref[...]
    buf[HALO + TM:HALO + TM + HALO, :] = jnp.where(last, 0.0, hn_ref[...])
    sub = 64
    sublanes = 8
    first_tap = HALO - CONV_HALF
    for r0 in range(0, TM, sub):
        acc = jnp.zeros((sub, D_CONV), f32) + b_ref[...]
        for r in range(sublanes):
            taps = [kk for kk in range(CONV_WIDTH) if (first_tap + kk) % sublanes == r]
            if not taps:
                continue
            span = sub + sublanes * max((first_tap + kk) // sublanes for kk in taps)
            shf[0:span, :] = buf[r0 + r:r0 + r + span, :]
            for kk in taps:
                a = sublanes * ((first_tap + kk) // sublanes)
                acc = acc + shf[a:a + sub, :] * w_ref[kk:kk + 1, :]
        mu = jnp.mean(acc, axis=-1, keepdims=True)
        xc = acc - mu
        var = jnp.mean(xc * xc, axis=-1, keepdims=True)
        y = xc * lax.rsqrt(var + 1e-5) * g_ref[...] + bb_ref[...]
        o_ref[r0:r0 + sub, :] = (y * _sigmoid(y)).astype(bf16)


def _conv(hc, dw_w, dw_b, ln_g, ln_b, nbc, nbs):
    m = hc.shape[0]
    prev, nxt = _halo_specs(D_CONV, m)
    const = lambda i: (0, 0)
    return pl.pallas_call(
        functools.partial(_conv_kernel, nbc=nbc, nbs=nbs),
        grid=(m // TM,),
        in_specs=[pl.BlockSpec((TM, D_CONV), lambda i: (i, 0)), prev, nxt,
                  pl.BlockSpec((CONV_WIDTH, D_CONV), const), pl.BlockSpec((1, D_CONV), const),
                  pl.BlockSpec((1, D_CONV), const), pl.BlockSpec((1, D_CONV), const)],
        out_specs=pl.BlockSpec((TM, D_CONV), lambda i: (i, 0)),
        out_shape=jax.ShapeDtypeStruct((m, D_CONV), bf16),
        scratch_shapes=[pltpu.VMEM((TM + 2 * HALO, D_CONV), f32), pltpu.VMEM((64 + 2 * HALO, D_CONV), f32)],
        compiler_params=_cparams(1),
        name="convmod",
    )(hc, hc, hc, dw_w, dw_b.reshape(1, D_CONV), ln_g.reshape(1, D_CONV), ln_b.reshape(1, D_CONV))


def _rwkv_chunk_kernel(f_ref, fp_ref, fn_ref, mu_ref, w0_ref, w2_ref, a0_ref, a2_ref, g2_ref, kk_ref, ka_ref, rk_ref,
                       rp_ref, y0_ref, mm_ref, nn_ref, bv_ref, gg_ref,
                       s_r, s_v, s_a, s_lw, s_k, s_b, *, nbc, nbs):
    first, last = _segment_flags(pl.program_id(0), nbc, nbs)
    f = f_ref[...]
    prow = jnp.where(first, 0.0, fp_ref[HALO - 1:HALO, :])
    nrow = jnp.where(last, 0.0, fn_ref[0:1, :])
    rows = lax.broadcasted_iota(jnp.int32, f.shape, 0)
    prev = jnp.where(rows == 0, prow, pltpu.roll(f, 1, 0))
    nxt = jnp.where(rows == TM - 1, nrow, pltpu.roll(f, TM - 1, 0))
    fs = f + mu_ref[0:1, :] * (prev - f) + mu_ref[1:2, :] * (nxt - f)

    r = fs[:, 0:D_RWKV]
    k = fs[:, D_RWKV:2 * D_RWKV]
    v = fs[:, 2 * D_RWKV:3 * D_RWKV]
    o_w = 3 * D_RWKV
    o_a = o_w + 2 * DECAY_LORA
    o_g = o_a + 2 * AAA_LORA
    ones_bd = _head_ones(D_RWKV, HEAD_DIM)
    gg_ref[...] = _dot(_sigmoid(fs[:, o_g:o_g + GATE_LORA]).astype(bf16), g2_ref[...].astype(bf16))
    kk = k * kk_ref[...]
    kk = kk * lax.rsqrt(_dot_exact_rhs(kk * kk, ones_bd) + NORM_EPS)
    ksum = None
    for d in range(2):
        wd = fs[:, o_w + d * DECAY_LORA:o_w + (d + 1) * DECAY_LORA]
        ad = fs[:, o_a + d * AAA_LORA:o_a + (d + 1) * AAA_LORA]
        w_raw = w0_ref[d:d + 1, :] + _dot(jnp.tanh(wd).astype(bf16), w2_ref[d].astype(bf16))
        z = -w_raw
        softplus = jnp.maximum(z, 0.0) + jnp.log(1.0 + jnp.exp(-jnp.abs(z)))
        s_lw[d] = -jnp.exp(-softplus - 0.5)
        a = _sigmoid(a0_ref[d:d + 1, :] + _dot(ad.astype(bf16), a2_ref[d].astype(bf16)))
        kd = k * (1.0 + (a - 1.0) * ka_ref[...])
        ksum = kd if d == 0 else ksum + kd
        s_k[d] = kd
        s_b[d] = kk * a
    s_r[...] = r
    s_v[...] = v
    s_a[...] = -kk
    bv_ref[...] = _dot_exact_rhs(r * ksum * rk_ref[...], ones_bd) * v

    ri = lax.broadcasted_iota(jnp.int32, (CHUNK, CHUNK), 0)
    ci = lax.broadcasted_iota(jnp.int32, (CHUNK, CHUNK), 1)
    tri = tuple(jnp.where(msk, 1.0, 0.0).astype(bf16) for msk in (ci <= ri, ci >= ri))
    rp_i = lax.broadcasted_iota(jnp.int32, (CHUNK, PAIR), 0)
    cp_i = lax.broadcasted_iota(jnp.int32, (CHUNK, PAIR), 1) % CHUNK
    eye = jnp.where(rp_i == cp_i, 1.0, 0.0).astype(f32)
    strict = (cp_i < rp_i, cp_i > rp_i)
    incl = (cp_i <= rp_i, cp_i >= rp_i)
    low = lax.broadcasted_iota(jnp.int32, (CHUNK, PAIR), 1) < HEAD_DIM

    def chunk_body(ci, carry):
        cs = [ci * CHUNKS_PER_BODY + u for u in range(CHUNKS_PER_BODY)]
        rss = [pl.ds(pl.multiple_of(c * CHUNK, CHUNK), CHUNK) for c in cs]
        v16_c, at16, rt, bt16, kt16, wc = [], {}, {}, {}, {}, {}
        for u, rs in enumerate(rss):
            r_c = s_r[rs, :]
            a_c = s_a[rs, :]
            v16_c.append(s_v[rs, :].astype(bf16))
            for d in range(2):
                lw = s_lw[d, rs, :]
                cum = _dot_exact_lhs(tri[d], lw)
                er = jnp.exp(cum)
                ei = jnp.exp(-cum)
                at16[u, d] = (a_c * jnp.exp(cum - lw)).astype(bf16)
                rt[u, d] = r_c * er
                bt16[u, d] = (s_b[d, rs, :] * ei).astype(bf16)
                kt16[u, d] = (s_k[d, rs, :] * ei).astype(bf16)
                wc[u, d] = er[CHUNK - 1:CHUNK, :] if d == 0 else er[0:1, :]

        insts = [(u, d, p) for u in range(CHUNKS_PER_BODY) for d in range(2) for p in range(N_PAIR)]
        psl = lambda p: slice(p * PAIR, (p + 1) * PAIR)
        gms = [_dot_nt(jnp.concatenate([at16[u, d][:, psl(p)], rt[u, d][:, psl(p)].astype(bf16)], axis=0),
                       jnp.concatenate([_pair_blockdiag(bt16[u, d][:, psl(p)]),
                                        _pair_blockdiag(kt16[u, d][:, psl(p)])], axis=0)) for u, d, p in insts]
        lab = [jnp.where(strict[d], g[:CHUNK, :PAIR], 0.0) for (u, d, p), g in zip(insts, gms)]
        lakrk16 = [jnp.concatenate([jnp.where(strict[d], g[:CHUNK, PAIR:], 0.0),
                                    jnp.where(incl[d], g[CHUNK:, PAIR:], 0.0)], axis=0).astype(bf16)
                   for (u, d, p), g in zip(insts, gms)]
        lrb16 = [jnp.where(incl[d], g[CHUNK:, :PAIR], 0.0).astype(bf16) for (u, d, p), g in zip(insts, gms)]
        uv = [_dot(lk, _pair_blockdiag(v16_c[u][:, psl(p)])) for (u, d, p), lk in zip(insts, lakrk16)]
        vtk = [_dot_tn(v16_c[u][:, psl(p)], kt16[u, d][:, psl(p)]) for u, d, p in insts]
        def split(t):
            hi = t.astype(bf16)
            return hi, (t - hi.astype(f32)).astype(bf16)

        def mul_x3(a, b):
            a_hi, a_lo = split(a)
            b_hi, b_lo = split(b)
            w = b.shape[1]
            full = _dot(a_hi, _pair_blockdiag(jnp.concatenate([b_hi, b_lo], axis=1)))
            return full[:, :w] + full[:, w:] + _dot(a_lo, _pair_blockdiag(b_hi))

        n_stage = int(math.log2(CHUNK))
        x = [eye + l for l in lab]
        q = [mul_x3(l, l) for l in lab]
        for stage in range(1, n_stage):
            if stage < n_stage - 1:
                prod = [mul_x3(jnp.concatenate([xx, qq], axis=0), qq) for xx, qq in zip(x, q)]
                x = [xx + pr[:CHUNK] for xx, pr in zip(x, prod)]
                q = [pr[CHUNK:] for pr in prod]
            else:
                x = [xx + mul_x3(xx, qq) for xx, qq in zip(x, q)]
        z0 = [jnp.concatenate([at16[u, d][:, psl(p)].astype(f32), lv[:CHUNK]], axis=1)
              for (u, d, p), lv in zip(insts, uv)]
        au16 = [mul_x3(xx, zz).astype(bf16) for xx, zz in zip(x, z0)]
        ry = [_dot(lr, _pair_blockdiag(au)) for lr, au in zip(lrb16, au16)]
        mn = [_dot_tn(au, bt16[u, d][:, psl(p)]) for (u, d, p), au in zip(insts, au16)]
        for i, (u, d, p) in enumerate(insts):
            ps = psl(p)
            rp_ref[d, rss[u], ps] = (rt[u, d][:, ps] + ry[i][:, :PAIR]).astype(bf16)
            y0_ref[d, rss[u], ps] = ry[i][:, PAIR:] + uv[i][CHUNK:]
            wc_p = wc[u, d][:, ps]
            m_pair = jnp.where(low, mn[i][0:CHUNK], mn[i][CHUNK:2 * CHUNK])
            n_pair = jnp.where(low, mn[i][2 * CHUNK:3 * CHUNK], mn[i][3 * CHUNK:])
            k_pair = jnp.where(low, vtk[i][:CHUNK], vtk[i][CHUNK:])
            mm_ref[d, cs[u], p] = (eye + m_pair) * wc_p
            nn_ref[d, cs[u], p] = (n_pair + k_pair) * wc_p
        return carry

    lax.fori_loop(0, TM // (CHUNK * CHUNKS_PER_BODY), chunk_body, 0)


def _rwkv_chunks(fr, mu, w0, w2, a0, a2, g2, kk, ka, rk, nbc, nbs):
    m = fr.shape[0]
    nct = TM // CHUNK
    prev, nxt = _halo_specs(RWKV_IN, m)
    c2 = lambda i: (0, 0)
    c3 = lambda i: (0, 0, 0)
    row = lambda i: (i, 0)
    dirrow = lambda i: (0, i, 0)
    ops = lambda i: (0, i, 0, 0, 0)
    return pl.pallas_call(
        functools.partial(_rwkv_chunk_kernel, nbc=nbc, nbs=nbs),
        grid=(m // TM,),
        in_specs=[pl.BlockSpec((TM, RWKV_IN), row), prev, nxt,
                  pl.BlockSpec((2, RWKV_IN), c2), pl.BlockSpec((2, D_RWKV), c2),
                  pl.BlockSpec((2, DECAY_LORA, D_RWKV), c3), pl.BlockSpec((2, D_RWKV), c2),
                  pl.BlockSpec((2, AAA_LORA, D_RWKV), c3), pl.BlockSpec((GATE_LORA, D_RWKV), c2),
                  pl.BlockSpec((1, D_RWKV), c2), pl.BlockSpec((1, D_RWKV), c2), pl.BlockSpec((1, D_RWKV), c2)],
        out_specs=[pl.BlockSpec((2, TM, D_RWKV), dirrow), pl.BlockSpec((2, TM, D_RWKV), dirrow),
                   pl.BlockSpec((2, nct, N_PAIR, HEAD_DIM, PAIR), ops),
                   pl.BlockSpec((2, nct, N_PAIR, HEAD_DIM, PAIR), ops),
                   pl.BlockSpec((TM, D_RWKV), row), pl.BlockSpec((TM, D_RWKV), row)],
        out_shape=[jax.ShapeDtypeStruct((2, m, D_RWKV), bf16), jax.ShapeDtypeStruct((2, m, D_RWKV), f32),
                   jax.ShapeDtypeStruct((2, m // CHUNK, N_PAIR, HEAD_DIM, PAIR), f32),
                   jax.ShapeDtypeStruct((2, m // CHUNK, N_PAIR, HEAD_DIM, PAIR), f32),
                   jax.ShapeDtypeStruct((m, D_RWKV), f32), jax.ShapeDtypeStruct((m, D_RWKV), f32)],
        scratch_shapes=[pltpu.VMEM((TM, D_RWKV), f32), pltpu.VMEM((TM, D_RWKV), f32), pltpu.VMEM((TM, D_RWKV), f32),
                        pltpu.VMEM((2, TM, D_RWKV), f32), pltpu.VMEM((2, TM, D_RWKV), f32),
                        pltpu.VMEM((2, TM, D_RWKV), f32)],
        compiler_params=_cparams(1),
        name="rwkv_chunks",
    )(fr, fr, fr, mu, w0, w2, a0, a2, g2, kk.reshape(1, D_RWKV), ka.reshape(1, D_RWKV), rk.reshape(1, D_RWKV))


def _rwkv_state_kernel(mf_ref, mb_ref, nf_ref, nb_ref, sf_ref, sb_ref, st_ref, *, nbatch):
    @pl.when(pl.program_id(0) == 0)
    def _():
        st_ref[...] = jnp.zeros(st_ref.shape, f32)

    group = 8
    probs = [(d, b, p) for b in range(nbatch) for d in range(2) for p in range(N_PAIR)]
    for sub in range(STATE_STEPS):
        pos = (sub, STATE_STEPS - 1 - sub)
        for g0 in range(0, len(probs), group):
            grp = probs[g0:g0 + group]
            st = [st_ref[d, b, p] for d, b, p in grp]
            for (d, b, p), s in zip(grp, st):
                (sf_ref if d == 0 else sb_ref)[b, pos[d], p] = s.astype(bf16)
            mt = [(mf_ref if d == 0 else mb_ref)[0, b, pos[d], p] for d, b, p in grp]
            s1 = [s.astype(bf16) for s in st]
            s2 = [(s - a.astype(f32)).astype(bf16) for s, a in zip(st, s1)]
            m1 = [x.astype(bf16) for x in mt]
            m2 = [(x - a.astype(f32)).astype(bf16) for x, a in zip(mt, m1)]
            p12 = [_dot(a1, _pair_blockdiag(jnp.concatenate([b1, b2], axis=1))) for a1, b1, b2 in zip(s1, m1, m2)]
            p21 = [_dot(a2, _pair_blockdiag(b1)) for a2, b1 in zip(s2, m1)]
            for (d, b, p), x12, x21 in zip(grp, p12, p21):
                st_ref[d, b, p] = ((x12[:, :PAIR] + x12[:, PAIR:] + x21)
                                   + (nf_ref if d == 0 else nb_ref)[0, b, pos[d], p])


def _rwkv_state(mm, nn, nbatch, nca, ncc):
    assert nca % STATE_STEPS == 0 and ncc % STATE_STEPS == 0
    shp = (2, nbatch, nca, N_PAIR, HEAD_DIM, PAIR)
    blk = (1, nbatch, STATE_STEPS, N_PAIR, HEAD_DIM, PAIR)
    nga, ngc = nca // STATE_STEPS, ncc // STATE_STEPS
    c_bwd = lambda s: jnp.where(s < ngc, ngc - 1 - s, nga - 1 - (s - ngc))
    fwd = pl.BlockSpec(blk, lambda s: (0, 0, s, 0, 0, 0))
    bwd = pl.BlockSpec(blk, lambda s: (1, 0, c_bwd(s), 0, 0, 0))
    oshape = jax.ShapeDtypeStruct(shp[1:], bf16)
    return pl.pallas_call(
        functools.partial(_rwkv_state_kernel, nbatch=nbatch),
        grid=(nga,),
        in_specs=[fwd, bwd, fwd, bwd],
        out_specs=[pl.BlockSpec(blk[1:], lambda s: (0, s, 0, 0, 0)),
                   pl.BlockSpec(blk[1:], lambda s: (0, c_bwd(s), 0, 0, 0))],
        out_shape=[oshape, oshape],
        scratch_shapes=[pltpu.VMEM((2, nbatch, N_PAIR, HEAD_DIM, PAIR), f32)],
        compiler_params=_cparams(1),
        name="rwkv_state",
    )(mm.reshape(shp), mm.reshape(shp), nn.reshape(shp), nn.reshape(shp))


def _rwkv_readout(rp_ref, y0_ref, sf_ref, sb_ref, bv_ref, gg_ref, lng_ref, lnb_ref, n_rows):
    rows = []
    for c in range(n_rows // CHUNK):
        rs = slice(c * CHUNK, (c + 1) * CHUNK)
        cols = []
        for p in range(N_PAIR):
            ps = slice(p * PAIR, (p + 1) * PAIR)
            lhs = jnp.concatenate([rp_ref[0, rs, ps], rp_ref[1, rs, ps]], axis=1).astype(bf16)
            rhs = jnp.concatenate([_pair_blockdiag(sf_ref[c, p].astype(bf16)),
                                   _pair_blockdiag(sb_ref[c, p].astype(bf16))], axis=1)
            cols.append(y0_ref[0, rs, ps] + y0_ref[1, rs, ps] + _dot_nt(lhs, rhs))
        rows.append(jnp.concatenate(cols, axis=1))
    y = jnp.concatenate(rows, axis=0)
    ones_bd = _head_ones(D_RWKV, HEAD_DIM)
    mu = _dot_exact_rhs(y, ones_bd) * (1.0 / HEAD_DIM)
    yc = y - mu
    var = _dot_exact_rhs(yc * yc, ones_bd) * (1.0 / HEAD_DIM)
    yn = yc * lax.rsqrt(var + GN_EPS) * lng_ref[...] + lnb_ref[...]
    return (yn + bv_ref[...]) * gg_ref[...]


def _mixed_residual(t_ref, rp_ref, y0_ref, sf_ref, sb_ref, bv_ref, gg_ref, lng_ref, lnb_ref, b_ref, c_ref, w_ref, *, gate,
                    rows):
    a = _rwkv_readout(rp_ref, y0_ref, sf_ref, sb_ref, bv_ref, gg_ref, lng_ref, lnb_ref, rows)
    out = _dot(a.astype(bf16), w_ref[0:D_RWKV, :])
    out = out + _dot(b_ref[...].astype(bf16), w_ref[D_RWKV:D_RWKV + D_CONV, :])
    out = out + _dot(c_ref[...].astype(bf16), w_ref[D_RWKV + D_CONV:, :])
    return t_ref[...] + gate * out


def _mixout_kernel(*refs):
    m_ref, o_ref = refs[-2:]
    o_ref[...] = _mixed_residual(*refs[:-2], gate=m_ref[0, 5:6, :], rows=TM)


def _mixout(t, rwkv_parts, ln_g, ln_b, ob, oc, w, layer, mods, grid, row_map, mod_map, c_map, out_rows):
    rp, y0, sf, sb, bv, gg = rwkv_parts
    d = t.shape[1]
    nct = TM // CHUNK
    sshape = (bv.shape[0] // CHUNK, N_PAIR, HEAD_DIM, PAIR)
    n_axes = len(grid)
    if n_axes == 1:
        out_map = lambda i: (i, 0)
    else:
        nj = grid[1]
        out_map = lambda b, j: (b * nj + j, 0)
    rmap = lambda *a: (row_map(*a), 0)
    dirrow = lambda *a: (0, row_map(*a), 0)
    st = lambda *a: (row_map(*a), 0, 0, 0)
    c2 = lambda *a: (0, 0)
    return pl.pallas_call(
        _mixout_kernel,
        grid=grid,
        in_specs=[pl.BlockSpec((TM, d), rmap),
                  pl.BlockSpec((2, TM, D_RWKV), dirrow), pl.BlockSpec((2, TM, D_RWKV), dirrow),
                  pl.BlockSpec((nct, N_PAIR, HEAD_DIM, PAIR), st), pl.BlockSpec((nct, N_PAIR, HEAD_DIM, PAIR), st),
                  pl.BlockSpec((TM, D_RWKV), rmap), pl.BlockSpec((TM, D_RWKV), rmap),
                  pl.BlockSpec((1, D_RWKV), c2), pl.BlockSpec((1, D_RWKV), c2),
                  pl.BlockSpec((TM, D_CONV), rmap),
                  pl.BlockSpec((TM, D_DIFF), lambda *a: (c_map(*a), 0)),
                  pl.BlockSpec((None, D_RWKV + D_CONV + D_DIFF, d), lambda *a: (layer, 0, 0)),
                  pl.BlockSpec((1, N_MOD, d), lambda *a: (mod_map(*a), 0, 0))],
        out_specs=pl.BlockSpec((TM, d), out_map),
        out_shape=jax.ShapeDtypeStruct((out_rows, d), f32),
        compiler_params=_cparams(n_axes),
        name="mixout",
    )(t, rp, y0, sf.reshape(sshape), sb.reshape(sshape), bv, gg, ln_g.reshape(1, D_RWKV), ln_b.reshape(1, D_RWKV),
      ob, oc, w, mods)


def _rope_tables(n_ctx, n_lat):
    n_rows = n_lat // GRID_W
    row = jnp.repeat(jnp.arange(n_rows, dtype=jnp.int32), GRID_W)
    col = jnp.tile(jnp.arange(GRID_W, dtype=jnp.int32), n_rows)
    inv = 1.0 / (ROPE_THETA ** (jnp.arange(ROPE_FREQS, dtype=f32) * 2.0 / AXIS_DIM))
    ang = jnp.stack([row, col], axis=-1).astype(f32)[..., None] * inv
    cos, sin = jnp.cos(ang), jnp.sin(ang)
    cos64 = jnp.concatenate([cos[:, 0], cos[:, 0], cos[:, 1], cos[:, 1]], axis=-1)
    sin64 = jnp.concatenate([-sin[:, 0], sin[:, 0], -sin[:, 1], sin[:, 1]], axis=-1)
    reps = D_QK // DIFF_QK_DIM
    cos_t = jnp.concatenate([jnp.ones((n_ctx, D_QK), f32), jnp.tile(cos64, (1, reps))], axis=0)
    sin_t = jnp.concatenate([jnp.zeros((n_ctx, D_QK), f32), jnp.tile(sin64, (1, reps))], axis=0)
    return cos_t, sin_t


def kernel(x, c, ctx, c_ctx, ada_w, ada_b, norm_g, ffn_w_in, ffn_w_out, mix_w_in, mix_w_out, rwkv_mu, rwkv_w0, rwkv_w2,
           rwkv_a0, rwkv_a2, rwkv_g2, rwkv_kk, rwkv_ka, rwkv_rk, rwkv_ln_g, rwkv_ln_b, conv_dw_w, conv_dw_b, conv_ln_g,
           conv_ln_b, diff_lam, diff_norm_g, final_g):
    nb, n_lat, d = x.shape
    n_ctx = ctx.shape[1]
    depth = ada_w.shape[0]
    seq = n_ctx + n_lat
    assert n_ctx % TM == 0 and n_lat % TM == 0 and n_lat % GRID_W == 0 and seq % n_ctx == 0
    assert nb + 1 <= COND_ROWS
    nbc, nbx, nbs = n_ctx // TM, n_lat // TM, seq // TM
    m = nb * seq

    if n_ctx == TM and _dense_tile(seq) == 3 * TM:
        t = (x.reshape(nb * n_lat, d), ctx.reshape(nb * n_ctx, d))
    else:
        t = jnp.concatenate([ctx, x], axis=1).reshape(m, d)
    cond =jnp.zeros((COND_ROWS, d), f32).at[:nb].set(c).at[nb].set(c_ctx)
    mods = _adaln(cond, ada_w, ada_b).reshape(depth, COND_ROWS, N_MOD, d)
    cos_t, sin_t = _rope_tables(n_ctx, n_lat)
    w_in16 = ffn_w_in.astype(bf16)
    w_out16 = ffn_w_out.astype(bf16)
    mix_in16 = mix_w_in.astype(bf16)
    mix_out16 = mix_w_out.astype(bf16)

    mod_all = lambda i: jnp.where(i % nbs < nbc, nb, i // nbs)
    row_all = lambda i: i
    row_lat = lambda b, j: b * nbs + nbc + j
    mod_lat = lambda b, j: b

    out = None
    for l in range(depth):
        last = l == depth - 1
        ml = mods[l]
        t = _ffn(t, ml, norm_g[l, 0], w_in16, w_out16, l, 0, 0, seq, n_ctx, nb)
        fr, hc, q, k, v = _mixin(t, ml, norm_g[l, 1], mix_in16, l, cos_t, sin_t, seq, n_ctx, nb)
        rp, y0, mm, nn, bv, gg = _rwkv_chunks(fr, rwkv_mu[l], rwkv_w0[l], rwkv_w2[l], rwkv_a0[l], rwkv_a2[l], rwkv_g2[l],
                                              rwkv_kk[l], rwkv_ka[l], rwkv_rk[l].reshape(D_RWKV), nbc, nbs)
        sf, sb = _rwkv_state(mm, nn, nb, seq // CHUNK, n_ctx // CHUNK)
        rwkv_parts = (rp, y0, sf, sb, bv, gg)
        ob = _conv(hc, conv_dw_w[l], conv_dw_b[l], conv_ln_g[l], conv_ln_b[l], nbc, nbs)
        lam_init = 0.8 - 0.6 * math.exp(-0.3 * l)
        if last:
            oc = _attn(q, k, v, diff_lam[l], diff_norm_g[l], lam_init, nb, n_ctx, n_lat, False)
            t = _mixout(t, rwkv_parts, rwkv_ln_g[l], rwkv_ln_b[l], ob, oc, mix_out16, l, ml, (nb, nbx), row_lat,
                        mod_lat, lambda b, j: b * nbx + j, nb * n_lat)
            out = _ffn(t, ml, norm_g[l, 2], w_in16, w_out16, l, 1, 6, n_lat, 0, nb, final_g=final_g)
        else:
            oc = _attn(q, k, v, diff_lam[l], diff_norm_g[l], lam_init, nb, n_ctx, n_lat, True)
            t = _mixout(t, rwkv_parts, rwkv_ln_g[l], rwkv_ln_b[l], ob, oc, mix_out16, l, ml, (nbs * nb,), row_all,
                        mod_all, row_all, m)
            t = _ffn(t, ml, norm_g[l, 2], w_in16, w_out16, l, 1, 6, seq, n_ctx, nb)
    return out.reshape(nb, n_lat, d)
```

```python
import functools
import math

import jax
import jax.numpy as jnp
from jax import lax
from jax.experimental import pallas as pl
from jax.experimental.pallas import tpu as pltpu

f32 = jnp.float32
bf16 = jnp.bfloat16

N_MOD = 9
EPS = 1e-6
GRID_W = 64
RWKV_HEADS = 4
HEAD_DIM = 64
D_RWKV = RWKV_HEADS * HEAD_DIM
PAIR = 2 * HEAD_DIM
N_PAIR = RWKV_HEADS // 2
DECAY_LORA = 64
AAA_LORA = 64
GATE_LORA = 128
GN_EPS = 64e-5
NORM_EPS = 1e-12
D_CONV = 256
CONV_WIDTH = 31
CONV_HALF = CONV_WIDTH // 2
DIFF_HEADS = 4
DIFF_QK_DIM = 64
DIFF_V_DIM = 2 * DIFF_QK_DIM
D_DIFF = DIFF_HEADS * DIFF_V_DIM
ROPE_THETA = 10000.0
LOG2_E = 1.4426950408889634
AXIS_DIM = DIFF_QK_DIM // 2
ROPE_FREQS = AXIS_DIM // 2
RWKV_IN = 3 * D_RWKV + 2 * DECAY_LORA + 2 * AAA_LORA + GATE_LORA
CONV_IN = 2 * D_CONV
D_QK = DIFF_HEADS * 2 * DIFF_QK_DIM
DIFF_IN = 2 * D_QK + D_DIFF
P_IN = RWKV_IN + CONV_IN + DIFF_IN

TM = 256
CHUNK = 64
CHUNKS_PER_BODY = 4
STATE_STEPS = 2
HALO = 16
COND_ROWS = 16
VMEM_LIMIT = 56 * 1024 * 1024


def _dot(a, b, prec=None):
    return jnp.dot(a, b, preferred_element_type=f32, precision=prec)


def _dot_nt(a, b, prec=None):
    return lax.dot_general(a, b, (((1,), (1,)), ((), ())), preferred_element_type=f32, precision=prec)


def _dot_tn(a, b, prec=None):
    return lax.dot_general(a, b, (((0,), (0,)), ((), ())), preferred_element_type=f32, precision=prec)


def _split3(a):
    a1 = a.astype(bf16)
    r1 = a - a1.astype(f32)
    a2 = r1.astype(bf16)
    a3 = (r1 - a2.astype(f32)).astype(bf16)
    return a1, a2, a3


def _dot_exact_rhs(a, b16):
    a1, a2, a3 = _split3(a)
    return _dot(a1, b16) + _dot(a2, b16) + _dot(a3, b16)


def _dot_exact_lhs(a16, b):
    b1, b2, b3 = _split3(b)
    return _dot(a16, b1) + _dot(a16, b2) + _dot(a16, b3)


def _pair_blockdiag(x):
    first = (lax.broadcasted_iota(jnp.int32, x.shape, 1) % PAIR) < HEAD_DIM
    zero = jnp.zeros_like(x)
    return jnp.concatenate([jnp.where(first, x, zero), jnp.where(first, zero, x)], axis=0)


def _sigmoid(x):
    return 1.0 / (1.0 + jnp.exp(-x))


def _cparams(n_axes):
    return pltpu.CompilerParams(dimension_semantics=("arbitrary",) * n_axes, vmem_limit_bytes=VMEM_LIMIT)


def _head_ones(n, width):
    r = lax.broadcasted_iota(jnp.int32, (n, n), 0) // width
    c = lax.broadcasted_iota(jnp.int32, (n, n), 1) // width
    return jnp.where(r == c, 1.0, 0.0).astype(bf16)


def _adaln_kernel(c_ref, w_ref, b_ref, o_ref):
    cond = c_ref[...]
    cond = cond * _sigmoid(cond)
    o_ref[0] = _dot(cond.astype(bf16), w_ref[0].astype(bf16)) + b_ref[0]


def _adaln(cond, ada_w, ada_b):
    depth, d, nd = ada_w.shape
    tn = nd // 4
    return pl.pallas_call(
        _adaln_kernel,
        grid=(depth, nd // tn),
        in_specs=[pl.BlockSpec((COND_ROWS, d), lambda l, j: (0, 0)),
                  pl.BlockSpec((1, d, tn), lambda l, j: (l, 0, j)),
                  pl.BlockSpec((1, 1, tn), lambda l, j: (l, 0, j))],
        out_specs=pl.BlockSpec((1, COND_ROWS, tn), lambda l, j: (l, 0, j)),
        out_shape=jax.ShapeDtypeStruct((depth, COND_ROWS, nd), f32),
        compiler_params=_cparams(2),
        name="adaln",
    )(cond, ada_w, ada_b.reshape(depth, 1, nd))


def _modulated_norm(x, g, shift, scale):
    y = x * lax.rsqrt(jnp.mean(x * x, axis=-1, keepdims=True) + EPS) * g
    return y * (1.0 + scale) + shift


def _dense_tile(rows_per_seq):
    for tm in (1024, 768, 512, 256):
        if rows_per_seq % tm == 0:
            return tm
    raise ValueError(rows_per_seq)


def _mod_row(mb_ref, mc_ref, idx, tm, n_ctx, blocks_per_seq):
    if n_ctx == 0:
        return mb_ref[0, idx:idx + 1, :]
    row = (pl.program_id(0) % blocks_per_seq) * tm + lax.broadcasted_iota(jnp.int32, (tm, 1), 0)
    return jnp.where(row < n_ctx, mc_ref[0, idx:idx + 1, :], mb_ref[0, idx:idx + 1, :])


def _ffn_kernel(*refs, source, s0, ff, fc, final, tm, n_ctx, bps):
    n_src = {"rows": 1, "split": 4}[source]
    src, (mb_ref, mc_ref, g_ref, wi_ref, wo_ref), rest = refs[:n_src], refs[n_src:n_src + 5], refs[n_src + 5:]
    o_ref = rest[-1]
    mod = lambda idx: _mod_row(mb_ref, mc_ref, idx, tm, n_ctx, bps)
    if source == "rows":
        x = src[0][...]
    else:
        first = pl.program_id(0) % bps == 0
        x = jnp.concatenate([jnp.where(first, src[0][...], src[1][...]), src[2][...], src[3][...]], axis=0)
    h = _modulated_norm(x, g_ref[...], mod(s0), mod(s0 + 1)).astype(bf16)
    acc = jnp.zeros(x.shape, f32)
    for j in range(ff // fc):
        gt = _dot(h, wi_ref[:, j * fc:(j + 1) * fc])
        up = _dot(h, wi_ref[:, ff + j * fc:ff + (j + 1) * fc])
        act = (gt * _sigmoid(gt) * up).astype(bf16)
        acc = acc + _dot(act, wo_ref[j * fc:(j + 1) * fc, :])
    out = x + 0.5 * mod(s0 + 2) * acc
    if final:
        fg_ref = rest[0]
        out = out * lax.rsqrt(jnp.mean(out * out, axis=-1, keepdims=True) + EPS) * fg_ref[...]
    o_ref[...] = out


def _ffn(t, mods, g, w_in, w_out, layer, slot, s0, rows_per_seq, n_ctx, ctx_row, final_g=None):
    ff = w_out.shape[2]
    fc = 256 if ff % 256 == 0 else ff
    tm = _dense_tile(rows_per_seq)
    bps = rows_per_seq // tm
    const = lambda i: (0, 0)
    pick = lambda i: (layer, slot, 0, 0)
    row = lambda i: (i, 0)
    if isinstance(t, tuple):
        lat, ctx = t
        d = lat.shape[1]
        assert n_ctx == TM and tm == 3 * TM
        m = lat.shape[0] + ctx.shape[0]
        per = (rows_per_seq - n_ctx) // TM
        lat_blk = lambda u: pl.BlockSpec(
            (TM, d), lambda i: ((i // bps) * per + jnp.maximum(3 * (i % bps) + u - 1, 0), 0))
        source, src_specs, src_args = "split", [pl.BlockSpec((TM, d), lambda i: (i // bps, 0)),
                                                lat_blk(0), lat_blk(1), lat_blk(2)], [ctx, lat, lat, lat]
    else:
        m, d = t.shape
        source, src_specs, src_args = "rows", [pl.BlockSpec((tm, d), row)], [t]
    in_specs = src_specs + [pl.BlockSpec((1, N_MOD, d), lambda i: (i // bps, 0, 0)),
                            pl.BlockSpec((1, N_MOD, d), lambda i: (ctx_row, 0, 0)),
                            pl.BlockSpec((1, d), const),
                            pl.BlockSpec((None, None, d, 2 * ff), pick, pipeline_mode=pl.Buffered(1)),
                            pl.BlockSpec((None, None, ff, d), pick, pipeline_mode=pl.Buffered(1))]
    args = src_args + [mods, mods, g.reshape(1, d), w_in, w_out]
    if final_g is not None:
        in_specs.append(pl.BlockSpec((1, d), const))
        args.append(final_g.reshape(1, d))
    return pl.pallas_call(
        functools.partial(_ffn_kernel, source=source, s0=s0, ff=ff, fc=fc, final=final_g is not None, tm=tm,
                          n_ctx=n_ctx, bps=bps),
        grid=(m // tm,), in_specs=in_specs,
        out_specs=pl.BlockSpec((tm, d), lambda i: (i, 0)),
        out_shape=jax.ShapeDtypeStruct((m, d), f32),
        compiler_params=_cparams(1),
        name="ffn",
    )(*args)


def _mixin_kernel(x_ref, mb_ref, mc_ref, g_ref, w_ref, cos_ref, sin_ref, fr_ref, hc_ref, q_ref, k_ref, v_ref, *, tm,
                  n_ctx, bps):
    x = x_ref[...]
    mod = lambda idx: _mod_row(mb_ref, mc_ref, idx, tm, n_ctx, bps)
    h = _modulated_norm(x, g_ref[...], mod(3), mod(4)).astype(bf16)
    fr_ref[...] = _dot(h, w_ref[:, 0:RWKV_IN])
    o = RWKV_IN
    val = _dot(h, w_ref[:, o:o + D_CONV])
    gate = _dot(h, w_ref[:, o + D_CONV:o + CONV_IN])
    hc_ref[...] = val * _sigmoid(gate)
    o = RWKV_IN + CONV_IN
    cos = cos_ref[...]
    sin = sin_ref[...]
    lane = lax.broadcasted_iota(jnp.int32, cos.shape, 1)
    low_half = ((lane // ROPE_FREQS) % 2) == 0

    def rope(t):
        partner = jnp.where(low_half, pltpu.roll(t, D_QK - ROPE_FREQS, 1), pltpu.roll(t, ROPE_FREQS, 1))
        return t * cos + partner * sin

    q = _dot(h, w_ref[:, o:o + D_QK])
    q_ref[...] = (rope(q) * (DIFF_QK_DIM ** -0.5 * LOG2_E)).astype(bf16)
    k = _dot(h, w_ref[:, o + D_QK:o + 2 * D_QK])
    k_ref[...] = rope(k).astype(bf16)
    v_ref[...] = _dot(h, w_ref[:, o + 2 * D_QK:o + 2 * D_QK + D_DIFF]).astype(bf16)


def _mixin(t, mods, g, w, layer, cos, sin, seq, n_ctx, ctx_row):
    m, d = t.shape
    tm = _dense_tile(seq)
    bps = seq // tm
    const = lambda i: (0, 0)
    row = lambda i: (i, 0)
    return pl.pallas_call(
        functools.partial(_mixin_kernel, tm=tm, n_ctx=n_ctx, bps=bps),
        grid=(m // tm,),
        in_specs=[pl.BlockSpec((tm, d), row),
                  pl.BlockSpec((1, N_MOD, d), lambda i: (i // bps, 0, 0)),
                  pl.BlockSpec((1, N_MOD, d), lambda i: (ctx_row, 0, 0)),
                  pl.BlockSpec((1, d), const),
                  pl.BlockSpec((None, d, P_IN), lambda i: (layer, 0, 0), pipeline_mode=pl.Buffered(1)),
                  pl.BlockSpec((tm, D_QK), lambda i: (i % bps, 0)),
                  pl.BlockSpec((tm, D_QK), lambda i: (i % bps, 0))],
        out_specs=[pl.BlockSpec((tm, RWKV_IN), row), pl.BlockSpec((tm, D_CONV), row),
                   pl.BlockSpec((tm, D_QK), row), pl.BlockSpec((tm, D_QK), row), pl.BlockSpec((tm, D_DIFF), row)],
        out_shape=[jax.ShapeDtypeStruct((m, RWKV_IN), f32), jax.ShapeDtypeStruct((m, D_CONV), f32),
                   jax.ShapeDtypeStruct((m, D_QK), bf16), jax.ShapeDtypeStruct((m, D_QK), bf16),
                   jax.ShapeDtypeStruct((m, D_DIFF), bf16)],
        compiler_params=_cparams(1),
        name="mixin",
    )(t, mods, mods, g.reshape(1, d), w, cos, sin)


def _attn_kernel(q_ref, k_ref, v_ref, lam_ref, g_ref, o_ref, vaug_ref, *, lam_init, n_ctx, n_lat, tq, unroll,
                 with_ctx):
    lv = lam_ref[...]
    lam = (jnp.exp(jnp.sum(lv[0:1] * lv[1:2], keepdims=True)) - jnp.exp(jnp.sum(lv[2:3] * lv[3:4], keepdims=True))
           + lam_init)

    @pl.when(jnp.logical_and(pl.program_id(0) == 0, pl.program_id(1) == 0))
    def _():
        vaug_ref[:, DIFF_V_DIM:] = jnp.ones((vaug_ref.shape[0], DIFF_V_DIM), bf16)

    vaug_ref[:, :DIFF_V_DIM] = v_ref[...]

    def attend(q, k, v):
        parts = []
        for m in range(2):
            sl = slice(m * DIFF_QK_DIM, (m + 1) * DIFF_QK_DIM)
            s = _dot_nt(q[:, sl], k[:, sl])
            e = jnp.exp2(s - jnp.max(s, axis=-1, keepdims=True))
            pv = _dot(e.astype(bf16), v)
            parts.append(pv[:, :DIFF_V_DIM] * (1.0 / pv[:, DIFF_V_DIM:]))
        o = parts[0] - lam * parts[1]
        o = o * lax.rsqrt(jnp.mean(o * o, axis=-1, keepdims=True) + 1e-5) * g_ref[...]
        return (o * (1.0 - lam_init)).astype(bf16)

    out_off = 0
    if with_ctx:
        o_ref[0:n_ctx, :] = attend(q_ref[0:n_ctx, :], k_ref[0:n_ctx, :], vaug_ref[0:n_ctx, :])
        out_off = n_ctx
    align = math.gcd(n_ctx, tq)

    def body(i, carry):
        for u in range(unroll):
            t = i * unroll + u
            q_rows = pl.ds(pl.multiple_of(n_ctx + t * tq, align), tq)
            o_rows = pl.ds(pl.multiple_of(out_off + t * tq, align), tq)
            o_ref[o_rows, :] = attend(q_ref[q_rows, :], k_ref[...], vaug_ref[...])
        return carry

    lax.fori_loop(0, n_lat // (tq * unroll), body, 0)


def _attn(q, k, v, lam_vecs, norm_g, lam_init, nbatch, n_ctx, n_lat, with_ctx):
    seq = n_ctx + n_lat
    tq = TM
    unroll = 8 if n_lat % (8 * tq) == 0 else 1
    out_seq = seq if with_ctx else n_lat
    blk = lambda b, h: (b, h)
    return pl.pallas_call(
        functools.partial(_attn_kernel, lam_init=lam_init, n_ctx=n_ctx, n_lat=n_lat, tq=tq, unroll=unroll,
                          with_ctx=with_ctx),
        grid=(nbatch, DIFF_HEADS),
        in_specs=[pl.BlockSpec((seq, DIFF_V_DIM), blk), pl.BlockSpec((seq, DIFF_V_DIM), blk),
                  pl.BlockSpec((seq, DIFF_V_DIM), blk),
                  pl.BlockSpec((4, DIFF_QK_DIM), lambda b, h: (0, 0)),
                  pl.BlockSpec((1, DIFF_V_DIM), lambda b, h: (0, 0))],
        out_specs=pl.BlockSpec((out_seq, DIFF_V_DIM), blk),
        out_shape=jax.ShapeDtypeStruct((nbatch * out_seq, D_DIFF), bf16),
        scratch_shapes=[pltpu.VMEM((seq, 2 * DIFF_V_DIM), bf16)],
        compiler_params=_cparams(2),
        name="diffattn",
    )(q, k, v, lam_vecs, norm_g.reshape(1, DIFF_V_DIM))


def _segment_flags(i, nbc, nbs):
    j = i % nbs
    first = jnp.logical_or(j == 0, j == nbc)
    last = jnp.logical_or(j == nbc - 1, j == nbs - 1)
    return first, last


def _halo_specs(width, n_rows):
    per = TM // HALO
    nblk = n_rows // HALO
    prev = pl.BlockSpec((HALO, width), lambda i: (jnp.maximum(i * per - 1, 0), 0))
    nxt = pl.BlockSpec((HALO, width), lambda i: (jnp.minimum((i + 1) * per, nblk - 1), 0))
    return prev, nxt


def _conv_kernel(h_ref, hp_ref, hn_ref, w_ref, b_ref, g_ref, bb_ref, o_ref, buf, shf, *, nbc, nbs):
    first, last = _segment_flags(pl.program_id(0), nbc, nbs)
    buf[0:HALO, :] = jnp.where(first, 0.0, hp_ref[...])
    buf[HALO:HALO + TM, :] = h_ref[...]
    buf[HALO + TM:HALO + TM + HALO, :] = jnp.where(last, 0.0, hn_ref[...])
    sub = 64
    sublanes = 8
    first_tap = HALO - CONV_HALF
    for r0 in range(0, TM, sub):
        acc = jnp.zeros((sub, D_CONV), f32) + b_ref[...]
        for r in range(sublanes):
            taps = [kk for kk in range(CONV_WIDTH) if (first_tap + kk) % sublanes == r]
            if not taps:
                continue
            span = sub + sublanes * max((first_tap + kk) // sublanes for kk in taps)
            shf[0:span, :] = buf[r0 + r:r0 + r + span, :]
            for kk in taps:
                a = sublanes * ((first_tap + kk) // sublanes)
                acc = acc + shf[a:a + sub, :] * w_ref[kk:kk + 1, :]
        mu = jnp.mean(acc, axis=-1, keepdims=True)
        xc = acc - mu
        var = jnp.mean(xc * xc, axis=-1, keepdims=True)
        y = xc * lax.rsqrt(var + 1e-5) * g_ref[...] + bb_ref[...]
        o_ref[r0:r0 + sub, :] = (y * _sigmoid(y)).astype(bf16)


def _conv(hc, dw_w, dw_b, ln_g, ln_b, nbc, nbs):
    m = hc.shape[0]
    prev, nxt = _halo_specs(D_CONV, m)
    const = lambda i: (0, 0)
    return pl.pallas_call(
        functools.partial(_conv_kernel, nbc=nbc, nbs=nbs),
        grid=(m // TM,),
        in_specs=[pl.BlockSpec((TM, D_CONV), lambda i: (i, 0)), prev, nxt,
                  pl.BlockSpec((CONV_WIDTH, D_CONV), const), pl.BlockSpec((1, D_CONV), const),
                  pl.BlockSpec((1, D_CONV), const), pl.BlockSpec((1, D_CONV), const)],
        out_specs=pl.BlockSpec((TM, D_CONV), lambda i: (i, 0)),
        out_shape=jax.ShapeDtypeStruct((m, D_CONV), bf16),
        scratch_shapes=[pltpu.VMEM((TM + 2 * HALO, D_CONV), f32), pltpu.VMEM((64 + 2 * HALO, D_CONV), f32)],
        compiler_params=_cparams(1),
        name="convmod",
    )(hc, hc, hc, dw_w, dw_b.reshape(1, D_CONV), ln_g.reshape(1, D_CONV), ln_b.reshape(1, D_CONV))


def _rwkv_chunk_kernel(f_ref, fp_ref, fn_ref, mu_ref, w0_ref, w2_ref, a0_ref, a2_ref, g2_ref, kk_ref, ka_ref, rk_ref,
                       rp_ref, y0_ref, mm_ref, nn_ref, bv_ref, gg_ref,
                       s_r, s_v, s_a, s_lw, s_k, s_b, *, nbc, nbs):
    first, last = _segment_flags(pl.program_id(0), nbc, nbs)
    f = f_ref[...]
    prow = jnp.where(first, 0.0, fp_ref[HALO - 1:HALO, :])
    nrow = jnp.where(last, 0.0, fn_ref[0:1, :])
    rows = lax.broadcasted_iota(jnp.int32, f.shape, 0)
    prev = jnp.where(rows == 0, prow, pltpu.roll(f, 1, 0))
    nxt = jnp.where(rows == TM - 1, nrow, pltpu.roll(f, TM - 1, 0))
    fs = f + mu_ref[0:1, :] * (prev - f) + mu_ref[1:2, :] * (nxt - f)

    r = fs[:, 0:D_RWKV]
    k = fs[:, D_RWKV:2 * D_RWKV]
    v = fs[:, 2 * D_RWKV:3 * D_RWKV]
    o_w = 3 * D_RWKV
    o_a = o_w + 2 * DECAY_LORA
    o_g = o_a + 2 * AAA_LORA
    ones_bd = _head_ones(D_RWKV, HEAD_DIM)
    gg_ref[...] = _dot(_sigmoid(fs[:, o_g:o_g + GATE_LORA]).astype(bf16), g2_ref[...].astype(bf16))
    kk = k * kk_ref[...]
    kk = kk * lax.rsqrt(_dot_exact_rhs(kk * kk, ones_bd) + NORM_EPS)
    ksum = None
    for d in range(2):
        wd = fs[:, o_w + d * DECAY_LORA:o_w + (d + 1) * DECAY_LORA]
        ad = fs[:, o_a + d * AAA_LORA:o_a + (d + 1) * AAA_LORA]
        w_raw = w0_ref[d:d + 1, :] + _dot(jnp.tanh(wd).astype(bf16), w2_ref[d].astype(bf16))
        z = -w_raw
        softplus = jnp.maximum(z, 0.0) + jnp.log(1.0 + jnp.exp(-jnp.abs(z)))
        s_lw[d] = -jnp.exp(-softplus - 0.5)
        a = _sigmoid(a0_ref[d:d + 1, :] + _dot(ad.astype(bf16), a2_ref[d].astype(bf16)))
        kd = k * (1.0 + (a - 1.0) * ka_ref[...])
        ksum = kd if d == 0 else ksum + kd
        s_k[d] = kd
        s_b[d] = kk * a
    s_r[...] = r
    s_v[...] = v
    s_a[...] = -kk
    bv_ref[...] = _dot_exact_rhs(r * ksum * rk_ref[...], ones_bd) * v

    ri = lax.broadcasted_iota(jnp.int32, (CHUNK, CHUNK), 0)
    ci = lax.broadcasted_iota(jnp.int32, (CHUNK, CHUNK), 1)
    tri = tuple(jnp.where(msk, 1.0, 0.0).astype(bf16) for msk in (ci <= ri, ci >= ri))
    rp_i = lax.broadcasted_iota(jnp.int32, (CHUNK, PAIR), 0)
    cp_i = lax.broadcasted_iota(jnp.int32, (CHUNK, PAIR), 1) % CHUNK
    eye = jnp.where(rp_i == cp_i, 1.0, 0.0).astype(f32)
    strict = (cp_i < rp_i, cp_i > rp_i)
    incl = (cp_i <= rp_i, cp_i >= rp_i)
    low = lax.broadcasted_iota(jnp.int32, (CHUNK, PAIR), 1) < HEAD_DIM

    def chunk_body(ci, carry):
        cs = [ci * CHUNKS_PER_BODY + u for u in range(CHUNKS_PER_BODY)]
        rss = [pl.ds(pl.multiple_of(c * CHUNK, CHUNK), CHUNK) for c in cs]
        v16_c, at16, rt, bt16, kt16, wc = [], {}, {}, {}, {}, {}
        for u, rs in enumerate(rss):
            r_c = s_r[rs, :]
            a_c = s_a[rs, :]
            v16_c.append(s_v[rs, :].astype(bf16))
            for d in range(2):
                lw = s_lw[d, rs, :]
                cum = _dot_exact_lhs(tri[d], lw)
                er = jnp.exp(cum)
                ei = jnp.exp(-cum)
                at16[u, d] = (a_c * jnp.exp(cum - lw)).astype(bf16)
                rt[u, d] = r_c * er
                bt16[u, d] = (s_b[d, rs, :] * ei).astype(bf16)
                kt16[u, d] = (s_k[d, rs, :] * ei).astype(bf16)
                wc[u, d] = er[CHUNK - 1:CHUNK, :] if d == 0 else er[0:1, :]

        insts = [(u, d, p) for u in range(CHUNKS_PER_BODY) for d in range(2) for p in range(N_PAIR)]
        psl = lambda p: slice(p * PAIR, (p + 1) * PAIR)
        gms = [_dot_nt(jnp.concatenate([at16[u, d][:, psl(p)], rt[u, d][:, psl(p)].astype(bf16)], axis=0),
                       jnp.concatenate([_pair_blockdiag(bt16[u, d][:, psl(p)]),
                                        _pair_blockdiag(kt16[u, d][:, psl(p)])], axis=0)) for u, d, p in insts]
        lab = [jnp.where(strict[d], g[:CHUNK, :PAIR], 0.0) for (u, d, p), g in zip(insts, gms)]
        lakrk16 = [jnp.concatenate([jnp.where(strict[d], g[:CHUNK, PAIR:], 0.0),
                                    jnp.where(incl[d], g[CHUNK:, PAIR:], 0.0)], axis=0).astype(bf16)
                   for (u, d, p), g in zip(insts, gms)]
        lrb16 = [jnp.where(incl[d], g[CHUNK:, :PAIR], 0.0).astype(bf16) for (u, d, p), g in zip(insts, gms)]
        uv = [_dot(lk, _pair_blockdiag(v16_c[u][:, psl(p)])) for (u, d, p), lk in zip(insts, lakrk16)]
        vtk = [_dot_tn(v16_c[u][:, psl(p)], kt16[u, d][:, psl(p)]) for u, d, p in insts]
        def split(t):
            hi = t.astype(bf16)
            return hi, (t - hi.astype(f32)).astype(bf16)

        def mul_x3(a, b):
            a_hi, a_lo = split(a)
            b_hi, b_lo = split(b)
            w = b.shape[1]
            full = _dot(a_hi, _pair_blockdiag(jnp.concatenate([b_hi, b_lo], axis=1)))
            return full[:, :w] + full[:, w:] + _dot(a_lo, _pair_blockdiag(b_hi))

        n_stage = int(math.log2(CHUNK))
        x = [eye + l for l in lab]
        q = [mul_x3(l, l) for l in lab]
        for stage in range(1, n_stage):
            if stage < n_stage - 1:
                prod = [mul_x3(jnp.concatenate([xx, qq], axis=0), qq) for xx, qq in zip(x, q)]
                x = [xx + pr[:CHUNK] for xx, pr in zip(x, prod)]
                q = [pr[CHUNK:] for pr in prod]
            else:
                x = [xx + mul_x3(xx, qq) for xx, qq in zip(x, q)]
        z0 = [jnp.concatenate([at16[u, d][:, psl(p)].astype(f32), lv[:CHUNK]], axis=1)
              for (u, d, p), lv in zip(insts, uv)]
        au16 = [mul_x3(xx, zz).astype(bf16) for xx, zz in zip(x, z0)]
        ry = [_dot(lr, _pair_blockdiag(au)) for lr, au in zip(lrb16, au16)]
        mn = [_dot_tn(au, bt16[u, d][:, psl(p)]) for (u, d, p), au in zip(insts, au16)]
        for i, (u, d, p) in enumerate(insts):
            ps = psl(p)
            rp_ref[d, rss[u], ps] = (rt[u, d][:, ps] + ry[i][:, :PAIR]).astype(bf16)
            if d == 1:
                j = insts.index((u, 0, p))
                y0_ref[rss[u], ps] = (ry[j][:, PAIR:] + uv[j][CHUNK:]) + (ry[i][:, PAIR:] + uv[i][CHUNK:])
            wc_p = wc[u, d][:, ps]
            m_pair = jnp.where(low, mn[i][0:CHUNK], mn[i][CHUNK:2 * CHUNK])
            n_pair = jnp.where(low, mn[i][2 * CHUNK:3 * CHUNK], mn[i][3 * CHUNK:])
            k_pair = jnp.where(low, vtk[i][:CHUNK], vtk[i][CHUNK:])
            mm_ref[d, cs[u], p] = (eye + m_pair) * wc_p
            nn_ref[d, cs[u], p] = (n_pair + k_pair) * wc_p
        return carry

    lax.fori_loop(0, TM // (CHUNK * CHUNKS_PER_BODY), chunk_body, 0)


def _rwkv_chunks(fr, mu, w0, w2, a0, a2, g2, kk, ka, rk, nbc, nbs):
    m = fr.shape[0]
    nct = TM // CHUNK
    prev, nxt = _halo_specs(RWKV_IN, m)
    c2 = lambda i: (0, 0)
    c3 = lambda i: (0, 0, 0)
    row = lambda i: (i, 0)
    dirrow = lambda i: (0, i, 0)
    ops = lambda i: (0, i, 0, 0, 0)
    return pl.pallas_call(
        functools.partial(_rwkv_chunk_kernel, nbc=nbc, nbs=nbs),
        grid=(m // TM,),
        in_specs=[pl.BlockSpec((TM, RWKV_IN), row), prev, nxt,
                  pl.BlockSpec((2, RWKV_IN), c2), pl.BlockSpec((2, D_RWKV), c2),
                  pl.BlockSpec((2, DECAY_LORA, D_RWKV), c3), pl.BlockSpec((2, D_RWKV), c2),
                  pl.BlockSpec((2, AAA_LORA, D_RWKV), c3), pl.BlockSpec((GATE_LORA, D_RWKV), c2),
                  pl.BlockSpec((1, D_RWKV), c2), pl.BlockSpec((1, D_RWKV), c2), pl.BlockSpec((1, D_RWKV), c2)],
        out_specs=[pl.BlockSpec((2, TM, D_RWKV), dirrow), pl.BlockSpec((TM, D_RWKV), row),
                   pl.BlockSpec((2, nct, N_PAIR, HEAD_DIM, PAIR), ops),
                   pl.BlockSpec((2, nct, N_PAIR, HEAD_DIM, PAIR), ops),
                   pl.BlockSpec((TM, D_RWKV), row), pl.BlockSpec((TM, D_RWKV), row)],
        out_shape=[jax.ShapeDtypeStruct((2, m, D_RWKV), bf16), jax.ShapeDtypeStruct((m, D_RWKV), f32),
                   jax.ShapeDtypeStruct((2, m // CHUNK, N_PAIR, HEAD_DIM, PAIR), f32),
                   jax.ShapeDtypeStruct((2, m // CHUNK, N_PAIR, HEAD_DIM, PAIR), f32),
                   jax.ShapeDtypeStruct((m, D_RWKV), f32), jax.ShapeDtypeStruct((m, D_RWKV), f32)],
        scratch_shapes=[pltpu.VMEM((TM, D_RWKV), f32), pltpu.VMEM((TM, D_RWKV), f32), pltpu.VMEM((TM, D_RWKV), f32),
                        pltpu.VMEM((2, TM, D_RWKV), f32), pltpu.VMEM((2, TM, D_RWKV), f32),
                        pltpu.VMEM((2, TM, D_RWKV), f32)],
        compiler_params=_cparams(1),
        name="rwkv_chunks",
    )(fr, fr, fr, mu, w0, w2, a0, a2, g2, kk.reshape(1, D_RWKV), ka.reshape(1, D_RWKV), rk.reshape(1, D_RWKV))


def _rwkv_state_kernel(mf_ref, mb_ref, nf_ref, nb_ref, sf_ref, sb_ref, st_ref, *, nbatch):
    @pl.when(pl.program_id(0) == 0)
    def _():
        st_ref[...] = jnp.zeros(st_ref.shape, f32)

    group = 8
    probs = [(d, b, p) for b in range(nbatch) for d in range(2) for p in range(N_PAIR)]
    for sub in range(STATE_STEPS):
        pos = (sub, STATE_STEPS - 1 - sub)
        for g0 in range(0, len(probs), group):
            grp = probs[g0:g0 + group]
            st = [st_ref[d, b, p] for d, b, p in grp]
            for (d, b, p), s in zip(grp, st):
                (sf_ref if d == 0 else sb_ref)[b, pos[d], p] = s.astype(bf16)
            mt = [(mf_ref if d == 0 else mb_ref)[0, b, pos[d], p] for d, b, p in grp]
            s1 = [s.astype(bf16) for s in st]
            s2 = [(s - a.astype(f32)).astype(bf16) for s, a in zip(st, s1)]
            m1 = [x.astype(bf16) for x in mt]
            m2 = [(x - a.astype(f32)).astype(bf16) for x, a in zip(mt, m1)]
            p12 = [_dot(a1, _pair_blockdiag(jnp.concatenate([b1, b2], axis=1))) for a1, b1, b2 in zip(s1, m1, m2)]
            p21 = [_dot(a2, _pair_blockdiag(b1)) for a2, b1 in zip(s2, m1)]
            for (d, b, p), x12, x21 in zip(grp, p12, p21):
                st_ref[d, b, p] = ((x12[:, :PAIR] + x12[:, PAIR:] + x21)
                                   + (nf_ref if d == 0 else nb_ref)[0, b, pos[d], p])


def _rwkv_state(mm, nn, nbatch, nca, ncc):
    assert nca % STATE_STEPS == 0 and ncc % STATE_STEPS == 0
    shp = (2, nbatch, nca, N_PAIR, HEAD_DIM, PAIR)
    blk = (1, nbatch, STATE_STEPS, N_PAIR, HEAD_DIM, PAIR)
    nga, ngc = nca // STATE_STEPS, ncc // STATE_STEPS
    c_bwd = lambda s: jnp.where(s < ngc, ngc - 1 - s, nga - 1 - (s - ngc))
    fwd = pl.BlockSpec(blk, lambda s: (0, 0, s, 0, 0, 0))
    bwd = pl.BlockSpec(blk, lambda s: (1, 0, c_bwd(s), 0, 0, 0))
    oshape = jax.ShapeDtypeStruct(shp[1:], bf16)
    return pl.pallas_call(
        functools.partial(_rwkv_state_kernel, nbatch=nbatch),
        grid=(nga,),
        in_specs=[fwd, bwd, fwd, bwd],
        out_specs=[pl.BlockSpec(blk[1:], lambda s: (0, s, 0, 0, 0)),
                   pl.BlockSpec(blk[1:], lambda s: (0, c_bwd(s), 0, 0, 0))],
        out_shape=[oshape, oshape],
        scratch_shapes=[pltpu.VMEM((2, nbatch, N_PAIR, HEAD_DIM, PAIR), f32)],
        compiler_params=_cparams(1),
        name="rwkv_state",
    )(mm.reshape(shp), mm.reshape(shp), nn.reshape(shp), nn.reshape(shp))


def _rwkv_readout(rp_ref, y0_ref, sf_ref, sb_ref, bv_ref, gg_ref, lng_ref, lnb_ref, n_rows):
    rows = []
    for c in range(n_rows // CHUNK):
        rs = slice(c * CHUNK, (c + 1) * CHUNK)
        cols = []
        for p in range(N_PAIR):
            ps = slice(p * PAIR, (p + 1) * PAIR)
            lhs = jnp.concatenate([rp_ref[0, rs, ps], rp_ref[1, rs, ps]], axis=1).astype(bf16)
            rhs = jnp.concatenate([_pair_blockdiag(sf_ref[c, p].astype(bf16)),
                                   _pair_blockdiag(sb_ref[c, p].astype(bf16))], axis=1)
            cols.append(y0_ref[rs, ps] + _dot_nt(lhs, rhs))
        rows.append(jnp.concatenate(cols, axis=1))
    y = jnp.concatenate(rows, axis=0)
    ones_bd = _head_ones(D_RWKV, HEAD_DIM)
    mu = _dot_exact_rhs(y, ones_bd) * (1.0 / HEAD_DIM)
    yc = y - mu
    var = _dot_exact_rhs(yc * yc, ones_bd) * (1.0 / HEAD_DIM)
    yn = yc * lax.rsqrt(var + GN_EPS) * lng_ref[...] + lnb_ref[...]
    return (yn + bv_ref[...]) * gg_ref[...]


def _mixed_residual(t_ref, rp_ref, y0_ref, sf_ref, sb_ref, bv_ref, gg_ref, lng_ref, lnb_ref, b_ref, c_ref, w_ref, *, gate,
                    rows):
    a = _rwkv_readout(rp_ref, y0_ref, sf_ref, sb_ref, bv_ref, gg_ref, lng_ref, lnb_ref, rows)
    out = _dot(a.astype(bf16), w_ref[0:D_RWKV, :])
    out = out + _dot(b_ref[...].astype(bf16), w_ref[D_RWKV:D_RWKV + D_CONV, :])
    out = out + _dot(c_ref[...].astype(bf16), w_ref[D_RWKV + D_CONV:, :])
    return t_ref[...] + gate * out


def _mixout_kernel(*refs):
    m_ref, o_ref = refs[-2:]
    o_ref[...] = _mixed_residual(*refs[:-2], gate=m_ref[0, 5:6, :], rows=TM)


def _mixout(t, rwkv_parts, ln_g, ln_b, ob, oc, w, layer, mods, grid, row_map, mod_map, c_map, out_rows):
    rp, y0, sf, sb, bv, gg = rwkv_parts
    d = t.shape[1]
    nct = TM // CHUNK
    sshape = (bv.shape[0] // CHUNK, N_PAIR, HEAD_DIM, PAIR)
    n_axes = len(grid)
    if n_axes == 1:
        out_map = lambda i: (i, 0)
    else:
        nj = grid[1]
        out_map = lambda b, j: (b * nj + j, 0)
    rmap = lambda *a: (row_map(*a), 0)
    dirrow = lambda *a: (0, row_map(*a), 0)
    st = lambda *a: (row_map(*a), 0, 0, 0)
    c2 = lambda *a: (0, 0)
    return pl.pallas_call(
        _mixout_kernel,
        grid=grid,
        in_specs=[pl.BlockSpec((TM, d), rmap),
                  pl.BlockSpec((2, TM, D_RWKV), dirrow), pl.BlockSpec((TM, D_RWKV), rmap),
                  pl.BlockSpec((nct, N_PAIR, HEAD_DIM, PAIR), st), pl.BlockSpec((nct, N_PAIR, HEAD_DIM, PAIR), st),
                  pl.BlockSpec((TM, D_RWKV), rmap), pl.BlockSpec((TM, D_RWKV), rmap),
                  pl.BlockSpec((1, D_RWKV), c2), pl.BlockSpec((1, D_RWKV), c2),
                  pl.BlockSpec((TM, D_CONV), rmap),
                  pl.BlockSpec((TM, D_DIFF), lambda *a: (c_map(*a), 0)),
                  pl.BlockSpec((None, D_RWKV + D_CONV + D_DIFF, d), lambda *a: (layer, 0, 0)),
                  pl.BlockSpec((1, N_MOD, d), lambda *a: (mod_map(*a), 0, 0))],
        out_specs=pl.BlockSpec((TM, d), out_map),
        out_shape=jax.ShapeDtypeStruct((out_rows, d), f32),
        compiler_params=_cparams(n_axes),
        name="mixout",
    )(t, rp, y0, sf.reshape(sshape), sb.reshape(sshape), bv, gg, ln_g.reshape(1, D_RWKV), ln_b.reshape(1, D_RWKV),
      ob, oc, w, mods)


def _rope_tables(n_ctx, n_lat):
    n_rows = n_lat // GRID_W
    row = jnp.repeat(jnp.arange(n_rows, dtype=jnp.int32), GRID_W)
    col = jnp.tile(jnp.arange(GRID_W, dtype=jnp.int32), n_rows)
    inv = 1.0 / (ROPE_THETA ** (jnp.arange(ROPE_FREQS, dtype=f32) * 2.0 / AXIS_DIM))
    ang = jnp.stack([row, col], axis=-1).astype(f32)[..., None] * inv
    cos, sin = jnp.cos(ang), jnp.sin(ang)
    cos64 = jnp.concatenate([cos[:, 0], cos[:, 0], cos[:, 1], cos[:, 1]], axis=-1)
    sin64 = jnp.concatenate([-sin[:, 0], sin[:, 0], -sin[:, 1], sin[:, 1]], axis=-1)
    reps = D_QK // DIFF_QK_DIM
    cos_t = jnp.concatenate([jnp.ones((n_ctx, D_QK), f32), jnp.tile(cos64, (1, reps))], axis=0)
    sin_t = jnp.concatenate([jnp.zeros((n_ctx, D_QK), f32), jnp.tile(sin64, (1, reps))], axis=0)
    return cos_t, sin_t


def kernel(x, c, ctx, c_ctx, ada_w, ada_b, norm_g, ffn_w_in, ffn_w_out, mix_w_in, mix_w_out, rwkv_mu, rwkv_w0, rwkv_w2,
           rwkv_a0, rwkv_a2, rwkv_g2, rwkv_kk, rwkv_ka, rwkv_rk, rwkv_ln_g, rwkv_ln_b, conv_dw_w, conv_dw_b, conv_ln_g,
           conv_ln_b, diff_lam, diff_norm_g, final_g):
    nb, n_lat, d = x.shape
    n_ctx = ctx.shape[1]
    depth = ada_w.shape[0]
    seq = n_ctx + n_lat
    assert n_ctx % TM == 0 and n_lat % TM == 0 and n_lat % GRID_W == 0 and seq % n_ctx == 0
    assert nb + 1 <= COND_ROWS
    nbc, nbx, nbs = n_ctx // TM, n_lat // TM, seq // TM
    m = nb * seq

    if n_ctx == TM and _dense_tile(seq) == 3 * TM:
        t = (x.reshape(nb * n_lat, d), ctx.reshape(nb * n_ctx, d))
    else:
        t = jnp.concatenate([ctx, x], axis=1).reshape(m, d)
    cond =jnp.zeros((COND_ROWS, d), f32).at[:nb].set(c).at[nb].set(c_ctx)
    mods = _adaln(cond, ada_w, ada_b).reshape(depth, COND_ROWS, N_MOD, d)
    cos_t, sin_t = _rope_tables(n_ctx, n_lat)
    w_in16 = ffn_w_in.astype(bf16)
    w_out16 = ffn_w_out.astype(bf16)
    mix_in16 = mix_w_in.astype(bf16)
    mix_out16 = mix_w_out.astype(bf16)

    mod_all = lambda i: jnp.where(i % nbs < nbc, nb, i // nbs)
    row_all = lambda i: i
    row_lat = lambda b, j: b * nbs + nbc + j
    mod_lat = lambda b, j: b

    out = None
    for l in range(depth):
        last = l == depth - 1
        ml = mods[l]
        t = _ffn(t, ml, norm_g[l, 0], w_in16, w_out16, l, 0, 0, seq, n_ctx, nb)
        fr, hc, q, k, v = _mixin(t, ml, norm_g[l, 1], mix_in16, l, cos_t, sin_t, seq, n_ctx, nb)
        rp, y0, mm, nn, bv, gg = _rwkv_chunks(fr, rwkv_mu[l], rwkv_w0[l], rwkv_w2[l], rwkv_a0[l], rwkv_a2[l], rwkv_g2[l],
                                              rwkv_kk[l], rwkv_ka[l], rwkv_rk[l].reshape(D_RWKV), nbc, nbs)
        sf, sb = _rwkv_state(mm, nn, nb, seq // CHUNK, n_ctx // CHUNK)
        rwkv_parts = (rp, y0, sf, sb, bv, gg)
        ob = _conv(hc, conv_dw_w[l], conv_dw_b[l], conv_ln_g[l], conv_ln_b[l], nbc, nbs)
        lam_init = 0.8 - 0.6 * math.exp(-0.3 * l)
        if last:
            oc = _attn(q, k, v, diff_lam[l], diff_norm_g[l], lam_init, nb, n_ctx, n_lat, False)
            t = _mixout(t, rwkv_parts, rwkv_ln_g[l], rwkv_ln_b[l], ob, oc, mix_out16, l, ml, (nb, nbx), row_lat,
                        mod_lat, lambda b, j: b * nbx + j, nb * n_lat)
            out = _ffn(t, ml, norm_g[l, 2], w_in16, w_out16, l, 1, 6, n_lat, 0, nb, final_g=final_g)
        else:
            oc = _attn(q, k, v, diff_lam[l], diff_norm_g[l], lam_init, nb, n_ctx, n_lat, True)
            t = _mixout(t, rwkv_parts, rwkv_ln_g[l], rwkv_ln_b[l], ob, oc, mix_out16, l, ml, (nbs * nb,), row_all,
                        mod_all, row_all, m)
            t = _ffn(t, ml, norm_g[l, 2], w_in16, w_out16, l, 1, 6, seq, n_ctx, nb)
    return out.reshape(nb, n_lat, d)
```

```python
import functools
import math

import jax
import jax.numpy as jnp
from jax import lax
from jax.experimental import pallas as pl
from jax.experimental.pallas import tpu as pltpu

f32 = jnp.float32
bf16 = jnp.bfloat16

N_MOD = 9
EPS = 1e-6
GRID_W = 64
RWKV_HEADS = 4
HEAD_DIM = 64
D_RWKV = RWKV_HEADS * HEAD_DIM
PAIR = 2 * HEAD_DIM
N_PAIR = RWKV_HEADS // 2
DECAY_LORA = 64
AAA_LORA = 64
GATE_LORA = 128
GN_EPS = 64e-5
NORM_EPS = 1e-12
D_CONV = 256
CONV_WIDTH = 31
CONV_HALF = CONV_WIDTH // 2
DIFF_HEADS = 4
DIFF_QK_DIM = 64
DIFF_V_DIM = 2 * DIFF_QK_DIM
D_DIFF = DIFF_HEADS * DIFF_V_DIM
ROPE_THETA = 10000.0
LOG2_E = 1.4426950408889634
AXIS_DIM = DIFF_QK_DIM // 2
ROPE_FREQS = AXIS_DIM // 2
RWKV_IN = 3 * D_RWKV + 2 * DECAY_LORA + 2 * AAA_LORA + GATE_LORA
CONV_IN = 2 * D_CONV
D_QK = DIFF_HEADS * 2 * DIFF_QK_DIM
DIFF_IN = 2 * D_QK + D_DIFF
P_IN = RWKV_IN + CONV_IN + DIFF_IN

TM = 256
CHUNK = 64
CHUNKS_PER_BODY = 4
STATE_STEPS = 2
INV_BASE = 16
HALO = 16
COND_ROWS = 16
VMEM_LIMIT = 56 * 1024 * 1024


def _dot(a, b, prec=None):
    return jnp.dot(a, b, preferred_element_type=f32, precision=prec)


def _dot_nt(a, b, prec=None):
    return lax.dot_general(a, b, (((1,), (1,)), ((), ())), preferred_element_type=f32, precision=prec)


def _dot_tn(a, b, prec=None):
    return lax.dot_general(a, b, (((0,), (0,)), ((), ())), preferred_element_type=f32, precision=prec)


def _split3(a):
    a1 = a.astype(bf16)
    r1 = a - a1.astype(f32)
    a2 = r1.astype(bf16)
    a3 = (r1 - a2.astype(f32)).astype(bf16)
    return a1, a2, a3


def _dot_exact_rhs(a, b16):
    a1, a2, a3 = _split3(a)
    return _dot(a1, b16) + _dot(a2, b16) + _dot(a3, b16)


def _dot_exact_lhs(a16, b):
    b1, b2, b3 = _split3(b)
    return _dot(a16, b1) + _dot(a16, b2) + _dot(a16, b3)


def _pair_blockdiag(x):
    first = (lax.broadcasted_iota(jnp.int32, x.shape, 1) % PAIR) < HEAD_DIM
    zero = jnp.zeros_like(x)
    return jnp.concatenate([jnp.where(first, x, zero), jnp.where(first, zero, x)], axis=0)


def _sigmoid(x):
    return 1.0 / (1.0 + jnp.exp(-x))


def _cparams(n_axes):
    return pltpu.CompilerParams(dimension_semantics=("arbitrary",) * n_axes, vmem_limit_bytes=VMEM_LIMIT)


def _head_ones(n, width):
    r = lax.broadcasted_iota(jnp.int32, (n, n), 0) // width
    c = lax.broadcasted_iota(jnp.int32, (n, n), 1) // width
    return jnp.where(r == c, 1.0, 0.0).astype(bf16)


def _adaln_kernel(c_ref, w_ref, b_ref, o_ref):
    cond = c_ref[...]
    cond = cond * _sigmoid(cond)
    o_ref[0] = _dot(cond.astype(bf16), w_ref[0].astype(bf16)) + b_ref[0]


def _adaln(cond, ada_w, ada_b):
    depth, d, nd = ada_w.shape
    tn = nd // 4
    return pl.pallas_call(
        _adaln_kernel,
        grid=(depth, nd // tn),
        in_specs=[pl.BlockSpec((COND_ROWS, d), lambda l, j: (0, 0)),
                  pl.BlockSpec((1, d, tn), lambda l, j: (l, 0, j)),
                  pl.BlockSpec((1, 1, tn), lambda l, j: (l, 0, j))],
        out_specs=pl.BlockSpec((1, COND_ROWS, tn), lambda l, j: (l, 0, j)),
        out_shape=jax.ShapeDtypeStruct((depth, COND_ROWS, nd), f32),
        compiler_params=_cparams(2),
        name="adaln",
    )(cond, ada_w, ada_b.reshape(depth, 1, nd))


def _modulated_norm(x, g, shift, scale):
    y = x * lax.rsqrt(jnp.mean(x * x, axis=-1, keepdims=True) + EPS) * g
    return y * (1.0 + scale) + shift


def _dense_tile(rows_per_seq):
    for tm in (1024, 768, 512, 256):
        if rows_per_seq % tm == 0:
            return tm
    raise ValueError(rows_per_seq)


def _mod_row(mb_ref, mc_ref, idx, tm, n_ctx, blocks_per_seq):
    if n_ctx == 0:
        return mb_ref[0, idx:idx + 1, :]
    row = (pl.program_id(0) % blocks_per_seq) * tm + lax.broadcasted_iota(jnp.int32, (tm, 1), 0)
    return jnp.where(row < n_ctx, mc_ref[0, idx:idx + 1, :], mb_ref[0, idx:idx + 1, :])


def _ffn_kernel(*refs, source, s0, ff, fc, final, tm, n_ctx, bps):
    n_src = {"rows": 1, "split": 4}[source]
    src, (mb_ref, mc_ref, g_ref, wi_ref, wo_ref), rest = refs[:n_src], refs[n_src:n_src + 5], refs[n_src + 5:]
    o_ref = rest[-1]
    mod = lambda idx: _mod_row(mb_ref, mc_ref, idx, tm, n_ctx, bps)
    if source == "rows":
        x = src[0][...]
    else:
        first = pl.program_id(0) % bps == 0
        x = jnp.concatenate([jnp.where(first, src[0][...], src[1][...]), src[2][...], src[3][...]], axis=0)
    h = _modulated_norm(x, g_ref[...], mod(s0), mod(s0 + 1)).astype(bf16)
    acc = jnp.zeros(x.shape, f32)
    for j in range(ff // fc):
        gt = _dot(h, wi_ref[:, j * fc:(j + 1) * fc])
        up = _dot(h, wi_ref[:, ff + j * fc:ff + (j + 1) * fc])
        act = (gt * _sigmoid(gt) * up).astype(bf16)
        acc = acc + _dot(act, wo_ref[j * fc:(j + 1) * fc, :])
    out = x + 0.5 * mod(s0 + 2) * acc
    if final:
        fg_ref = rest[0]
        out = out * lax.rsqrt(jnp.mean(out * out, axis=-1, keepdims=True) + EPS) * fg_ref[...]
    o_ref[...] = out


def _ffn(t, mods, g, w_in, w_out, layer, slot, s0, rows_per_seq, n_ctx, ctx_row, final_g=None):
    ff = w_out.shape[2]
    fc = 256 if ff % 256 == 0 else ff
    tm = _dense_tile(rows_per_seq)
    bps = rows_per_seq // tm
    const = lambda i: (0, 0)
    pick = lambda i: (layer, slot, 0, 0)
    row = lambda i: (i, 0)
    if isinstance(t, tuple):
        lat, ctx = t
        d = lat.shape[1]
        assert n_ctx == TM and tm == 3 * TM
        m = lat.shape[0] + ctx.shape[0]
        per = (rows_per_seq - n_ctx) // TM
        lat_blk = lambda u: pl.BlockSpec(
            (TM, d), lambda i: ((i // bps) * per + jnp.maximum(3 * (i % bps) + u - 1, 0), 0))
        source, src_specs, src_args = "split", [pl.BlockSpec((TM, d), lambda i: (i // bps, 0)),
                                                lat_blk(0), lat_blk(1), lat_blk(2)], [ctx, lat, lat, lat]
    else:
        m, d = t.shape
        source, src_specs, src_args = "rows", [pl.BlockSpec((tm, d), row)], [t]
    in_specs = src_specs + [pl.BlockSpec((1, N_MOD, d), lambda i: (i // bps, 0, 0)),
                            pl.BlockSpec((1, N_MOD, d), lambda i: (ctx_row, 0, 0)),
                            pl.BlockSpec((1, d), const),
                            pl.BlockSpec((None, None, d, 2 * ff), pick, pipeline_mode=pl.Buffered(1)),
                            pl.BlockSpec((None, None, ff, d), pick, pipeline_mode=pl.Buffered(1))]
    args = src_args + [mods, mods, g.reshape(1, d), w_in, w_out]
    if final_g is not None:
        in_specs.append(pl.BlockSpec((1, d), const))
        args.append(final_g.reshape(1, d))
    return pl.pallas_call(
        functools.partial(_ffn_kernel, source=source, s0=s0, ff=ff, fc=fc, final=final_g is not None, tm=tm,
                          n_ctx=n_ctx, bps=bps),
        grid=(m // tm,), in_specs=in_specs,
        out_specs=pl.BlockSpec((tm, d), lambda i: (i, 0)),
        out_shape=jax.ShapeDtypeStruct((m, d), f32),
        compiler_params=_cparams(1),
        name="ffn",
    )(*args)


def _mixin_kernel(x_ref, mb_ref, mc_ref, g_ref, w_ref, cos_ref, sin_ref, fr_ref, hc_ref, q_ref, k_ref, v_ref, *, tm,
                  n_ctx, bps):
    x = x_ref[...]
    mod = lambda idx: _mod_row(mb_ref, mc_ref, idx, tm, n_ctx, bps)
    h = _modulated_norm(x, g_ref[...], mod(3), mod(4)).astype(bf16)
    fr_ref[...] = _dot(h, w_ref[:, 0:RWKV_IN])
    o = RWKV_IN
    val = _dot(h, w_ref[:, o:o + D_CONV])
    gate = _dot(h, w_ref[:, o + D_CONV:o + CONV_IN])
    hc_ref[...] = val * _sigmoid(gate)
    o = RWKV_IN + CONV_IN
    cos = cos_ref[...]
    sin = sin_ref[...]
    lane = lax.broadcasted_iota(jnp.int32, cos.shape, 1)
    low_half = ((lane // ROPE_FREQS) % 2) == 0

    def rope(t):
        partner = jnp.where(low_half, pltpu.roll(t, D_QK - ROPE_FREQS, 1), pltpu.roll(t, ROPE_FREQS, 1))
        return t * cos + partner * sin

    q = _dot(h, w_ref[:, o:o + D_QK])
    q_ref[...] = (rope(q) * (DIFF_QK_DIM ** -0.5 * LOG2_E)).astype(bf16)
    k = _dot(h, w_ref[:, o + D_QK:o + 2 * D_QK])
    k_ref[...] = rope(k).astype(bf16)
    v_ref[...] = _dot(h, w_ref[:, o + 2 * D_QK:o + 2 * D_QK + D_DIFF]).astype(bf16)


def _mixin(t, mods, g, w, layer, cos, sin, seq, n_ctx, ctx_row):
    m, d = t.shape
    tm = _dense_tile(seq)
    bps = seq // tm
    const = lambda i: (0, 0)
    row = lambda i: (i, 0)
    return pl.pallas_call(
        functools.partial(_mixin_kernel, tm=tm, n_ctx=n_ctx, bps=bps),
        grid=(m // tm,),
        in_specs=[pl.BlockSpec((tm, d), row),
                  pl.BlockSpec((1, N_MOD, d), lambda i: (i // bps, 0, 0)),
                  pl.BlockSpec((1, N_MOD, d), lambda i: (ctx_row, 0, 0)),
                  pl.BlockSpec((1, d), const),
                  pl.BlockSpec((None, d, P_IN), lambda i: (layer, 0, 0), pipeline_mode=pl.Buffered(1)),
                  pl.BlockSpec((tm, D_QK), lambda i: (i % bps, 0)),
                  pl.BlockSpec((tm, D_QK), lambda i: (i % bps, 0))],
        out_specs=[pl.BlockSpec((tm, RWKV_IN), row), pl.BlockSpec((tm, D_CONV), row),
                   pl.BlockSpec((tm, D_QK), row), pl.BlockSpec((tm, D_QK), row), pl.BlockSpec((tm, D_DIFF), row)],
        out_shape=[jax.ShapeDtypeStruct((m, RWKV_IN), f32), jax.ShapeDtypeStruct((m, D_CONV), f32),
                   jax.ShapeDtypeStruct((m, D_QK), bf16), jax.ShapeDtypeStruct((m, D_QK), bf16),
                   jax.ShapeDtypeStruct((m, D_DIFF), bf16)],
        compiler_params=_cparams(1),
        name="mixin",
    )(t, mods, mods, g.reshape(1, d), w, cos, sin)


def _attn_kernel(q_ref, k_ref, v_ref, lam_ref, g_ref, o_ref, vaug_ref, *, lam_init, n_ctx, n_lat, tq, unroll,
                 with_ctx):
    lv = lam_ref[...]
    lam = (jnp.exp(jnp.sum(lv[0:1] * lv[1:2], keepdims=True)) - jnp.exp(jnp.sum(lv[2:3] * lv[3:4], keepdims=True))
           + lam_init)

    @pl.when(jnp.logical_and(pl.program_id(0) == 0, pl.program_id(1) == 0))
    def _():
        vaug_ref[:, DIFF_V_DIM:] = jnp.ones((vaug_ref.shape[0], DIFF_V_DIM), bf16)

    vaug_ref[:, :DIFF_V_DIM] = v_ref[...]

    def attend(q, k, v):
        parts = []
        for m in range(2):
            sl = slice(m * DIFF_QK_DIM, (m + 1) * DIFF_QK_DIM)
            s = _dot_nt(q[:, sl], k[:, sl])
            e = jnp.exp2(s - jnp.max(s, axis=-1, keepdims=True))
            pv = _dot(e.astype(bf16), v)
            parts.append(pv[:, :DIFF_V_DIM] * (1.0 / pv[:, DIFF_V_DIM:]))
        o = parts[0] - lam * parts[1]
        o = o * lax.rsqrt(jnp.mean(o * o, axis=-1, keepdims=True) + 1e-5) * g_ref[...]
        return (o * (1.0 - lam_init)).astype(bf16)

    out_off = 0
    if with_ctx:
        o_ref[0:n_ctx, :] = attend(q_ref[0:n_ctx, :], k_ref[0:n_ctx, :], vaug_ref[0:n_ctx, :])
        out_off = n_ctx
    align = math.gcd(n_ctx, tq)

    def body(i, carry):
        for u in range(unroll):
            t = i * unroll + u
            q_rows = pl.ds(pl.multiple_of(n_ctx + t * tq, align), tq)
            o_rows = pl.ds(pl.multiple_of(out_off + t * tq, align), tq)
            o_ref[o_rows, :] = attend(q_ref[q_rows, :], k_ref[...], vaug_ref[...])
        return carry

    lax.fori_loop(0, n_lat // (tq * unroll), body, 0)


def _attn(q, k, v, lam_vecs, norm_g, lam_init, nbatch, n_ctx, n_lat, with_ctx):
    seq = n_ctx + n_lat
    tq = TM
    unroll = 8 if n_lat % (8 * tq) == 0 else 1
    out_seq = seq if with_ctx else n_lat
    blk = lambda b, h: (b, h)
    return pl.pallas_call(
        functools.partial(_attn_kernel, lam_init=lam_init, n_ctx=n_ctx, n_lat=n_lat, tq=tq, unroll=unroll,
                          with_ctx=with_ctx),
        grid=(nbatch, DIFF_HEADS),
        in_specs=[pl.BlockSpec((seq, DIFF_V_DIM), blk), pl.BlockSpec((seq, DIFF_V_DIM), blk),
                  pl.BlockSpec((seq, DIFF_V_DIM), blk),
                  pl.BlockSpec((4, DIFF_QK_DIM), lambda b, h: (0, 0)),
                  pl.BlockSpec((1, DIFF_V_DIM), lambda b, h: (0, 0))],
        out_specs=pl.BlockSpec((out_seq, DIFF_V_DIM), blk),
        out_shape=jax.ShapeDtypeStruct((nbatch * out_seq, D_DIFF), bf16),
        scratch_shapes=[pltpu.VMEM((seq, 2 * DIFF_V_DIM), bf16)],
        compiler_params=_cparams(2),
        name="diffattn",
    )(q, k, v, lam_vecs, norm_g.reshape(1, DIFF_V_DIM))


def _segment_flags(i, nbc, nbs):
    j = i % nbs
    first = jnp.logical_or(j == 0, j == nbc)
    last = jnp.logical_or(j == nbc - 1, j == nbs - 1)
    return first, last


def _halo_specs(width, n_rows):
    per = TM // HALO
    nblk = n_rows // HALO
    prev = pl.BlockSpec((HALO, width), lambda i: (jnp.maximum(i * per - 1, 0), 0))
    nxt = pl.BlockSpec((HALO, width), lambda i: (jnp.minimum((i + 1) * per, nblk - 1), 0))
    return prev, nxt


def _conv_kernel(h_ref, hp_ref, hn_ref, w_ref, b_ref, g_ref, bb_ref, o_ref, buf, shf, *, nbc, nbs):
    first, last = _segment_flags(pl.program_id(0), nbc, nbs)
    buf[0:HALO, :] = jnp.where(first, 0.0, hp_ref[...])
    buf[HALO:HALO + TM, :] = h_ref[...]
    buf[HALO + TM:HALO + TM + HALO, :] = jnp.where(last, 0.0, hn_ref[...])
    sub = 64
    sublanes = 8
    first_tap = HALO - CONV_HALF
    for r0 in range(0, TM, sub):
        acc = jnp.zeros((sub, D_CONV), f32) + b_ref[...]
        for r in range(sublanes):
            taps = [kk for kk in range(CONV_WIDTH) if (first_tap + kk) % sublanes == r]
            if not taps:
                continue
            span = sub + sublanes * max((first_tap + kk) // sublanes for kk in taps)
            shf[0:span, :] = buf[r0 + r:r0 + r + span, :]
            for kk in taps:
                a = sublanes * ((first_tap + kk) // sublanes)
                acc = acc + shf[a:a + sub, :] * w_ref[kk:kk + 1, :]
        mu = jnp.mean(acc, axis=-1, keepdims=True)
        xc = acc - mu
        var = jnp.mean(xc * xc, axis=-1, keepdims=True)
        y = xc * lax.rsqrt(var + 1e-5) * g_ref[...] + bb_ref[...]
        o_ref[r0:r0 + sub, :] = (y * _sigmoid(y)).astype(bf16)


def _conv(hc, dw_w, dw_b, ln_g, ln_b, nbc, nbs):
    m = hc.shape[0]
    prev, nxt = _halo_specs(D_CONV, m)
    const = lambda i: (0, 0)
    return pl.pallas_call(
        functools.partial(_conv_kernel, nbc=nbc, nbs=nbs),
        grid=(m // TM,),
        in_specs=[pl.BlockSpec((TM, D_CONV), lambda i: (i, 0)), prev, nxt,
                  pl.BlockSpec((CONV_WIDTH, D_CONV), const), pl.BlockSpec((1, D_CONV), const),
                  pl.BlockSpec((1, D_CONV), const), pl.BlockSpec((1, D_CONV), const)],
        out_specs=pl.BlockSpec((TM, D_CONV), lambda i: (i, 0)),
        out_shape=jax.ShapeDtypeStruct((m, D_CONV), bf16),
        scratch_shapes=[pltpu.VMEM((TM + 2 * HALO, D_CONV), f32), pltpu.VMEM((64 + 2 * HALO, D_CONV), f32)],
        compiler_params=_cparams(1),
        name="convmod",
    )(hc, hc, hc, dw_w, dw_b.reshape(1, D_CONV), ln_g.reshape(1, D_CONV), ln_b.reshape(1, D_CONV))


def _rwkv_chunk_kernel(f_ref, fp_ref, fn_ref, mu_ref, w0_ref, w2_ref, a0_ref, a2_ref, g2_ref, kk_ref, ka_ref, rk_ref,
                       rp_ref, y0_ref, mm_ref, nn_ref, bv_ref, gg_ref,
                       s_r, s_v, s_a, s_lw, s_k, s_b, *, nbc, nbs):
    first, last = _segment_flags(pl.program_id(0), nbc, nbs)
    f = f_ref[...]
    prow = jnp.where(first, 0.0, fp_ref[HALO - 1:HALO, :])
    nrow = jnp.where(last, 0.0, fn_ref[0:1, :])
    rows = lax.broadcasted_iota(jnp.int32, f.shape, 0)
    prev = jnp.where(rows == 0, prow, pltpu.roll(f, 1, 0))
    nxt = jnp.where(rows == TM - 1, nrow, pltpu.roll(f, TM - 1, 0))
    fs = f + mu_ref[0:1, :] * (prev - f) + mu_ref[1:2, :] * (nxt - f)

    r = fs[:, 0:D_RWKV]
    k = fs[:, D_RWKV:2 * D_RWKV]
    v = fs[:, 2 * D_RWKV:3 * D_RWKV]
    o_w = 3 * D_RWKV
    o_a = o_w + 2 * DECAY_LORA
    o_g = o_a + 2 * AAA_LORA
    ones_bd = _head_ones(D_RWKV, HEAD_DIM)
    gg_ref[...] = _dot(_sigmoid(fs[:, o_g:o_g + GATE_LORA]).astype(bf16), g2_ref[...].astype(bf16))
    kk = k * kk_ref[...]
    kk = kk * lax.rsqrt(_dot_exact_rhs(kk * kk, ones_bd) + NORM_EPS)
    ksum = None
    for d in range(2):
        wd = fs[:, o_w + d * DECAY_LORA:o_w + (d + 1) * DECAY_LORA]
        ad = fs[:, o_a + d * AAA_LORA:o_a + (d + 1) * AAA_LORA]
        w_raw = w0_ref[d:d + 1, :] + _dot(jnp.tanh(wd).astype(bf16), w2_ref[d].astype(bf16))
        z = -w_raw
        softplus = jnp.maximum(z, 0.0) + jnp.log(1.0 + jnp.exp(-jnp.abs(z)))
        s_lw[d] = -jnp.exp(-softplus - 0.5)
        a = _sigmoid(a0_ref[d:d + 1, :] + _dot(ad.astype(bf16), a2_ref[d].astype(bf16)))
        kd = k * (1.0 + (a - 1.0) * ka_ref[...])
        ksum = kd if d == 0 else ksum + kd
        s_k[d] = kd
        s_b[d] = kk * a
    s_r[...] = r
    s_v[...] = v
    s_a[...] = -kk
    bv_ref[...] = _dot_exact_rhs(r * ksum * rk_ref[...], ones_bd) * v

    ri = lax.broadcasted_iota(jnp.int32, (CHUNK, CHUNK), 0)
    ci = lax.broadcasted_iota(jnp.int32, (CHUNK, CHUNK), 1)
    tri = tuple(jnp.where(msk, 1.0, 0.0).astype(bf16) for msk in (ci <= ri, ci >= ri))
    rp_i = lax.broadcasted_iota(jnp.int32, (CHUNK, PAIR), 0)
    cp_i = lax.broadcasted_iota(jnp.int32, (CHUNK, PAIR), 1) % CHUNK
    eye = jnp.where(rp_i == cp_i, 1.0, 0.0).astype(f32)
    strict = (cp_i < rp_i, cp_i > rp_i)
    incl = (cp_i <= rp_i, cp_i >= rp_i)
    low = lax.broadcasted_iota(jnp.int32, (CHUNK, PAIR), 1) < HEAD_DIM
    same_block = {}
    s_blk = INV_BASE
    while s_blk <= CHUNK:
        same_block[s_blk] = (rp_i // s_blk) == (cp_i // s_blk)
        s_blk *= 2

    def chunk_body(ci, carry):
        cs = [ci * CHUNKS_PER_BODY + u for u in range(CHUNKS_PER_BODY)]
        rss = [pl.ds(pl.multiple_of(c * CHUNK, CHUNK), CHUNK) for c in cs]
        v16_c, at16, rt, bt16, kt16, wc = [], {}, {}, {}, {}, {}
        for u, rs in enumerate(rss):
            r_c = s_r[rs, :]
            a_c = s_a[rs, :]
            v16_c.append(s_v[rs, :].astype(bf16))
            for d in range(2):
                lw = s_lw[d, rs, :]
                cum = _dot_exact_lhs(tri[d], lw)
                er = jnp.exp(cum)
                ei = jnp.exp(-cum)
                at16[u, d] = (a_c * jnp.exp(cum - lw)).astype(bf16)
                rt[u, d] = r_c * er
                bt16[u, d] = (s_b[d, rs, :] * ei).astype(bf16)
                kt16[u, d] = (s_k[d, rs, :] * ei).astype(bf16)
                wc[u, d] = er[CHUNK - 1:CHUNK, :] if d == 0 else er[0:1, :]

        insts = [(u, d, p) for u in range(CHUNKS_PER_BODY) for d in range(2) for p in range(N_PAIR)]
        psl = lambda p: slice(p * PAIR, (p + 1) * PAIR)
        gms = [_dot_nt(jnp.concatenate([at16[u, d][:, psl(p)], rt[u, d][:, psl(p)].astype(bf16)], axis=0),
                       jnp.concatenate([_pair_blockdiag(bt16[u, d][:, psl(p)]),
                                        _pair_blockdiag(kt16[u, d][:, psl(p)])], axis=0)) for u, d, p in insts]
        lab = [jnp.where(strict[d], g[:CHUNK, :PAIR], 0.0) for (u, d, p), g in zip(insts, gms)]
        lakrk16 = [jnp.concatenate([jnp.where(strict[d], g[:CHUNK, PAIR:], 0.0),
                                    jnp.where(incl[d], g[CHUNK:, PAIR:], 0.0)], axis=0).astype(bf16)
                   for (u, d, p), g in zip(insts, gms)]
        lrb16 = [jnp.where(incl[d], g[CHUNK:, :PAIR], 0.0).astype(bf16) for (u, d, p), g in zip(insts, gms)]
        uv = [_dot(lk, _pair_blockdiag(v16_c[u][:, psl(p)])) for (u, d, p), lk in zip(insts, lakrk16)]
        vtk = [_dot_tn(v16_c[u][:, psl(p)], kt16[u, d][:, psl(p)]) for u, d, p in insts]
        def split(t):
            hi = t.astype(bf16)
            return hi, (t - hi.astype(f32)).astype(bf16)

        def mul_x3(a, b):
            a_hi, a_lo = split(a)
            b_hi, b_lo = split(b)
            w = b.shape[1]
            full = _dot(a_hi, _pair_blockdiag(jnp.concatenate([b_hi, b_lo], axis=1)))
            return full[:, :w] + full[:, w:] + _dot(a_lo, _pair_blockdiag(b_hi))

        base_d = [jnp.where(same_block[INV_BASE], l, 0.0) for l in lab]
        n_stage = int(math.log2(INV_BASE))
        x = [eye + l for l in base_d]
        q = [mul_x3(l, l) for l in base_d]
        for stage in range(1, n_stage):
            if stage < n_stage - 1:
                prod = [mul_x3(jnp.concatenate([xx, qq], axis=0), qq) for xx, qq in zip(x, q)]
                x = [xx + pr[:CHUNK] for xx, pr in zip(x, prod)]
                q = [pr[CHUNK:] for pr in prod]
            else:
                x = [xx + mul_x3(xx, qq) for xx, qq in zip(x, q)]
        size = INV_BASE
        while size < CHUNK:
            cross = [jnp.where(jnp.logical_and(same_block[2 * size], jnp.logical_not(same_block[size])), l, 0.0)
                     for l in lab]
            cx = [mul_x3(cr, xx) for cr, xx in zip(cross, x)]
            x = [xx + mul_x3(xx, c_) for xx, c_ in zip(x, cx)]
            size *= 2
        z0 = [jnp.concatenate([at16[u, d][:, psl(p)].astype(f32), lv[:CHUNK]], axis=1)
              for (u, d, p), lv in zip(insts, uv)]
        au16 = [mul_x3(xx, zz).astype(bf16) for xx, zz in zip(x, z0)]
        ry = [_dot(lr, _pair_blockdiag(au)) for lr, au in zip(lrb16, au16)]
        mn = [_dot_tn(au, bt16[u, d][:, psl(p)]) for (u, d, p), au in zip(insts, au16)]
        for i, (u, d, p) in enumerate(insts):
            ps = psl(p)
            rp_ref[d, rss[u], ps] = (rt[u, d][:, ps] + ry[i][:, :PAIR]).astype(bf16)
            if d == 1:
                j = insts.index((u, 0, p))
                y0_ref[rss[u], ps] = (ry[j][:, PAIR:] + uv[j][CHUNK:]) + (ry[i][:, PAIR:] + uv[i][CHUNK:])
            wc_p = wc[u, d][:, ps]
            m_pair = jnp.where(low, mn[i][0:CHUNK], mn[i][CHUNK:2 * CHUNK])
            n_pair = jnp.where(low, mn[i][2 * CHUNK:3 * CHUNK], mn[i][3 * CHUNK:])
            k_pair = jnp.where(low, vtk[i][:CHUNK], vtk[i][CHUNK:])
            mm_ref[d, cs[u], p] = (eye + m_pair) * wc_p
            nn_ref[d, cs[u], p] = (n_pair + k_pair) * wc_p
        return carry

    lax.fori_loop(0, TM // (CHUNK * CHUNKS_PER_BODY), chunk_body, 0)


def _rwkv_chunks(fr, mu, w0, w2, a0, a2, g2, kk, ka, rk, nbc, nbs):
    m = fr.shape[0]
    nct = TM // CHUNK
    prev, nxt = _halo_specs(RWKV_IN, m)
    c2 = lambda i: (0, 0)
    c3 = lambda i: (0, 0, 0)
    row = lambda i: (i, 0)
    dirrow = lambda i: (0, i, 0)
    ops = lambda i: (0, i, 0, 0, 0)
    return pl.pallas_call(
        functools.partial(_rwkv_chunk_kernel, nbc=nbc, nbs=nbs),
        grid=(m // TM,),
        in_specs=[pl.BlockSpec((TM, RWKV_IN), row), prev, nxt,
                  pl.BlockSpec((2, RWKV_IN), c2), pl.BlockSpec((2, D_RWKV), c2),
                  pl.BlockSpec((2, DECAY_LORA, D_RWKV), c3), pl.BlockSpec((2, D_RWKV), c2),
                  pl.BlockSpec((2, AAA_LORA, D_RWKV), c3), pl.BlockSpec((GATE_LORA, D_RWKV), c2),
                  pl.BlockSpec((1, D_RWKV), c2), pl.BlockSpec((1, D_RWKV), c2), pl.BlockSpec((1, D_RWKV), c2)],
        out_specs=[pl.BlockSpec((2, TM, D_RWKV), dirrow), pl.BlockSpec((TM, D_RWKV), row),
                   pl.BlockSpec((2, nct, N_PAIR, HEAD_DIM, PAIR), ops),
                   pl.BlockSpec((2, nct, N_PAIR, HEAD_DIM, PAIR), ops),
                   pl.BlockSpec((TM, D_RWKV), row), pl.BlockSpec((TM, D_RWKV), row)],
        out_shape=[jax.ShapeDtypeStruct((2, m, D_RWKV), bf16), jax.ShapeDtypeStruct((m, D_RWKV), f32),
                   jax.ShapeDtypeStruct((2, m // CHUNK, N_PAIR, HEAD_DIM, PAIR), f32),
                   jax.ShapeDtypeStruct((2, m // CHUNK, N_PAIR, HEAD_DIM, PAIR), f32),
                   jax.ShapeDtypeStruct((m, D_RWKV), f32), jax.ShapeDtypeStruct((m, D_RWKV), f32)],
        scratch_shapes=[pltpu.VMEM((TM, D_RWKV), f32), pltpu.VMEM((TM, D_RWKV), f32), pltpu.VMEM((TM, D_RWKV), f32),
                        pltpu.VMEM((2, TM, D_RWKV), f32), pltpu.VMEM((2, TM, D_RWKV), f32),
                        pltpu.VMEM((2, TM, D_RWKV), f32)],
        compiler_params=_cparams(1),
        name="rwkv_chunks",
    )(fr, fr, fr, mu, w0, w2, a0, a2, g2, kk.reshape(1, D_RWKV), ka.reshape(1, D_RWKV), rk.reshape(1, D_RWKV))


def _rwkv_state_kernel(mf_ref, mb_ref, nf_ref, nb_ref, sf_ref, sb_ref, st_ref, *, nbatch):
    @pl.when(pl.program_id(0) == 0)
    def _():
        st_ref[...] = jnp.zeros(st_ref.shape, f32)

    group = 8
    probs = [(d, b, p) for b in range(nbatch) for d in range(2) for p in range(N_PAIR)]
    for sub in range(STATE_STEPS):
        pos = (sub, STATE_STEPS - 1 - sub)
        for g0 in range(0, len(probs), group):
            grp = probs[g0:g0 + group]
            st = [st_ref[d, b, p] for d, b, p in grp]
            for (d, b, p), s in zip(grp, st):
                (sf_ref if d == 0 else sb_ref)[b, pos[d], p] = s.astype(bf16)
            mt = [(mf_ref if d == 0 else mb_ref)[0, b, pos[d], p] for d, b, p in grp]
            s1 = [s.astype(bf16) for s in st]
            s2 = [(s - a.astype(f32)).astype(bf16) for s, a in zip(st, s1)]
            m1 = [x.astype(bf16) for x in mt]
            m2 = [(x - a.astype(f32)).astype(bf16) for x, a in zip(mt, m1)]
            p12 = [_dot(a1, _pair_blockdiag(jnp.concatenate([b1, b2], axis=1))) for a1, b1, b2 in zip(s1, m1, m2)]
            p21 = [_dot(a2, _pair_blockdiag(b1)) for a2, b1 in zip(s2, m1)]
            for (d, b, p), x12, x21 in zip(grp, p12, p21):
                st_ref[d, b, p] = ((x12[:, :PAIR] + x12[:, PAIR:] + x21)
                                   + (nf_ref if d == 0 else nb_ref)[0, b, pos[d], p])


def _rwkv_state(mm, nn, nbatch, nca, ncc):
    assert nca % STATE_STEPS == 0 and ncc % STATE_STEPS == 0
    shp = (2, nbatch, nca, N_PAIR, HEAD_DIM, PAIR)
    blk = (1, nbatch, STATE_STEPS, N_PAIR, HEAD_DIM, PAIR)
    nga, ngc = nca // STATE_STEPS, ncc // STATE_STEPS
    c_bwd = lambda s: jnp.where(s < ngc, ngc - 1 - s, nga - 1 - (s - ngc))
    fwd = pl.BlockSpec(blk, lambda s: (0, 0, s, 0, 0, 0))
    bwd = pl.BlockSpec(blk, lambda s: (1, 0, c_bwd(s), 0, 0, 0))
    oshape = jax.ShapeDtypeStruct(shp[1:], bf16)
    return pl.pallas_call(
        functools.partial(_rwkv_state_kernel, nbatch=nbatch),
        grid=(nga,),
        in_specs=[fwd, bwd, fwd, bwd],
        out_specs=[pl.BlockSpec(blk[1:], lambda s: (0, s, 0, 0, 0)),
                   pl.BlockSpec(blk[1:], lambda s: (0, c_bwd(s), 0, 0, 0))],
        out_shape=[oshape, oshape],
        scratch_shapes=[pltpu.VMEM((2, nbatch, N_PAIR, HEAD_DIM, PAIR), f32)],
        compiler_params=_cparams(1),
        name="rwkv_state",
    )(mm.reshape(shp), mm.reshape(shp), nn.reshape(shp), nn.reshape(shp))


def _rwkv_readout(rp_ref, y0_ref, sf_ref, sb_ref, bv_ref, gg_ref, lng_ref, lnb_ref, n_rows):
    rows = []
    for c in range(n_rows // CHUNK):
        rs = slice(c * CHUNK, (c + 1) * CHUNK)
        cols = []
        for p in range(N_PAIR):
            ps = slice(p * PAIR, (p + 1) * PAIR)
            lhs = jnp.concatenate([rp_ref[0, rs, ps], rp_ref[1, rs, ps]], axis=1).astype(bf16)
            rhs = jnp.concatenate([_pair_blockdiag(sf_ref[c, p].astype(bf16)),
                                   _pair_blockdiag(sb_ref[c, p].astype(bf16))], axis=1)
            cols.append(y0_ref[rs, ps] + _dot_nt(lhs, rhs))
        rows.append(jnp.concatenate(cols, axis=1))
    y = jnp.concatenate(rows, axis=0)
    ones_bd = _head_ones(D_RWKV, HEAD_DIM)
    mu = _dot_exact_rhs(y, ones_bd) * (1.0 / HEAD_DIM)
    yc = y - mu
    var = _dot_exact_rhs(yc * yc, ones_bd) * (1.0 / HEAD_DIM)
    yn = yc * lax.rsqrt(var + GN_EPS) * lng_ref[...] + lnb_ref[...]
    return (yn + bv_ref[...]) * gg_ref[...]


def _mixed_residual(t_ref, rp_ref, y0_ref, sf_ref, sb_ref, bv_ref, gg_ref, lng_ref, lnb_ref, b_ref, c_ref, w_ref, *, gate,
                    rows):
    a = _rwkv_readout(rp_ref, y0_ref, sf_ref, sb_ref, bv_ref, gg_ref, lng_ref, lnb_ref, rows)
    out = _dot(a.astype(bf16), w_ref[0:D_RWKV, :])
    out = out + _dot(b_ref[...].astype(bf16), w_ref[D_RWKV:D_RWKV + D_CONV, :])
    out = out + _dot(c_ref[...].astype(bf16), w_ref[D_RWKV + D_CONV:, :])
    return t_ref[...] + gate * out


def _mixout_kernel(*refs):
    m_ref, o_ref = refs[-2:]
    o_ref[...] = _mixed_residual(*refs[:-2], gate=m_ref[0, 5:6, :], rows=TM)


def _mixout(t, rwkv_parts, ln_g, ln_b, ob, oc, w, layer, mods, grid, row_map, mod_map, c_map, out_rows):
    rp, y0, sf, sb, bv, gg = rwkv_parts
    d = t.shape[1]
    nct = TM // CHUNK
    sshape = (bv.shape[0] // CHUNK, N_PAIR, HEAD_DIM, PAIR)
    n_axes = len(grid)
    if n_axes == 1:
        out_map = lambda i: (i, 0)
    else:
        nj = grid[1]
        out_map = lambda b, j: (b * nj + j, 0)
    rmap = lambda *a: (row_map(*a), 0)
    dirrow = lambda *a: (0, row_map(*a), 0)
    st = lambda *a: (row_map(*a), 0, 0, 0)
    c2 = lambda *a: (0, 0)
    return pl.pallas_call(
        _mixout_kernel,
        grid=grid,
        in_specs=[pl.BlockSpec((TM, d), rmap),
                  pl.BlockSpec((2, TM, D_RWKV), dirrow), pl.BlockSpec((TM, D_RWKV), rmap),
                  pl.BlockSpec((nct, N_PAIR, HEAD_DIM, PAIR), st), pl.BlockSpec((nct, N_PAIR, HEAD_DIM, PAIR), st),
                  pl.BlockSpec((TM, D_RWKV), rmap), pl.BlockSpec((TM, D_RWKV), rmap),
                  pl.BlockSpec((1, D_RWKV), c2), pl.BlockSpec((1, D_RWKV), c2),
                  pl.BlockSpec((TM, D_CONV), rmap),
                  pl.BlockSpec((TM, D_DIFF), lambda *a: (c_map(*a), 0)),
                  pl.BlockSpec((None, D_RWKV + D_CONV + D_DIFF, d), lambda *a: (layer, 0, 0)),
                  pl.BlockSpec((1, N_MOD, d), lambda *a: (mod_map(*a), 0, 0))],
        out_specs=pl.BlockSpec((TM, d), out_map),
        out_shape=jax.ShapeDtypeStruct((out_rows, d), f32),
        compiler_params=_cparams(n_axes),
        name="mixout",
    )(t, rp, y0, sf.reshape(sshape), sb.reshape(sshape), bv, gg, ln_g.reshape(1, D_RWKV), ln_b.reshape(1, D_RWKV),
      ob, oc, w, mods)


def _rope_tables(n_ctx, n_lat):
    n_rows = n_lat // GRID_W
    row = jnp.repeat(jnp.arange(n_rows, dtype=jnp.int32), GRID_W)
    col = jnp.tile(jnp.arange(GRID_W, dtype=jnp.int32), n_rows)
    inv = 1.0 / (ROPE_THETA ** (jnp.arange(ROPE_FREQS, dtype=f32) * 2.0 / AXIS_DIM))
    ang = jnp.stack([row, col], axis=-1).astype(f32)[..., None] * inv
    cos, sin = jnp.cos(ang), jnp.sin(ang)
    cos64 = jnp.concatenate([cos[:, 0], cos[:, 0], cos[:, 1], cos[:, 1]], axis=-1)
    sin64 = jnp.concatenate([-sin[:, 0], sin[:, 0], -sin[:, 1], sin[:, 1]], axis=-1)
    reps = D_QK // DIFF_QK_DIM
    cos_t = jnp.concatenate([jnp.ones((n_ctx, D_QK), f32), jnp.tile(cos64, (1, reps))], axis=0)
    sin_t = jnp.concatenate([jnp.zeros((n_ctx, D_QK), f32), jnp.tile(sin64, (1, reps))], axis=0)
    return cos_t, sin_t


def kernel(x, c, ctx, c_ctx, ada_w, ada_b, norm_g, ffn_w_in, ffn_w_out, mix_w_in, mix_w_out, rwkv_mu, rwkv_w0, rwkv_w2,
           rwkv_a0, rwkv_a2, rwkv_g2, rwkv_kk, rwkv_ka, rwkv_rk, rwkv_ln_g, rwkv_ln_b, conv_dw_w, conv_dw_b, conv_ln_g,
           conv_ln_b, diff_lam, diff_norm_g, final_g):
    nb, n_lat, d = x.shape
    n_ctx = ctx.shape[1]
    depth = ada_w.shape[0]
    seq = n_ctx + n_lat
    assert n_ctx % TM == 0 and n_lat % TM == 0 and n_lat % GRID_W == 0 and seq % n_ctx == 0
    assert nb + 1 <= COND_ROWS
    nbc, nbx, nbs = n_ctx // TM, n_lat // TM, seq // TM
    m = nb * seq

    if n_ctx == TM and _dense_tile(seq) == 3 * TM:
        t = (x.reshape(nb * n_lat, d), ctx.reshape(nb * n_ctx, d))
    else:
        t = jnp.concatenate([ctx, x], axis=1).reshape(m, d)
    cond =jnp.zeros((COND_ROWS, d), f32).at[:nb].set(c).at[nb].set(c_ctx)
    mods = _adaln(cond, ada_w, ada_b).reshape(depth, COND_ROWS, N_MOD, d)
    cos_t, sin_t = _rope_tables(n_ctx, n_lat)
    w_in16 = ffn_w_in.astype(bf16)
    w_out16 = ffn_w_out.astype(bf16)
    mix_in16 = mix_w_in.astype(bf16)
    mix_out16 = mix_w_out.astype(bf16)

    mod_all = lambda i: jnp.where(i % nbs < nbc, nb, i // nbs)
    row_all = lambda i: i
    row_lat = lambda b, j: b * nbs + nbc + j
    mod_lat = lambda b, j: b

    out = None
    for l in range(depth):
        last = l == depth - 1
        ml = mods[l]
        t = _ffn(t, ml, norm_g[l, 0], w_in16, w_out16, l, 0, 0, seq, n_ctx, nb)
        fr, hc, q, k, v = _mixin(t, ml, norm_g[l, 1], mix_in16, l, cos_t, sin_t, seq, n_ctx, nb)
        rp, y0, mm, nn, bv, gg = _rwkv_chunks(fr, rwkv_mu[l], rwkv_w0[l], rwkv_w2[l], rwkv_a0[l], rwkv_a2[l], rwkv_g2[l],
                                              rwkv_kk[l], rwkv_ka[l], rwkv_rk[l].reshape(D_RWKV), nbc, nbs)
        sf, sb = _rwkv_state(mm, nn, nb, seq // CHUNK, n_ctx // CHUNK)
        rwkv_parts = (rp, y0, sf, sb, bv, gg)
        ob = _conv(hc, conv_dw_w[l], conv_dw_b[l], conv_ln_g[l], conv_ln_b[l], nbc, nbs)
        lam_init = 0.8 - 0.6 * math.exp(-0.3 * l)
        if last:
            oc = _attn(q, k, v, diff_lam[l], diff_norm_g[l], lam_init, nb, n_ctx, n_lat, False)
            t = _mixout(t, rwkv_parts, rwkv_ln_g[l], rwkv_ln_b[l], ob, oc, mix_out16, l, ml, (nb, nbx), row_lat,
                        mod_lat, lambda b, j: b * nbx + j, nb * n_lat)
            out = _ffn(t, ml, norm_g[l, 2], w_in16, w_out16, l, 1, 6, n_lat, 0, nb, final_g=final_g)
        else:
            oc = _attn(q, k, v, diff_lam[l], diff_norm_g[l], lam_init, nb, n_ctx, n_lat, True)
            t = _mixout(t, rwkv_parts, rwkv_ln_g[l], rwkv_ln_b[l], ob, oc, mix_out16, l, ml, (nbs * nb,), row_all,
                        mod_all, row_all, m)
            t = _ffn(t, ml, norm_g[l, 2], w_in16, w_out16, l, 1, 6, seq, n_ctx, nb)
    return out.reshape(nb, n_lat, d)
```
